```python
import jax, jax.numpy as jnp
from jax import lax
import numpy as np

D_MODEL = 1024
BATCH = 16
SEQ = 2048
DEPTH = 2

CHUNK = 64
Q_BLOCK = 128
MLA_HEADS = 8
MLA_NOPE_DIM = 64
MLA_ROPE_DIM = 32
MLA_V_DIM = 64
MLA_Q_RANK = 256
MLA_KV_RANK = 128
ROPE_THETA = 10000.0
FOX_HEADS = 8
FOX_HEAD_DIM = 64
MLA_OUT = MLA_HEADS * MLA_V_DIM
FOX_OUT = FOX_HEADS * FOX_HEAD_DIM
MIX_WIDTH = MLA_OUT + FOX_OUT
IN_WIDTH = MLA_Q_RANK + MLA_KV_RANK + MLA_ROPE_DIM + 3 * FOX_OUT + FOX_HEADS
N_GROUPS = 4
EXPERTS_PER_GROUP = 8
N_EXPERTS = N_GROUPS * EXPERTS_PER_GROUP
EXPERT_TOP_K = 2
D_EXPERT = 512
EXPERT_BLOCK = 128
NORM_EPS = 1e-6

kernel_name = "hymba_mla_fox_hiermoe_trunk"


def rmsnorm(x, g):
    xf = x.astype(jnp.float32)
    y = xf * lax.rsqrt(jnp.mean(xf * xf, axis=-1, keepdims=True) + NORM_EPS)
    return (y * g.astype(jnp.float32)).astype(x.dtype)


def rope_tables(positions):
    half = MLA_ROPE_DIM // 2
    inv_freq = ROPE_THETA ** (-jnp.arange(half, dtype=jnp.float32) / half)
    ang = positions.astype(jnp.float32)[..., None] * inv_freq
    return jnp.cos(ang), jnp.sin(ang)


def apply_rope(x, cos, sin):
    half = x.shape[-1] // 2
    xf = x.astype(jnp.float32)
    x1, x2 = xf[..., :half], xf[..., half:]
    return jnp.concatenate([x1 * cos - x2 * sin, x1 * sin + x2 * cos], axis=-1).astype(x.dtype)


def block_sweep_attention(q, k, v, scale, chunk_causal, log_decay_cum=None):
    S = q.shape[2]
    outs = []
    for i in range(S // Q_BLOCK):
        qs, qe = i * Q_BLOCK, (i + 1) * Q_BLOCK
        s = jnp.einsum('bhqd,bhkd->bhqk', q[:, :, qs:qe], k[:, :, :qe],
                       preferred_element_type=jnp.float32) * scale
        t_idx = jnp.arange(qs, qe)[:, None]
        s_idx = jnp.arange(qe)[None, :]
        if chunk_causal:
            allowed = (s_idx // CHUNK) <= (t_idx // CHUNK)
        else:
            allowed = s_idx <= t_idx
        if log_decay_cum is not None:
            s = s + (log_decay_cum[:, :, qs:qe, None] - log_decay_cum[:, :, None, :qe])
        s = jnp.where(allowed, s, -jnp.inf)
        p = jax.nn.softmax(s, axis=-1).astype(v.dtype)
        outs.append(jnp.einsum('bhqk,bhkd->bhqd', p, v[:, :, :qe]))
    return jnp.concatenate(outs, axis=2)


def mla_group(q_lat, kv_lat, k_rope, positions, q_norm, w_uq, kv_norm, w_ukv):
    B, S, _ = q_lat.shape
    q = (rmsnorm(q_lat, q_norm) @ w_uq).reshape(B, S, MLA_HEADS, MLA_NOPE_DIM + MLA_ROPE_DIM)
    q = q.transpose(0, 2, 1, 3)
    kv = (rmsnorm(kv_lat, kv_norm) @ w_ukv).reshape(B, S, MLA_HEADS, MLA_NOPE_DIM + MLA_V_DIM)
    kv = kv.transpose(0, 2, 1, 3)
    k_nope, v = kv[..., :MLA_NOPE_DIM], kv[..., MLA_NOPE_DIM:]
    cos, sin = rope_tables(positions)
    q_pe = apply_rope(q[..., MLA_NOPE_DIM:], cos[:, None], sin[:, None])
    k_pe = apply_rope(k_rope, cos, sin)[:, None]
    q = jnp.concatenate([q[..., :MLA_NOPE_DIM], q_pe], axis=-1)
    k = jnp.concatenate([k_nope, jnp.broadcast_to(k_pe, (B, MLA_HEADS, S, MLA_ROPE_DIM))], axis=-1)
    o = block_sweep_attention(q, k, v, (MLA_NOPE_DIM + MLA_ROPE_DIM) ** -0.5, chunk_causal=True)
    return o.transpose(0, 2, 1, 3).reshape(B, S, MLA_OUT)


def fox_group(q, k, v, f_logit, b_forget):
    B, S, _ = q.shape
    def to_heads(t):
        return t.reshape(B, S, FOX_HEADS, FOX_HEAD_DIM).transpose(0, 2, 1, 3)
    log_f = jax.nn.log_sigmoid(f_logit.astype(jnp.float32) + b_forget.astype(jnp.float32))
    c = jnp.cumsum(log_f, axis=1).transpose(0, 2, 1)
    o = block_sweep_attention(to_heads(q), to_heads(k), to_heads(v), FOX_HEAD_DIM ** -0.5,
                              chunk_causal=False, log_decay_cum=c)
    return o.transpose(0, 2, 1, 3).reshape(B, S, FOX_OUT)


def hier_moe(h, w_rg, b_rg, w_re, b_re, w_gate, w_up, w_down):
    B, S, D = h.shape
    N = B * S
    t = h.reshape(N, D)
    g_prob = jax.nn.softmax((t @ w_rg).astype(jnp.float32) + b_rg.astype(jnp.float32), axis=-1)
    g_val, g_idx = lax.top_k(g_prob, 1)
    e_logits = ((t @ w_re).astype(jnp.float32) + b_re.astype(jnp.float32)).reshape(N, N_GROUPS, EXPERTS_PER_GROUP)
    idx = jnp.broadcast_to(g_idx[:, :, None], (N, 1, EXPERTS_PER_GROUP))
    e_prob = jax.nn.softmax(jnp.take_along_axis(e_logits, idx, axis=1)[:, 0], axis=-1)
    e_val, e_local = lax.top_k(e_prob, EXPERT_TOP_K)
    gates = g_val * e_val / jnp.sum(e_val, axis=-1, keepdims=True)
    expert_ids = g_idx * EXPERTS_PER_GROUP + e_local
    A = N * EXPERT_TOP_K
    eid = expert_ids.reshape(A).astype(jnp.int32)
    tok = jnp.repeat(jnp.arange(N, dtype=jnp.int32), EXPERT_TOP_K)
    gate = gates.reshape(A)
    order = jnp.argsort(eid)
    eid_s, tok_s, gate_s = eid[order], tok[order], gate[order]
    counts = jax.ops.segment_sum(jnp.ones((A,), jnp.int32), eid, num_segments=N_EXPERTS)
    start = jnp.cumsum(counts) - counts
    padded = (counts + EXPERT_BLOCK - 1) // EXPERT_BLOCK * EXPERT_BLOCK
    pad_end = jnp.cumsum(padded)
    pad_start = pad_end - padded
    dest = pad_start[eid_s] + (jnp.arange(A, dtype=jnp.int32) - start[eid_s])
    n_blocks = -(-A // EXPERT_BLOCK) + N_EXPERTS
    buf = jnp.zeros((n_blocks * EXPERT_BLOCK, D), t.dtype).at[dest].set(t[tok_s])
    blk_start = jnp.arange(n_blocks, dtype=jnp.int32) * EXPERT_BLOCK
    blk_expert = jnp.clip(jnp.searchsorted(pad_end, blk_start, side='right'), 0, N_EXPERTS - 1)

    def expert_block(args):
        xb, e = args
        return (jax.nn.silu(xb @ w_gate[e]) * (xb @ w_up[e])) @ w_down[e]

    out = lax.map(expert_block, (buf.reshape(n_blocks, EXPERT_BLOCK, D), blk_expert))
    y_s = out.reshape(n_blocks * EXPERT_BLOCK, D)[dest] * gate_s[:, None].astype(t.dtype)
    y = jnp.zeros((N, D), t.dtype).at[tok_s].add(y_s)
    return y.reshape(B, S, D)


def setup_inputs(seed: int = 0) -> dict:
    key = jax.random.key(seed)
    ks = jax.random.split(key, 24)
    f32 = jnp.float32
    L, D = DEPTH, D_MODEL

    def nrm(k, shape, scale):
        return jax.random.normal(k, shape, f32) * scale

    def gain(k, shape):
        return 1.0 + 0.02 * jax.random.normal(k, shape, f32)

    x = jax.random.normal(ks[0], (BATCH, SEQ, D), f32)
    offsets = jax.random.randint(ks[1], (BATCH, 1), 0, 1000, dtype=jnp.int32) * CHUNK
    positions = (offsets + jnp.arange(SEQ, dtype=jnp.int32)[None, :]).astype(jnp.int32)
    return {
        "x": x,
        "positions": positions,
        "attn_norm": gain(ks[2], (L, D)),
        "w_in": nrm(ks[3], (L, D, IN_WIDTH), D ** -0.5),
        "b_forget": 2.0 + 0.1 * jax.random.normal(ks[4], (L, FOX_HEADS), f32),
        "q_norm": gain(ks[5], (L, MLA_Q_RANK)),
        "w_uq": nrm(ks[6], (L, MLA_Q_RANK, MLA_HEADS * (MLA_NOPE_DIM + MLA_ROPE_DIM)), MLA_Q_RANK ** -0.5),
        "kv_norm": gain(ks[7], (L, MLA_KV_RANK)),
        "w_ukv": nrm(ks[8], (L, MLA_KV_RANK, MLA_HEADS * (MLA_NOPE_DIM + MLA_V_DIM)), MLA_KV_RANK ** -0.5),
        "mla_out_norm": gain(ks[9], (L, MLA_OUT)),
        "fox_out_norm": gain(ks[10], (L, FOX_OUT)),
        "w_out": nrm(ks[11], (L, MIX_WIDTH, D), MIX_WIDTH ** -0.5),
        "ffn_norm": gain(ks[12], (L, D)),
        "w_router_group": nrm(ks[13], (L, D, N_GROUPS), D ** -0.5),
        "b_router_group": nrm(ks[14], (L, N_GROUPS), 0.01),
        "w_router_expert": nrm(ks[15], (L, D, N_EXPERTS), D ** -0.5),
        "b_router_expert": nrm(ks[16], (L, N_EXPERTS), 0.01),
        "w_gate": nrm(ks[17], (L, N_EXPERTS, D, D_EXPERT), D ** -0.5),
        "w_up": nrm(ks[18], (L, N_EXPERTS, D, D_EXPERT), D ** -0.5),
        "w_down": nrm(ks[19], (L, N_EXPERTS, D_EXPERT, D), D_EXPERT ** -0.5),
        "final_norm": gain(ks[20], (D,)),
    }


def reference(x, positions, attn_norm, w_in, b_forget, q_norm, w_uq, kv_norm, w_ukv,
              mla_out_norm, fox_out_norm, w_out, ffn_norm, w_router_group, b_router_group,
              w_router_expert, b_router_expert, w_gate, w_up, w_down, final_norm):
    split_at = np.cumsum([MLA_Q_RANK, MLA_KV_RANK, MLA_ROPE_DIM, FOX_OUT, FOX_OUT, FOX_OUT]).tolist()
    for l in range(DEPTH):
        h = rmsnorm(x, attn_norm[l])
        proj = h @ w_in[l]
        q_lat, kv_lat, k_rope, fq, fk, fv, f_logit = jnp.split(proj, split_at, axis=-1)
        o_mla = mla_group(q_lat, kv_lat, k_rope, positions, q_norm[l], w_uq[l], kv_norm[l], w_ukv[l])
        o_fox = fox_group(fq, fk, fv, f_logit, b_forget[l])
        mixed = jnp.concatenate([rmsnorm(o_mla, mla_out_norm[l]), rmsnorm(o_fox, fox_out_norm[l])], axis=-1)
        x = x + mixed @ w_out[l]
        h2 = rmsnorm(x, ffn_norm[l])
        x = x + hier_moe(h2, w_router_group[l], b_router_group[l], w_router_expert[l],
                         b_router_expert[l], w_gate[l], w_up[l], w_down[l])
    return rmsnorm(x, final_norm)
```

```python
import functools

import jax
import jax.numpy as jnp
from jax import lax
from jax.experimental import pallas as pl
from jax.experimental.pallas import tpu as pltpu

D_MODEL = 1024
CHUNK = 64
MLA_HEADS = 8
MLA_NOPE = 64
MLA_ROPE = 32
MLA_V = 64
MLA_Q_RANK = 256
MLA_KV_RANK = 128
ROPE_THETA = 10000.0
FOX_HEADS = 8
FOX_DIM = 64
HEAD_W = 512
N_GROUPS = 4
EXPERTS_PER_GROUP = 8
N_EXPERTS = 32
TOP_K = 2
D_EXPERT = 512
NORM_EPS = 1e-6

LANES = 128
IN_COLS = 2176
ROW_TILE = 512
Q_TILE = 256
EXPERT_TILE = 256
CUM_TILE = 256
NEG_BIG = -1e30
VMEM_LIMIT = 48 * 1024 * 1024

F32 = jnp.float32
BF16 = jnp.bfloat16


def _rms(x, g):
    return (x * lax.rsqrt(jnp.mean(x * x, axis=-1, keepdims=True) + NORM_EPS)) * g


def _dot(a, b):
    return jnp.dot(a, b, preferred_element_type=F32)


def _dot_nt(a, b):
    return lax.dot_general(a, b, (((1,), (1,)), ((), ())), preferred_element_type=F32)


def _params(sem):
    return pltpu.CompilerParams(dimension_semantics=sem, vmem_limit_bytes=VMEM_LIMIT)


def _in_proj_kernel(x_ref, g_ref, w_ref, gq_ref, wuq_ref, gkv_ref, wukv_ref, cc_ref, ss_ref,
                    fq_ref, fk_ref, fv_ref, qn_ref, qpe_ref, kv_ref, kpe_ref, flt_ref):
    h = _rms(x_ref[...], g_ref[...])
    p = _dot(h.astype(BF16), w_ref[...])
    fq_ref[...] = p[:, 0:512].astype(BF16)
    fk_ref[...] = p[:, 512:1024].astype(BF16)
    fv_ref[...] = p[:, 1024:1536].astype(BF16)

    cc = cc_ref[...]
    ss = ss_ref[...]
    scale = (MLA_NOPE + MLA_ROPE) ** -0.5
    q = _dot(_rms(p[:, 1536:1792], gq_ref[...]).astype(BF16), wuq_ref[...])
    qn_ref[...] = (q[:, 0:512] * scale).astype(BF16)
    cc2 = jnp.concatenate([cc, cc], axis=1)
    ss2 = jnp.concatenate([ss, ss], axis=1)
    qpe_ref[...] = ((q[:, 512:768] * cc2 + q[:, 768:1024] * ss2) * scale).astype(BF16)

    kv_ref[...] = _dot(_rms(p[:, 1792:1920], gkv_ref[...]).astype(BF16), wukv_ref[...]).astype(BF16)

    slab_a = p[:, 1920:2048]
    slab_b = p[:, 2048:2176]
    lane = lax.broadcasted_iota(jnp.int32, slab_a.shape, 1)
    roped = jnp.where(lane < MLA_ROPE, slab_a * cc + slab_b * ss, 0.0)
    tiled = roped + pltpu.roll(roped, 32, 1) + pltpu.roll(roped, 64, 1) + pltpu.roll(roped, 96, 1)
    kpe_ref[...] = tiled.astype(BF16)
    flt_ref[...] = slab_a.T[MLA_ROPE:MLA_ROPE + FOX_HEADS, :]


def _in_proj(x2d, g, w, gq, wuq, gkv, wukv, cc, ss):
    n = x2d.shape[0]
    rows = min(ROW_TILE, n)
    row = lambda c: pl.BlockSpec((rows, c), lambda i: (i, 0))
    full = lambda a: pl.BlockSpec(a.shape, lambda i: (0,) * a.ndim)
    out_shape = (
        jax.ShapeDtypeStruct((n, HEAD_W), BF16), jax.ShapeDtypeStruct((n, HEAD_W), BF16),
        jax.ShapeDtypeStruct((n, HEAD_W), BF16), jax.ShapeDtypeStruct((n, HEAD_W), BF16),
        jax.ShapeDtypeStruct((n, 256), BF16), jax.ShapeDtypeStruct((n, 1024), BF16),
        jax.ShapeDtypeStruct((n, LANES), BF16), jax.ShapeDtypeStruct((FOX_HEADS, n), F32),
    )
    return pl.pallas_call(
        _in_proj_kernel,
        grid=(n // rows,),
        in_specs=[row(D_MODEL), full(g), full(w), full(gq), full(wuq), full(gkv), full(wukv),
                  row(LANES), row(LANES)],
        out_specs=(row(HEAD_W), row(HEAD_W), row(HEAD_W), row(HEAD_W), row(256), row(1024), row(LANES),
                   pl.BlockSpec((FOX_HEADS, rows), lambda i: (0, i))),
        out_shape=out_shape,
        compiler_params=_params(("parallel",)),
        name="in_proj",
    )(x2d, g, w, gq, wuq, gkv, wukv, cc, ss)


def _fox_decay_kernel(fl_ref, b_ref, tri_ref, negc_ref):
    z = fl_ref[...] + b_ref[...]
    lf = jnp.minimum(z, 0.0) - jnp.log1p(jnp.exp(-jnp.abs(z)))
    seq = lf.shape[1]
    tri = tri_ref[...]
    carry = jnp.zeros((FOX_HEADS, 1), F32)
    zeros = jnp.zeros((FOX_HEADS, CUM_TILE), F32)
    for j in range(seq // CUM_TILE):
        v = lf[:, j * CUM_TILE:(j + 1) * CUM_TILE]
        hi = v.astype(BF16).astype(F32)
        r1 = v - hi
        mid = r1.astype(BF16).astype(F32)
        lo = r1 - mid
        parts = _dot(jnp.concatenate([hi, mid, lo, zeros], axis=0).astype(BF16), tri)
        cs = (parts[0:8] + parts[8:16]) + parts[16:24] + carry
        negc_ref[0, :, j * CUM_TILE:(j + 1) * CUM_TILE] = -cs
        carry = cs[:, CUM_TILE - 1:CUM_TILE]


def _fox_decay(flt, b_col, tri, batch, seq):
    return pl.pallas_call(
        _fox_decay_kernel,
        grid=(batch,),
        in_specs=[pl.BlockSpec((FOX_HEADS, seq), lambda b: (0, b)),
                  pl.BlockSpec((FOX_HEADS, 1), lambda b: (0, 0)),
                  pl.BlockSpec((CUM_TILE, CUM_TILE), lambda b: (0, 0))],
        out_specs=pl.BlockSpec((1, FOX_HEADS, seq), lambda b: (b, 0, 0)),
        out_shape=jax.ShapeDtypeStruct((batch, FOX_HEADS, seq), F32),
        compiler_params=_params(("parallel",)),
        name="fox_decay",
    )(flt, b_col, tri)


def _softmax_pv(s_off, s_diag, v_ref, qs, qe):
    m = jnp.max(s_diag, axis=-1, keepdims=True)
    if s_off is not None:
        m = jnp.maximum(m, jnp.max(s_off, axis=-1, keepdims=True))
    p_diag = jnp.exp(s_diag - m)
    l = jnp.sum(p_diag, axis=-1, keepdims=True)
    o = _dot(p_diag.astype(BF16), v_ref[qs:qe, :])
    if s_off is not None:
        p_off = jnp.exp(s_off - m)
        l = l + jnp.sum(p_off, axis=-1, keepdims=True)
        o = o + _dot(p_off.astype(BF16), v_ref[0:qs, :])
    return o / l


def _mla_attn_kernel(qn_ref, qpe_ref, kv_ref, kpe_ref, o_ref, qs_ref, ks_ref, vs_ref):
    j = pl.program_id(1)
    seq = qn_ref.shape[1]
    lane = lax.broadcasted_iota(jnp.int32, (seq, LANES), 1)
    qn = qn_ref[0]
    qpe = qpe_ref[0]
    kpe = kpe_ref[0]
    for hh in range(2):
        nope_mask = (lane >= 64) if hh == 0 else (lane < 64)
        pe_slot = 2 * (j % 2) + hh
        pe_mask = (lane // MLA_ROPE) == pe_slot
        qs_ref[hh, :, 0:LANES] = jnp.where(nope_mask, qn, jnp.zeros_like(qn))
        qs_ref[hh, :, LANES:2 * LANES] = jnp.where(pe_mask, qpe, jnp.zeros_like(qpe))
        kvh = kv_ref[0, :, hh * LANES:(hh + 1) * LANES]
        ks_ref[hh, :, 0:LANES] = kvh
        ks_ref[hh, :, LANES:2 * LANES] = kpe
        vs_ref[hh] = kvh

    row = lax.broadcasted_iota(jnp.int32, (Q_TILE, Q_TILE), 0)
    col = lax.broadcasted_iota(jnp.int32, (Q_TILE, Q_TILE), 1)
    allowed = (col // CHUNK) <= (row // CHUNK)
    out_lane = lax.broadcasted_iota(jnp.int32, (Q_TILE, LANES), 1)
    for i in range(seq // Q_TILE):
        qs, qe = i * Q_TILE, (i + 1) * Q_TILE
        outs = []
        for hh in range(2):
            q = qs_ref[hh, qs:qe, :]
            s_diag = jnp.where(allowed, _dot_nt(q, ks_ref[hh, qs:qe, :]), NEG_BIG)
            s_off = _dot_nt(q, ks_ref[hh, 0:qs, :]) if i > 0 else None
            outs.append(_softmax_pv(s_off, s_diag, vs_ref.at[hh], qs, qe))
        o_ref[0, qs:qe, :] = jnp.where(out_lane < 64, outs[0], outs[1]).astype(BF16)


def _mla_attn(qn, qpe, kv, kpe, batch, seq):
    qn, qpe, kv, kpe = (a.reshape(batch, seq, a.shape[-1]) for a in (qn, qpe, kv, kpe))
    return pl.pallas_call(
        _mla_attn_kernel,
        grid=(batch, MLA_HEADS // 2),
        in_specs=[pl.BlockSpec((1, seq, LANES), lambda b, j: (b, 0, j)),
                  pl.BlockSpec((1, seq, LANES), lambda b, j: (b, 0, j // 2)),
                  pl.BlockSpec((1, seq, 2 * LANES), lambda b, j: (b, 0, j)),
                  pl.BlockSpec((1, seq, LANES), lambda b, j: (b, 0, 0))],
        out_specs=pl.BlockSpec((1, seq, LANES), lambda b, j: (b, 0, j)),
        out_shape=jax.ShapeDtypeStruct((batch, seq, HEAD_W), BF16),
        scratch_shapes=[pltpu.VMEM((2, seq, 2 * LANES), BF16), pltpu.VMEM((2, seq, 2 * LANES), BF16),
                        pltpu.VMEM((2, seq, LANES), BF16)],
        compiler_params=_params(("parallel", "parallel")),
        name="mla_attn",
    )(qn, qpe, kv, kpe)


def _fox_attn_kernel(q_ref, k_ref, v_ref, negc_ref, o_ref, qs_ref):
    j = pl.program_id(1)
    seq = q_ref.shape[1]
    lane = lax.broadcasted_iota(jnp.int32, (seq, LANES), 1)
    q = q_ref[0]
    qs_ref[0] = jnp.where(lane < 64, q, jnp.zeros_like(q))
    qs_ref[1] = jnp.where(lane >= 64, q, jnp.zeros_like(q))

    row = lax.broadcasted_iota(jnp.int32, (Q_TILE, Q_TILE), 0)
    col = lax.broadcasted_iota(jnp.int32, (Q_TILE, Q_TILE), 1)
    allowed = col <= row
    out_lane = lax.broadcasted_iota(jnp.int32, (Q_TILE, LANES), 1)
    for i in range(seq // Q_TILE):
        qs, qe = i * Q_TILE, (i + 1) * Q_TILE
        outs = []
        for hh in range(2):
            negc = negc_ref[0, pl.ds(2 * j + hh, 1), :]
            qh = qs_ref[hh, qs:qe, :]
            s_diag = jnp.where(allowed, _dot_nt(qh, k_ref[0, qs:qe, :]) + negc[:, qs:qe], NEG_BIG)
            s_off = (_dot_nt(qh, k_ref[0, 0:qs, :]) + negc[:, 0:qs]) if i > 0 else None
            outs.append(_softmax_pv(s_off, s_diag, v_ref.at[0], qs, qe))
        o_ref[0, qs:qe, :] = jnp.where(out_lane < 64, outs[0], outs[1]).astype(BF16)


def _fox_attn(fq, fk, fv, negc, batch, seq):
    fq, fk, fv = (a.reshape(batch, seq, HEAD_W) for a in (fq, fk, fv))
    spec = pl.BlockSpec((1, seq, LANES), lambda b, j: (b, 0, j))
    return pl.pallas_call(
        _fox_attn_kernel,
        grid=(batch, FOX_HEADS // 2),
        in_specs=[spec, spec, spec, pl.BlockSpec((1, FOX_HEADS, seq), lambda b, j: (b, 0, 0))],
        out_specs=spec,
        out_shape=jax.ShapeDtypeStruct((batch, seq, HEAD_W), BF16),
        scratch_shapes=[pltpu.VMEM((2, seq, LANES), BF16)],
        compiler_params=_params(("parallel", "parallel")),
        name="fox_attn",
    )(fq, fk, fv, negc)


def _out_proj_kernel(om_ref, of_ref, x_ref, gm_ref, gf_ref, wo_ref, gn_ref, wr_ref, br_ref,
                     x1_ref, h2_ref, lg_ref):
    a = _rms(om_ref[...].astype(F32), gm_ref[...]).astype(BF16)
    b = _rms(of_ref[...].astype(F32), gf_ref[...]).astype(BF16)
    x1 = x_ref[...] + _dot(a, wo_ref[0:HEAD_W, :]) + _dot(b, wo_ref[HEAD_W:2 * HEAD_W, :])
    x1_ref[...] = x1
    h2 = _rms(x1, gn_ref[...])
    h_hi = h2.astype(BF16)
    h2_ref[...] = h_hi
    h_lo = (h2 - h_hi.astype(F32)).astype(BF16)
    t = _dot(h_hi, wr_ref[...])
    lg_ref[...] = (t[:, 0:LANES] + t[:, LANES:2 * LANES]) + _dot(h_lo, wr_ref[:, 0:LANES]) + br_ref[...]


def _out_proj(om, of, x2d, gm, gf, wo, gn, wr, br):
    n = x2d.shape[0]
    rows = min(ROW_TILE, n)
    row = lambda c: pl.BlockSpec((rows, c), lambda i: (i, 0))
    full = lambda a: pl.BlockSpec(a.shape, lambda i: (0,) * a.ndim)
    return pl.pallas_call(
        _out_proj_kernel,
        grid=(n // rows,),
        in_specs=[row(HEAD_W), row(HEAD_W), row(D_MODEL), full(gm), full(gf), full(wo), full(gn), full(wr),
                  full(br)],
        out_specs=(row(D_MODEL), row(D_MODEL), row(LANES)),
        out_shape=(jax.ShapeDtypeStruct((n, D_MODEL), F32), jax.ShapeDtypeStruct((n, D_MODEL), BF16),
                   jax.ShapeDtypeStruct((n, LANES), F32)),
        compiler_params=_params(("parallel",)),
        name="out_proj",
    )(om, of, x2d, gm, gf, wo, gn, wr, br)


def _experts_kernel(be_ref, bv_ref, xs_ref, gs_ref, wg_ref, wu_ref, wd_ref, o_ref, wg_s, wu_s, wd_s):
    i = pl.program_id(0)
    valid = bv_ref[i] != 0

    @pl.when(jnp.logical_and(valid, jnp.logical_or(i == 0, be_ref[i] != be_ref[jnp.maximum(i - 1, 0)])))
    def _():
        wg_s[...] = wg_ref[0, 0].astype(BF16)
        wu_s[...] = wu_ref[0, 0].astype(BF16)
        wd_s[...] = wd_ref[0, 0].astype(BF16)

    @pl.when(valid)
    def _():
        x = xs_ref[...]
        g = _dot(x, wg_s[...])
        u = _dot(x, wu_s[...])
        act = ((g * jax.nn.sigmoid(g)) * u).astype(BF16)
        o_ref[...] = (_dot(act, wd_s[...]) * gs_ref[:, 0:1]).astype(BF16)

    @pl.when(jnp.logical_not(valid))
    def _():
        o_ref[...] = jnp.zeros_like(o_ref)


def _experts(layer, blk_expert, blk_valid, xs, gate_slab, w_gate, w_up, w_down):
    n_slots = xs.shape[0]
    n_blocks = n_slots // EXPERT_TILE
    grid_spec = pltpu.PrefetchScalarGridSpec(
        num_scalar_prefetch=2,
        grid=(n_blocks,),
        in_specs=[pl.BlockSpec((EXPERT_TILE, D_MODEL), lambda i, be, bv: (i, 0)),
                  pl.BlockSpec((EXPERT_TILE, LANES), lambda i, be, bv: (i, 0)),
                  pl.BlockSpec((1, 1, D_MODEL, D_EXPERT), lambda i, be, bv: (layer, be[i], 0, 0)),
                  pl.BlockSpec((1, 1, D_MODEL, D_EXPERT), lambda i, be, bv: (layer, be[i], 0, 0)),
                  pl.BlockSpec((1, 1, D_EXPERT, D_MODEL), lambda i, be, bv: (layer, be[i], 0, 0))],
        out_specs=pl.BlockSpec((EXPERT_TILE, D_MODEL), lambda i, be, bv: (i, 0)),
        scratch_shapes=[pltpu.VMEM((D_MODEL, D_EXPERT), BF16), pltpu.VMEM((D_MODEL, D_EXPERT), BF16),
                        pltpu.VMEM((D_EXPERT, D_MODEL), BF16)],
    )
    return pl.pallas_call(
        _experts_kernel,
        grid_spec=grid_spec,
        out_shape=jax.ShapeDtypeStruct((n_slots, D_MODEL), BF16),
        compiler_params=_params(("arbitrary",)),
        name="experts",
    )(blk_expert, blk_valid, xs, gate_slab, w_gate, w_up, w_down)


def _combine_kernel(x_ref, y0_ref, y1_ref, g_ref, o_ref, *, final):
    x = x_ref[...] + (y0_ref[...].astype(F32) + y1_ref[...].astype(F32))
    o_ref[...] = _rms(x, g_ref[...]) if final else x


def _combine(x1, y0, y1, g, final):
    n = x1.shape[0]
    rows = min(ROW_TILE, n)
    row = pl.BlockSpec((rows, D_MODEL), lambda i: (i, 0))
    return pl.pallas_call(
        functools.partial(_combine_kernel, final=final),
        grid=(n // rows,),
        in_specs=[row, row, row, pl.BlockSpec((1, D_MODEL), lambda i: (0, 0))],
        out_specs=row,
        out_shape=jax.ShapeDtypeStruct((n, D_MODEL), F32),
        compiler_params=_params(("parallel",)),
        name="combine",
    )(x1, y0, y1, g)


def _swap_halves(w):
    half = w.shape[-1] // 2
    return jnp.concatenate([w[..., half:], w[..., :half]], axis=-1)


def _prep_in_weights(w_in):
    d = w_in.shape[0]
    q_lat, kv_lat, kr = w_in[:, 0:256], w_in[:, 256:384], w_in[:, 384:416]
    fq, fk, fv, fl = w_in[:, 416:928], w_in[:, 928:1440], w_in[:, 1440:1952], w_in[:, 1952:1960]
    slab_a = jnp.concatenate([kr, fl, jnp.zeros((d, LANES - MLA_ROPE - FOX_HEADS), F32)], axis=1)
    slab_b = jnp.concatenate([_swap_halves(kr), jnp.zeros((d, LANES - MLA_ROPE), F32)], axis=1)
    return jnp.concatenate([fq * (FOX_DIM ** -0.5), fk, fv, q_lat, kv_lat, slab_a, slab_b], axis=1).astype(BF16)


def _prep_uq(w_uq):
    w = w_uq.reshape(MLA_Q_RANK, MLA_HEADS, MLA_NOPE + MLA_ROPE)
    nope, pe = w[:, :, :MLA_NOPE], w[:, :, MLA_NOPE:]
    pairs = nope.reshape(MLA_Q_RANK, MLA_HEADS // 2, 2, MLA_NOPE)[:, :, ::-1, :].reshape(MLA_Q_RANK, -1)
    return jnp.concatenate([pairs, pe.reshape(MLA_Q_RANK, -1), _swap_halves(pe).reshape(MLA_Q_RANK, -1)],
                           axis=1).astype(BF16)


def _prep_ukv(w_ukv):
    w = w_ukv.reshape(MLA_KV_RANK, MLA_HEADS // 2, 2, 2, MLA_NOPE)
    even = w[:, :, 0, ::-1, :]
    odd = w[:, :, 1, :, :]
    return jnp.stack([even, odd], axis=2).reshape(MLA_KV_RANK, -1).astype(BF16)


def _prep_router(w_rg, b_rg, w_re, b_re):
    d = w_rg.shape[0]
    w = jnp.concatenate([w_rg, w_re, jnp.zeros((d, LANES - N_GROUPS - N_EXPERTS), F32)], axis=1)
    w_hi = w.astype(BF16)
    w_lo = (w - w_hi.astype(F32)).astype(BF16)
    b = jnp.concatenate([b_rg, b_re, jnp.zeros((LANES - N_GROUPS - N_EXPERTS,), F32)])[None, :]
    return jnp.concatenate([w_hi, w_lo], axis=1), b


def _rope_slabs(positions):
    half = MLA_ROPE // 2
    inv_freq = ROPE_THETA ** (-jnp.arange(half, dtype=F32) / half)
    ang = positions.astype(F32).reshape(-1)[:, None] * inv_freq
    cos, sin = jnp.cos(ang), jnp.sin(ang)
    reps = LANES // MLA_ROPE
    return jnp.tile(jnp.concatenate([cos, cos], axis=1), (1, reps)), jnp.tile(
        jnp.concatenate([-sin, sin], axis=1), (1, reps))


def _route(logits, n_blocks):
    n = logits.shape[0]
    g_prob = jax.nn.softmax(logits[:, 0:N_GROUPS], axis=-1)
    g_val, g_idx = lax.top_k(g_prob, 1)
    e_logits = logits[:, N_GROUPS:N_GROUPS + N_EXPERTS].reshape(n, N_GROUPS, EXPERTS_PER_GROUP)
    idx = jnp.broadcast_to(g_idx[:, :, None], (n, 1, EXPERTS_PER_GROUP))
    e_prob = jax.nn.softmax(jnp.take_along_axis(e_logits, idx, axis=1)[:, 0], axis=-1)
    e_val, e_local = lax.top_k(e_prob, TOP_K)
    gates = g_val * e_val / jnp.sum(e_val, axis=-1, keepdims=True)
    eid = (g_idx * EXPERTS_PER_GROUP + e_local).reshape(-1).astype(jnp.int32)

    onehot = (eid[:, None] == jnp.arange(N_EXPERTS, dtype=jnp.int32)[None, :]).astype(jnp.int32)
    csum = jnp.cumsum(onehot, axis=0)
    rank = jnp.sum(onehot * csum, axis=1) - 1
    counts = csum[-1]
    padded = (counts + EXPERT_TILE - 1) // EXPERT_TILE * EXPERT_TILE
    pad_end = jnp.cumsum(padded)
    pad_start = pad_end - padded
    dest = pad_start[eid] + rank
    n_slots = n_blocks * EXPERT_TILE
    tok = jnp.repeat(jnp.arange(n, dtype=jnp.int32), TOP_K)
    slot_tok = jnp.zeros((n_slots,), jnp.int32).at[dest].set(tok)
    slot_gate = jnp.zeros((n_slots,), F32).at[dest].set(gates.reshape(-1))
    blk_start = jnp.arange(n_blocks, dtype=jnp.int32) * EXPERT_TILE
    blk_expert = jnp.clip(jnp.searchsorted(pad_end, blk_start, side="right"), 0, N_EXPERTS - 1).astype(jnp.int32)
    blk_valid = (blk_start < pad_end[-1]).astype(jnp.int32)
    return slot_tok, slot_gate, dest.reshape(n, TOP_K), blk_expert, blk_valid


def kernel(x, positions, attn_norm, w_in, b_forget, q_norm, w_uq, kv_norm, w_ukv, mla_out_norm, fox_out_norm,
           w_out, ffn_norm, w_router_group, b_router_group, w_router_expert, b_router_expert, w_gate, w_up,
           w_down, final_norm):
    batch, seq, d = x.shape
    n = batch * seq
    depth = w_in.shape[0]
    n_blocks = -(-(n * TOP_K) // EXPERT_TILE) + N_EXPERTS
    cc, ss = _rope_slabs(positions)
    tri = (jnp.arange(CUM_TILE)[:, None] <= jnp.arange(CUM_TILE)[None, :]).astype(BF16)
    xf = x.reshape(n, d)
    for l in range(depth):
        fq, fk, fv, qn, qpe, kv, kpe, flt = _in_proj(
            xf, attn_norm[l][None, :], _prep_in_weights(w_in[l]), q_norm[l][None, :], _prep_uq(w_uq[l]),
            kv_norm[l][None, :], _prep_ukv(w_ukv[l]), cc, ss)
        negc = _fox_decay(flt, b_forget[l][:, None], tri, batch, seq)
        o_mla = _mla_attn(qn, qpe, kv, kpe, batch, seq).reshape(n, HEAD_W)
        o_fox = _fox_attn(fq, fk, fv, negc, batch, seq).reshape(n, HEAD_W)
        wr, br = _prep_router(w_router_group[l], b_router_group[l], w_router_expert[l], b_router_expert[l])
        x1, h2, logits = _out_proj(o_mla, o_fox, xf, mla_out_norm[l][None, :], fox_out_norm[l][None, :],
                                   w_out[l].astype(BF16), ffn_norm[l][None, :], wr, br)
        slot_tok, slot_gate, dest, blk_expert, blk_valid = _route(logits, n_blocks)
        xs = jnp.take(h2, slot_tok, axis=0)
        gate_slab = jnp.broadcast_to(slot_gate[:, None], (slot_gate.shape[0], LANES))
        ys = _experts(l, blk_expert, blk_valid, xs, gate_slab, w_gate, w_up, w_down)
        y0 = jnp.take(ys, dest[:, 0], axis=0)
        y1 = jnp.take(ys, dest[:, 1], axis=0)
        final = l == depth - 1
        xf = _combine(x1, y0, y1, final_norm[None, :] if final else ffn_norm[l][None, :], final)
    return xf.reshape(batch, seq, d)
```

```python
import functools

import jax
import jax.numpy as jnp
from jax import lax
from jax.experimental import pallas as pl
from jax.experimental.pallas import tpu as pltpu

D_MODEL = 1024
CHUNK = 64
MLA_HEADS = 8
MLA_NOPE = 64
MLA_ROPE = 32
MLA_V = 64
MLA_Q_RANK = 256
MLA_KV_RANK = 128
ROPE_THETA = 10000.0
FOX_HEADS = 8
FOX_DIM = 64
HEAD_W = 512
N_GROUPS = 4
EXPERTS_PER_GROUP = 8
N_EXPERTS = 32
TOP_K = 2
D_EXPERT = 512
NORM_EPS = 1e-6

LANES = 128
IN_COLS = 2176
ROW_TILE = 512
Q_TILE = 256
EXPERT_TILE = 256
CUM_TILE = 256
NEG_BIG = -1e30
VMEM_LIMIT = 48 * 1024 * 1024

F32 = jnp.float32
BF16 = jnp.bfloat16


def _rms(x, g):
    return (x * lax.rsqrt(jnp.mean(x * x, axis=-1, keepdims=True) + NORM_EPS)) * g


def _dot(a, b):
    return jnp.dot(a, b, preferred_element_type=F32)


def _dot_nt(a, b):
    return lax.dot_general(a, b, (((1,), (1,)), ((), ())), preferred_element_type=F32)


def _params(sem):
    return pltpu.CompilerParams(dimension_semantics=sem, vmem_limit_bytes=VMEM_LIMIT)


def _in_proj_kernel(x_ref, g_ref, w_ref, gq_ref, wuq_ref, gkv_ref, wukv_ref, cc_ref, ss_ref,
                    fq_ref, fk_ref, fv_ref, qn_ref, qpe_ref, kv_ref, kpe_ref, flt_ref):
    h = _rms(x_ref[...], g_ref[...])
    p = _dot(h.astype(BF16), w_ref[...])
    fq_ref[...] = p[:, 0:512].astype(BF16)
    fk_ref[...] = p[:, 512:1024].astype(BF16)
    fv_ref[...] = p[:, 1024:1536].astype(BF16)

    cc = cc_ref[...]
    ss = ss_ref[...]
    scale = (MLA_NOPE + MLA_ROPE) ** -0.5
    q = _dot(_rms(p[:, 1536:1792], gq_ref[...]).astype(BF16), wuq_ref[...])
    qn_ref[...] = (q[:, 0:512] * scale).astype(BF16)
    cc2 = jnp.concatenate([cc, cc], axis=1)
    ss2 = jnp.concatenate([ss, ss], axis=1)
    qpe_ref[...] = ((q[:, 512:768] * cc2 + q[:, 768:1024] * ss2) * scale).astype(BF16)

    kv_ref[...] = _dot(_rms(p[:, 1792:1920], gkv_ref[...]).astype(BF16), wukv_ref[...]).astype(BF16)

    slab_a = p[:, 1920:2048]
    slab_b = p[:, 2048:2176]
    lane = lax.broadcasted_iota(jnp.int32, slab_a.shape, 1)
    roped = jnp.where(lane < MLA_ROPE, slab_a * cc + slab_b * ss, 0.0)
    tiled = roped + pltpu.roll(roped, 32, 1) + pltpu.roll(roped, 64, 1) + pltpu.roll(roped, 96, 1)
    kpe_ref[...] = tiled.astype(BF16)
    flt_ref[...] = slab_a.T[MLA_ROPE:MLA_ROPE + FOX_HEADS, :]


def _in_proj(x2d, g, w, gq, wuq, gkv, wukv, cc, ss):
    n = x2d.shape[0]
    rows = min(ROW_TILE, n)
    row = lambda c: pl.BlockSpec((rows, c), lambda i: (i, 0))
    full = lambda a: pl.BlockSpec(a.shape, lambda i: (0,) * a.ndim)
    out_shape = (
        jax.ShapeDtypeStruct((n, HEAD_W), BF16), jax.ShapeDtypeStruct((n, HEAD_W), BF16),
        jax.ShapeDtypeStruct((n, HEAD_W), BF16), jax.ShapeDtypeStruct((n, HEAD_W), BF16),
        jax.ShapeDtypeStruct((n, 256), BF16), jax.ShapeDtypeStruct((n, 1024), BF16),
        jax.ShapeDtypeStruct((n, LANES), BF16), jax.ShapeDtypeStruct((FOX_HEADS, n), F32),
    )
    return pl.pallas_call(
        _in_proj_kernel,
        grid=(n // rows,),
        in_specs=[row(D_MODEL), full(g), full(w), full(gq), full(wuq), full(gkv), full(wukv),
                  row(LANES), row(LANES)],
        out_specs=(row(HEAD_W), row(HEAD_W), row(HEAD_W), row(HEAD_W), row(256), row(1024), row(LANES),
                   pl.BlockSpec((FOX_HEADS, rows), lambda i: (0, i))),
        out_shape=out_shape,
        compiler_params=_params(("parallel",)),
        name="in_proj",
    )(x2d, g, w, gq, wuq, gkv, wukv, cc, ss)


def _fox_decay_kernel(fl_ref, b_ref, tri_ref, negc_ref):
    z = fl_ref[...] + b_ref[...]
    lf = jnp.minimum(z, 0.0) - jnp.log1p(jnp.exp(-jnp.abs(z)))
    seq = lf.shape[1]
    tri = tri_ref[...]
    carry = jnp.zeros((FOX_HEADS, 1), F32)
    zeros = jnp.zeros((FOX_HEADS, CUM_TILE), F32)
    for j in range(seq // CUM_TILE):
        v = lf[:, j * CUM_TILE:(j + 1) * CUM_TILE]
        hi = v.astype(BF16).astype(F32)
        r1 = v - hi
        mid = r1.astype(BF16).astype(F32)
        lo = r1 - mid
        parts = _dot(jnp.concatenate([hi, mid, lo, zeros], axis=0).astype(BF16), tri)
        cs = (parts[0:8] + parts[8:16]) + parts[16:24] + carry
        negc_ref[0, :, j * CUM_TILE:(j + 1) * CUM_TILE] = -cs
        carry = cs[:, CUM_TILE - 1:CUM_TILE]


def _fox_decay(flt, b_col, tri, batch, seq):
    return pl.pallas_call(
        _fox_decay_kernel,
        grid=(batch,),
        in_specs=[pl.BlockSpec((FOX_HEADS, seq), lambda b: (0, b)),
                  pl.BlockSpec((FOX_HEADS, 1), lambda b: (0, 0)),
                  pl.BlockSpec((CUM_TILE, CUM_TILE), lambda b: (0, 0))],
        out_specs=pl.BlockSpec((1, FOX_HEADS, seq), lambda b: (b, 0, 0)),
        out_shape=jax.ShapeDtypeStruct((batch, FOX_HEADS, seq), F32),
        compiler_params=_params(("parallel",)),
        name="fox_decay",
    )(flt, b_col, tri)


def _softmax_pv(s_off, s_diag, v_ref, qs, qe):
    m = jnp.max(s_diag, axis=-1, keepdims=True)
    if s_off is not None:
        m = jnp.maximum(m, jnp.max(s_off, axis=-1, keepdims=True))
    p_diag = jnp.exp(s_diag - m)
    l = jnp.sum(p_diag, axis=-1, keepdims=True)
    o = _dot(p_diag.astype(BF16), v_ref[qs:qe, :])
    if s_off is not None:
        p_off = jnp.exp(s_off - m)
        l = l + jnp.sum(p_off, axis=-1, keepdims=True)
        o = o + _dot(p_off.astype(BF16), v_ref[0:qs, :])
    return o / l


def _mla_attn_kernel(qn_ref, qpe_ref, kv_ref, kpe_ref, o_ref, qs_ref, ks_ref, vs_ref):
    j = pl.program_id(1)
    seq = qn_ref.shape[1]
    lane = lax.broadcasted_iota(jnp.int32, (seq, LANES), 1)
    qn = qn_ref[0]
    qpe = qpe_ref[0]
    kpe = kpe_ref[0]
    for hh in range(2):
        nope_mask = (lane >= 64) if hh == 0 else (lane < 64)
        pe_slot = 2 * (j % 2) + hh
        pe_mask = (lane // MLA_ROPE) == pe_slot
        qs_ref[hh, :, 0:LANES] = jnp.where(nope_mask, qn, jnp.zeros_like(qn))
        qs_ref[hh, :, LANES:2 * LANES] = jnp.where(pe_mask, qpe, jnp.zeros_like(qpe))
        kvh = kv_ref[0, :, hh * LANES:(hh + 1) * LANES]
        ks_ref[hh, :, 0:LANES] = kvh
        ks_ref[hh, :, LANES:2 * LANES] = kpe
        vs_ref[hh] = kvh

    row = lax.broadcasted_iota(jnp.int32, (Q_TILE, Q_TILE), 0)
    col = lax.broadcasted_iota(jnp.int32, (Q_TILE, Q_TILE), 1)
    allowed = (col // CHUNK) <= (row // CHUNK)
    out_lane = lax.broadcasted_iota(jnp.int32, (Q_TILE, LANES), 1)
    for i in range(seq // Q_TILE):
        qs, qe = i * Q_TILE, (i + 1) * Q_TILE
        outs = []
        for hh in range(2):
            q = qs_ref[hh, qs:qe, :]
            s_diag = jnp.where(allowed, _dot_nt(q, ks_ref[hh, qs:qe, :]), NEG_BIG)
            s_off = _dot_nt(q, ks_ref[hh, 0:qs, :]) if i > 0 else None
            outs.append(_softmax_pv(s_off, s_diag, vs_ref.at[hh], qs, qe))
        o_ref[0, qs:qe, :] = jnp.where(out_lane < 64, outs[0], outs[1]).astype(BF16)


def _mla_attn(qn, qpe, kv, kpe, batch, seq):
    qn, qpe, kv, kpe = (a.reshape(batch, seq, a.shape[-1]) for a in (qn, qpe, kv, kpe))
    return pl.pallas_call(
        _mla_attn_kernel,
        grid=(batch, MLA_HEADS // 2),
        in_specs=[pl.BlockSpec((1, seq, LANES), lambda b, j: (b, 0, j)),
                  pl.BlockSpec((1, seq, LANES), lambda b, j: (b, 0, j // 2)),
                  pl.BlockSpec((1, seq, 2 * LANES), lambda b, j: (b, 0, j)),
                  pl.BlockSpec((1, seq, LANES), lambda b, j: (b, 0, 0))],
        out_specs=pl.BlockSpec((1, seq, LANES), lambda b, j: (b, 0, j)),
        out_shape=jax.ShapeDtypeStruct((batch, seq, HEAD_W), BF16),
        scratch_shapes=[pltpu.VMEM((2, seq, 2 * LANES), BF16), pltpu.VMEM((2, seq, 2 * LANES), BF16),
                        pltpu.VMEM((2, seq, LANES), BF16)],
        compiler_params=_params(("parallel", "parallel")),
        name="mla_attn",
    )(qn, qpe, kv, kpe)


def _fox_attn_kernel(q_ref, k_ref, v_ref, negc_ref, o_ref, qs_ref):
    j = pl.program_id(1)
    seq = q_ref.shape[1]
    lane = lax.broadcasted_iota(jnp.int32, (seq, LANES), 1)
    q = q_ref[0]
    qs_ref[0] = jnp.where(lane < 64, q, jnp.zeros_like(q))
    qs_ref[1] = jnp.where(lane >= 64, q, jnp.zeros_like(q))

    row = lax.broadcasted_iota(jnp.int32, (Q_TILE, Q_TILE), 0)
    col = lax.broadcasted_iota(jnp.int32, (Q_TILE, Q_TILE), 1)
    allowed = col <= row
    out_lane = lax.broadcasted_iota(jnp.int32, (Q_TILE, LANES), 1)
    for i in range(seq // Q_TILE):
        qs, qe = i * Q_TILE, (i + 1) * Q_TILE
        outs = []
        for hh in range(2):
            negc = negc_ref[0, pl.ds(2 * j + hh, 1), :]
            qh = qs_ref[hh, qs:qe, :]
            s_diag = jnp.where(allowed, _dot_nt(qh, k_ref[0, qs:qe, :]) + negc[:, qs:qe], NEG_BIG)
            s_off = (_dot_nt(qh, k_ref[0, 0:qs, :]) + negc[:, 0:qs]) if i > 0 else None
            outs.append(_softmax_pv(s_off, s_diag, v_ref.at[0], qs, qe))
        o_ref[0, qs:qe, :] = jnp.where(out_lane < 64, outs[0], outs[1]).astype(BF16)


def _fox_attn(fq, fk, fv, negc, batch, seq):
    fq, fk, fv = (a.reshape(batch, seq, HEAD_W) for a in (fq, fk, fv))
    spec = pl.BlockSpec((1, seq, LANES), lambda b, j: (b, 0, j))
    return pl.pallas_call(
        _fox_attn_kernel,
        grid=(batch, FOX_HEADS // 2),
        in_specs=[spec, spec, spec, pl.BlockSpec((1, FOX_HEADS, seq), lambda b, j: (b, 0, 0))],
        out_specs=spec,
        out_shape=jax.ShapeDtypeStruct((batch, seq, HEAD_W), BF16),
        scratch_shapes=[pltpu.VMEM((2, seq, LANES), BF16)],
        compiler_params=_params(("parallel", "parallel")),
        name="fox_attn",
    )(fq, fk, fv, negc)


def _lane_max(v):
    return jnp.max(v, axis=1, keepdims=True)


def _first_lane(hit, lane_f):
    return jnp.min(jnp.where(hit, lane_f, float(LANES)), axis=1, keepdims=True)


def _out_proj_kernel(om_ref, of_ref, x_ref, gm_ref, gf_ref, wo_ref, gn_ref, wr_ref, br_ref, tri_ref,
                     x1_ref, h2_ref, rt_ref, cnt_ref, carry_ref):
    @pl.when(pl.program_id(0) == 0)
    def _():
        carry_ref[...] = jnp.zeros_like(carry_ref)

    a = _rms(om_ref[...].astype(F32), gm_ref[...]).astype(BF16)
    b = _rms(of_ref[...].astype(F32), gf_ref[...]).astype(BF16)
    x1 = x_ref[...] + _dot(a, wo_ref[0:HEAD_W, :]) + _dot(b, wo_ref[HEAD_W:2 * HEAD_W, :])
    x1_ref[...] = x1
    h2 = _rms(x1, gn_ref[...])
    h_hi = h2.astype(BF16)
    h2_ref[...] = h_hi
    h_lo = (h2 - h_hi.astype(F32)).astype(BF16)
    t = _dot(h_hi, wr_ref[...])
    lg = (t[:, 0:LANES] + t[:, LANES:2 * LANES]) + _dot(h_lo, wr_ref[:, 0:LANES]) + br_ref[...]

    lane = lax.broadcasted_iota(jnp.int32, lg.shape, 1)
    lane_f = lane.astype(F32)
    neg_inf = float("-inf")
    is_group = lane < N_GROUPS
    gl = jnp.where(is_group, lg, neg_inf)
    mg = _lane_max(gl)
    gi = _first_lane(gl == mg, lane_f)
    g_val = 1.0 / jnp.sum(jnp.where(is_group, jnp.exp(lg - mg), 0.0), axis=1, keepdims=True)
    group_of_lane = ((lane - N_GROUPS) >> 3).astype(F32)
    is_expert = (lane >= N_GROUPS) & (lane < N_GROUPS + N_EXPERTS) & (group_of_lane == gi)
    el = jnp.where(is_expert, lg, neg_inf)
    m1 = _lane_max(el)
    i1 = _first_lane(el == m1, lane_f)
    el2 = jnp.where(lane_f == i1, neg_inf, el)
    m2 = _lane_max(el2)
    i2 = _first_lane(el2 == m2, lane_f)
    r = jnp.exp(m2 - m1)
    g0 = g_val / (1.0 + r)
    g1 = g0 * r

    hit1 = lane_f == i1
    hit2 = lane_f == i2
    onehot = jnp.where(hit1 | hit2, 1.0, 0.0)
    carry = carry_ref[...]
    before = _dot(tri_ref[...], onehot.astype(BF16)) + carry
    r0 = jnp.sum(jnp.where(hit1, before, 0.0), axis=1, keepdims=True)
    r1 = jnp.sum(jnp.where(hit2, before, 0.0), axis=1, keepdims=True)
    carry = carry + jnp.sum(onehot, axis=0, keepdims=True)
    carry_ref[...] = carry
    cnt_ref[...] = carry

    vals = (i1 - N_GROUPS, i2 - N_GROUPS, g0, g1, r0, r1)
    rt = jnp.zeros_like(lg)
    for k, v in enumerate(vals):
        rt = jnp.where(lane == k, v, rt)
    rt_ref[...] = rt


def _out_proj(om, of, x2d, gm, gf, wo, gn, wr, br, tri):
    n = x2d.shape[0]
    rows = tri.shape[0]
    row = lambda c: pl.BlockSpec((rows, c), lambda i: (i, 0))
    full = lambda a: pl.BlockSpec(a.shape, lambda i: (0,) * a.ndim)
    return pl.pallas_call(
        _out_proj_kernel,
        grid=(n // rows,),
        in_specs=[row(HEAD_W), row(HEAD_W), row(D_MODEL), full(gm), full(gf), full(wo), full(gn), full(wr),
                  full(br), full(tri)],
        out_specs=(row(D_MODEL), row(D_MODEL), row(LANES), pl.BlockSpec((1, LANES), lambda i: (0, 0))),
        out_shape=(jax.ShapeDtypeStruct((n, D_MODEL), F32), jax.ShapeDtypeStruct((n, D_MODEL), BF16),
                   jax.ShapeDtypeStruct((n, LANES), F32), jax.ShapeDtypeStruct((1, LANES), F32)),
        scratch_shapes=[pltpu.VMEM((1, LANES), F32)],
        compiler_params=_params(("arbitrary",)),
        name="out_proj",
    )(om, of, x2d, gm, gf, wo, gn, wr, br, tri)


def _experts_kernel(be_ref, bv_ref, xs_ref, gs_ref, wg_ref, wu_ref, wd_ref, o_ref, wg_s, wu_s, wd_s):
    i = pl.program_id(0)
    valid = bv_ref[i] != 0

    @pl.when(jnp.logical_and(valid, jnp.logical_or(i == 0, be_ref[i] != be_ref[jnp.maximum(i - 1, 0)])))
    def _():
        wg_s[...] = wg_ref[0, 0].astype(BF16)
        wu_s[...] = wu_ref[0, 0].astype(BF16)
        wd_s[...] = wd_ref[0, 0].astype(BF16)

    @pl.when(valid)
    def _():
        x = xs_ref[...]
        g = _dot(x, wg_s[...])
        u = _dot(x, wu_s[...])
        act = ((g * jax.nn.sigmoid(g)) * u).astype(BF16)
        o_ref[...] = (_dot(act, wd_s[...]) * gs_ref[:, 0:1]).astype(BF16)

    @pl.when(jnp.logical_not(valid))
    def _():
        o_ref[...] = jnp.zeros_like(o_ref)


def _experts(layer, blk_expert, blk_valid, xs, gate_slab, w_gate, w_up, w_down):
    n_slots = xs.shape[0]
    n_blocks = n_slots // EXPERT_TILE
    grid_spec = pltpu.PrefetchScalarGridSpec(
        num_scalar_prefetch=2,
        grid=(n_blocks,),
        in_specs=[pl.BlockSpec((EXPERT_TILE, D_MODEL), lambda i, be, bv: (i, 0)),
                  pl.BlockSpec((EXPERT_TILE, LANES), lambda i, be, bv: (i, 0)),
                  pl.BlockSpec((1, 1, D_MODEL, D_EXPERT), lambda i, be, bv: (layer, be[i], 0, 0)),
                  pl.BlockSpec((1, 1, D_MODEL, D_EXPERT), lambda i, be, bv: (layer, be[i], 0, 0)),
                  pl.BlockSpec((1, 1, D_EXPERT, D_MODEL), lambda i, be, bv: (layer, be[i], 0, 0))],
        out_specs=pl.BlockSpec((EXPERT_TILE, D_MODEL), lambda i, be, bv: (i, 0)),
        scratch_shapes=[pltpu.VMEM((D_MODEL, D_EXPERT), BF16), pltpu.VMEM((D_MODEL, D_EXPERT), BF16),
                        pltpu.VMEM((D_EXPERT, D_MODEL), BF16)],
    )
    return pl.pallas_call(
        _experts_kernel,
        grid_spec=grid_spec,
        out_shape=jax.ShapeDtypeStruct((n_slots, D_MODEL), BF16),
        compiler_params=_params(("arbitrary",)),
        name="experts",
    )(blk_expert, blk_valid, xs, gate_slab, w_gate, w_up, w_down)


def _combine_kernel(x_ref, y0_ref, y1_ref, g_ref, o_ref, *, final):
    x = x_ref[...] + (y0_ref[...].astype(F32) + y1_ref[...].astype(F32))
    o_ref[...] = _rms(x, g_ref[...]) if final else x


def _combine(x1, y0, y1, g, final):
    n = x1.shape[0]
    rows = min(ROW_TILE, n)
    row = pl.BlockSpec((rows, D_MODEL), lambda i: (i, 0))
    return pl.pallas_call(
        functools.partial(_combine_kernel, final=final),
        grid=(n // rows,),
        in_specs=[row, row, row, pl.BlockSpec((1, D_MODEL), lambda i: (0, 0))],
        out_specs=row,
        out_shape=jax.ShapeDtypeStruct((n, D_MODEL), F32),
        compiler_params=_params(("parallel",)),
        name="combine",
    )(x1, y0, y1, g)


def _swap_halves(w):
    half = w.shape[-1] // 2
    return jnp.concatenate([w[..., half:], w[..., :half]], axis=-1)


def _prep_in_weights(w_in):
    d = w_in.shape[0]
    q_lat, kv_lat, kr = w_in[:, 0:256], w_in[:, 256:384], w_in[:, 384:416]
    fq, fk, fv, fl = w_in[:, 416:928], w_in[:, 928:1440], w_in[:, 1440:1952], w_in[:, 1952:1960]
    slab_a = jnp.concatenate([kr, fl, jnp.zeros((d, LANES - MLA_ROPE - FOX_HEADS), F32)], axis=1)
    slab_b = jnp.concatenate([_swap_halves(kr), jnp.zeros((d, LANES - MLA_ROPE), F32)], axis=1)
    return jnp.concatenate([fq * (FOX_DIM ** -0.5), fk, fv, q_lat, kv_lat, slab_a, slab_b], axis=1).astype(BF16)


def _prep_uq(w_uq):
    w = w_uq.reshape(MLA_Q_RANK, MLA_HEADS, MLA_NOPE + MLA_ROPE)
    nope, pe = w[:, :, :MLA_NOPE], w[:, :, MLA_NOPE:]
    pairs = nope.reshape(MLA_Q_RANK, MLA_HEADS // 2, 2, MLA_NOPE)[:, :, ::-1, :].reshape(MLA_Q_RANK, -1)
    return jnp.concatenate([pairs, pe.reshape(MLA_Q_RANK, -1), _swap_halves(pe).reshape(MLA_Q_RANK, -1)],
                           axis=1).astype(BF16)


def _prep_ukv(w_ukv):
    w = w_ukv.reshape(MLA_KV_RANK, MLA_HEADS // 2, 2, 2, MLA_NOPE)
    even = w[:, :, 0, ::-1, :]
    odd = w[:, :, 1, :, :]
    return jnp.stack([even, odd], axis=2).reshape(MLA_KV_RANK, -1).astype(BF16)


def _prep_router(w_rg, b_rg, w_re, b_re):
    d = w_rg.shape[0]
    w = jnp.concatenate([w_rg, w_re, jnp.zeros((d, LANES - N_GROUPS - N_EXPERTS), F32)], axis=1)
    w_hi = w.astype(BF16)
    w_lo = (w - w_hi.astype(F32)).astype(BF16)
    b = jnp.concatenate([b_rg, b_re, jnp.zeros((LANES - N_GROUPS - N_EXPERTS,), F32)])[None, :]
    return jnp.concatenate([w_hi, w_lo], axis=1), b


def _rope_slabs(positions):
    half = MLA_ROPE // 2
    inv_freq = ROPE_THETA ** (-jnp.arange(half, dtype=F32) / half)
    ang = positions.astype(F32).reshape(-1)[:, None] * inv_freq
    cos, sin = jnp.cos(ang), jnp.sin(ang)
    reps = LANES // MLA_ROPE
    return jnp.tile(jnp.concatenate([cos, cos], axis=1), (1, reps)), jnp.tile(
        jnp.concatenate([-sin, sin], axis=1), (1, reps))


def _slot_layout(route, counts, n_blocks):
    n = route.shape[0]
    eid = route[:, 0:TOP_K].astype(jnp.int32)
    gates = route[:, TOP_K:2 * TOP_K]
    rank = route[:, 2 * TOP_K:3 * TOP_K].astype(jnp.int32)
    counts = counts[0, N_GROUPS:N_GROUPS + N_EXPERTS].astype(jnp.int32)
    padded = (counts + EXPERT_TILE - 1) // EXPERT_TILE * EXPERT_TILE
    pad_end = jnp.cumsum(padded)
    pad_start = pad_end - padded
    experts = jnp.arange(N_EXPERTS, dtype=jnp.int32)
    dest = jnp.sum(jnp.where(eid[:, :, None] == experts, pad_start, 0), axis=-1) + rank
    n_slots = n_blocks * EXPERT_TILE
    assignment = jnp.arange(n * TOP_K, dtype=jnp.int32)
    slot_assignment = jnp.zeros((n_slots,), jnp.int32).at[dest.reshape(-1)].set(assignment, unique_indices=True)
    slot_tok = slot_assignment // TOP_K
    slot_gate = gates.reshape(-1)[slot_assignment]
    blk_start = jnp.arange(n_blocks, dtype=jnp.int32) * EXPERT_TILE
    blk_expert = jnp.minimum(jnp.sum((blk_start[:, None] >= pad_end[None, :]).astype(jnp.int32), axis=1),
                             N_EXPERTS - 1)
    blk_valid = (blk_start < pad_end[-1]).astype(jnp.int32)
    return slot_tok, slot_gate, dest, blk_expert, blk_valid


def kernel(x, positions, attn_norm, w_in, b_forget, q_norm, w_uq, kv_norm, w_ukv, mla_out_norm, fox_out_norm,
           w_out, ffn_norm, w_router_group, b_router_group, w_router_expert, b_router_expert, w_gate, w_up,
           w_down, final_norm):
    batch, seq, d = x.shape
    n = batch * seq
    depth = w_in.shape[0]
    n_blocks = -(-(n * TOP_K) // EXPERT_TILE) + N_EXPERTS
    cc, ss = _rope_slabs(positions)
    tri = (jnp.arange(CUM_TILE)[:, None] <= jnp.arange(CUM_TILE)[None, :]).astype(BF16)
    rows = min(ROW_TILE, n)
    tri_rows = (jnp.arange(rows)[None, :] < jnp.arange(rows)[:, None]).astype(BF16)
    xf = x.reshape(n, d)
    for l in range(depth):
        fq, fk, fv, qn, qpe, kv, kpe, flt = _in_proj(
            xf, attn_norm[l][None, :], _prep_in_weights(w_in[l]), q_norm[l][None, :], _prep_uq(w_uq[l]),
            kv_norm[l][None, :], _prep_ukv(w_ukv[l]), cc, ss)
        negc = _fox_decay(flt, b_forget[l][:, None], tri, batch, seq)
        o_mla = _mla_attn(qn, qpe, kv, kpe, batch, seq).reshape(n, HEAD_W)
        o_fox = _fox_attn(fq, fk, fv, negc, batch, seq).reshape(n, HEAD_W)
        wr, br = _prep_router(w_router_group[l], b_router_group[l], w_router_expert[l], b_router_expert[l])
        x1, h2, route, counts = _out_proj(o_mla, o_fox, xf, mla_out_norm[l][None, :], fox_out_norm[l][None, :],
                                          w_out[l].astype(BF16), ffn_norm[l][None, :], wr, br, tri_rows)
        slot_tok, slot_gate, dest, blk_expert, blk_valid = _slot_layout(route, counts, n_blocks)
        xs = jnp.take(h2, slot_tok, axis=0)
        gate_slab = jnp.broadcast_to(slot_gate[:, None], (slot_gate.shape[0], LANES))
        ys = _experts(l, blk_expert, blk_valid, xs, gate_slab, w_gate, w_up, w_down)
        y0 = jnp.take(ys, dest[:, 0], axis=0)
        y1 = jnp.take(ys, dest[:, 1], axis=0)
        final = l == depth - 1
        xf = _combine(x1, y0, y1, final_norm[None, :] if final else ffn_norm[l][None, :], final)
    return xf.reshape(batch, seq, d)
```

```python
import functools

import jax
import jax.numpy as jnp
from jax import lax
from jax.experimental import pallas as pl
from jax.experimental.pallas import tpu as pltpu

D_MODEL = 1024
CHUNK = 64
MLA_HEADS = 8
MLA_NOPE = 64
MLA_ROPE = 32
MLA_V = 64
MLA_Q_RANK = 256
MLA_KV_RANK = 128
ROPE_THETA = 10000.0
FOX_HEADS = 8
FOX_DIM = 64
HEAD_W = 512
N_GROUPS = 4
EXPERTS_PER_GROUP = 8
N_EXPERTS = 32
TOP_K = 2
D_EXPERT = 512
NORM_EPS = 1e-6

LANES = 128
IN_COLS = 2176
ROW_TILE = 512
Q_TILE = 256
EXPERT_TILE = 256
CUM_TILE = 256
NEG_BIG = -1e30
VMEM_LIMIT = 48 * 1024 * 1024

F32 = jnp.float32
BF16 = jnp.bfloat16


def _rms(x, g):
    return (x * lax.rsqrt(jnp.mean(x * x, axis=-1, keepdims=True) + NORM_EPS)) * g


def _dot(a, b):
    return jnp.dot(a, b, preferred_element_type=F32)


def _dot_nt(a, b):
    return lax.dot_general(a, b, (((1,), (1,)), ((), ())), preferred_element_type=F32)


def _params(sem):
    return pltpu.CompilerParams(dimension_semantics=sem, vmem_limit_bytes=VMEM_LIMIT)


def _pack_rows(v):
    w = v.shape[1] // 2
    lo = lax.bitcast_convert_type(v[:, :w], jnp.uint32) >> 16
    hi = lax.bitcast_convert_type(v[:, w:], jnp.uint32) & jnp.uint32(0xFFFF0000)
    return hi | lo


def _unpack_rows(p):
    lo = lax.bitcast_convert_type(p << 16, F32)
    hi = lax.bitcast_convert_type(p & jnp.uint32(0xFFFF0000), F32)
    return jnp.concatenate([lo, hi], axis=1)


def _in_proj_kernel(x_ref, g_ref, w_ref, gq_ref, wuq_ref, gkv_ref, wukv_ref, cc_ref, ss_ref,
                    fq_ref, fk_ref, fv_ref, qn_ref, qpe_ref, kv_ref, kpe_ref, flt_ref):
    h = _rms(x_ref[...], g_ref[...])
    p = _dot(h.astype(BF16), w_ref[...])
    fq_ref[...] = p[:, 0:512].astype(BF16)
    fk_ref[...] = p[:, 512:1024].astype(BF16)
    fv_ref[...] = p[:, 1024:1536].astype(BF16)

    cc = cc_ref[...]
    ss = ss_ref[...]
    scale = (MLA_NOPE + MLA_ROPE) ** -0.5
    q = _dot(_rms(p[:, 1536:1792], gq_ref[...]).astype(BF16), wuq_ref[...])
    qn_ref[...] = (q[:, 0:512] * scale).astype(BF16)
    cc2 = jnp.concatenate([cc, cc], axis=1)
    ss2 = jnp.concatenate([ss, ss], axis=1)
    qpe_ref[...] = ((q[:, 512:768] * cc2 + q[:, 768:1024] * ss2) * scale).astype(BF16)

    kv_ref[...] = _dot(_rms(p[:, 1792:1920], gkv_ref[...]).astype(BF16), wukv_ref[...]).astype(BF16)

    slab_a = p[:, 1920:2048]
    slab_b = p[:, 2048:2176]
    lane = lax.broadcasted_iota(jnp.int32, slab_a.shape, 1)
    roped = jnp.where(lane < MLA_ROPE, slab_a * cc + slab_b * ss, 0.0)
    tiled = roped + pltpu.roll(roped, 32, 1) + pltpu.roll(roped, 64, 1) + pltpu.roll(roped, 96, 1)
    kpe_ref[...] = tiled.astype(BF16)
    flt_ref[...] = slab_a.T[MLA_ROPE:MLA_ROPE + FOX_HEADS, :]


def _in_proj(x2d, g, w, gq, wuq, gkv, wukv, cc, ss):
    n = x2d.shape[0]
    rows = min(ROW_TILE, n)
    row = lambda c: pl.BlockSpec((rows, c), lambda i: (i, 0))
    full = lambda a: pl.BlockSpec(a.shape, lambda i: (0,) * a.ndim)
    out_shape = (
        jax.ShapeDtypeStruct((n, HEAD_W), BF16), jax.ShapeDtypeStruct((n, HEAD_W), BF16),
        jax.ShapeDtypeStruct((n, HEAD_W), BF16), jax.ShapeDtypeStruct((n, HEAD_W), BF16),
        jax.ShapeDtypeStruct((n, 256), BF16), jax.ShapeDtypeStruct((n, 1024), BF16),
        jax.ShapeDtypeStruct((n, LANES), BF16), jax.ShapeDtypeStruct((FOX_HEADS, n), F32),
    )
    return pl.pallas_call(
        _in_proj_kernel,
        grid=(n // rows,),
        in_specs=[row(D_MODEL), full(g), full(w), full(gq), full(wuq), full(gkv), full(wukv),
                  row(LANES), row(LANES)],
        out_specs=(row(HEAD_W), row(HEAD_W), row(HEAD_W), row(HEAD_W), row(256), row(1024), row(LANES),
                   pl.BlockSpec((FOX_HEADS, rows), lambda i: (0, i))),
        out_shape=out_shape,
        compiler_params=_params(("parallel",)),
        name="in_proj",
    )(x2d, g, w, gq, wuq, gkv, wukv, cc, ss)


def _fox_decay_kernel(fl_ref, b_ref, tri_ref, negc_ref):
    z = fl_ref[...] + b_ref[...]
    lf = jnp.minimum(z, 0.0) - jnp.log1p(jnp.exp(-jnp.abs(z)))
    seq = lf.shape[1]
    tri = tri_ref[...]
    carry = jnp.zeros((FOX_HEADS, 1), F32)
    zeros = jnp.zeros((FOX_HEADS, CUM_TILE), F32)
    for j in range(seq // CUM_TILE):
        v = lf[:, j * CUM_TILE:(j + 1) * CUM_TILE]
        hi = v.astype(BF16).astype(F32)
        r1 = v - hi
        mid = r1.astype(BF16).astype(F32)
        lo = r1 - mid
        parts = _dot(jnp.concatenate([hi, mid, lo, zeros], axis=0).astype(BF16), tri)
        cs = (parts[0:8] + parts[8:16]) + parts[16:24] + carry
        negc_ref[0, :, j * CUM_TILE:(j + 1) * CUM_TILE] = -cs
        carry = cs[:, CUM_TILE - 1:CUM_TILE]


def _fox_decay(flt, b_col, tri, batch, seq):
    return pl.pallas_call(
        _fox_decay_kernel,
        grid=(batch,),
        in_specs=[pl.BlockSpec((FOX_HEADS, seq), lambda b: (0, b)),
                  pl.BlockSpec((FOX_HEADS, 1), lambda b: (0, 0)),
                  pl.BlockSpec((CUM_TILE, CUM_TILE), lambda b: (0, 0))],
        out_specs=pl.BlockSpec((1, FOX_HEADS, seq), lambda b: (b, 0, 0)),
        out_shape=jax.ShapeDtypeStruct((batch, FOX_HEADS, seq), F32),
        compiler_params=_params(("parallel",)),
        name="fox_decay",
    )(flt, b_col, tri)


def _softmax_pv(s_off, s_diag, v_ref, qs, qe):
    m = jnp.max(s_diag, axis=-1, keepdims=True)
    if s_off is not None:
        m = jnp.maximum(m, jnp.max(s_off, axis=-1, keepdims=True))
    p_diag = jnp.exp(s_diag - m)
    l = jnp.sum(p_diag, axis=-1, keepdims=True)
    o = _dot(p_diag.astype(BF16), v_ref[qs:qe, :])
    if s_off is not None:
        p_off = jnp.exp(s_off - m)
        l = l + jnp.sum(p_off, axis=-1, keepdims=True)
        o = o + _dot(p_off.astype(BF16), v_ref[0:qs, :])
    return o / l


def _mla_attn_kernel(qn_ref, qpe_ref, kv_ref, kpe_ref, o_ref, qs_ref, ks_ref, vs_ref):
    j = pl.program_id(1)
    seq = qn_ref.shape[1]
    lane = lax.broadcasted_iota(jnp.int32, (seq, LANES), 1)
    qn = qn_ref[0]
    qpe = qpe_ref[0]
    kpe = kpe_ref[0]
    for hh in range(2):
        nope_mask = (lane >= 64) if hh == 0 else (lane < 64)
        pe_slot = 2 * (j % 2) + hh
        pe_mask = (lane // MLA_ROPE) == pe_slot
        qs_ref[hh, :, 0:LANES] = jnp.where(nope_mask, qn, jnp.zeros_like(qn))
        qs_ref[hh, :, LANES:2 * LANES] = jnp.where(pe_mask, qpe, jnp.zeros_like(qpe))
        kvh = kv_ref[0, :, hh * LANES:(hh + 1) * LANES]
        ks_ref[hh, :, 0:LANES] = kvh
        ks_ref[hh, :, LANES:2 * LANES] = kpe
        vs_ref[hh] = kvh

    row = lax.broadcasted_iota(jnp.int32, (Q_TILE, Q_TILE), 0)
    col = lax.broadcasted_iota(jnp.int32, (Q_TILE, Q_TILE), 1)
    allowed = (col // CHUNK) <= (row // CHUNK)
    out_lane = lax.broadcasted_iota(jnp.int32, (Q_TILE, LANES), 1)
    for i in range(seq // Q_TILE):
        qs, qe = i * Q_TILE, (i + 1) * Q_TILE
        outs = []
        for hh in range(2):
            q = qs_ref[hh, qs:qe, :]
            s_diag = jnp.where(allowed, _dot_nt(q, ks_ref[hh, qs:qe, :]), NEG_BIG)
            s_off = _dot_nt(q, ks_ref[hh, 0:qs, :]) if i > 0 else None
            outs.append(_softmax_pv(s_off, s_diag, vs_ref.at[hh], qs, qe))
        o_ref[0, qs:qe, :] = jnp.where(out_lane < 64, outs[0], outs[1]).astype(BF16)


def _mla_attn(qn, qpe, kv, kpe, batch, seq):
    qn, qpe, kv, kpe = (a.reshape(batch, seq, a.shape[-1]) for a in (qn, qpe, kv, kpe))
    return pl.pallas_call(
        _mla_attn_kernel,
        grid=(batch, MLA_HEADS // 2),
        in_specs=[pl.BlockSpec((1, seq, LANES), lambda b, j: (b, 0, j)),
                  pl.BlockSpec((1, seq, LANES), lambda b, j: (b, 0, j // 2)),
                  pl.BlockSpec((1, seq, 2 * LANES), lambda b, j: (b, 0, j)),
                  pl.BlockSpec((1, seq, LANES), lambda b, j: (b, 0, 0))],
        out_specs=pl.BlockSpec((1, seq, LANES), lambda b, j: (b, 0, j)),
        out_shape=jax.ShapeDtypeStruct((batch, seq, HEAD_W), BF16),
        scratch_shapes=[pltpu.VMEM((2, seq, 2 * LANES), BF16), pltpu.VMEM((2, seq, 2 * LANES), BF16),
                        pltpu.VMEM((2, seq, LANES), BF16)],
        compiler_params=_params(("parallel", "parallel")),
        name="mla_attn",
    )(qn, qpe, kv, kpe)


def _fox_attn_kernel(q_ref, k_ref, v_ref, negc_ref, o_ref, qs_ref):
    j = pl.program_id(1)
    seq = q_ref.shape[1]
    lane = lax.broadcasted_iota(jnp.int32, (seq, LANES), 1)
    q = q_ref[0]
    qs_ref[0] = jnp.where(lane < 64, q, jnp.zeros_like(q))
    qs_ref[1] = jnp.where(lane >= 64, q, jnp.zeros_like(q))

    row = lax.broadcasted_iota(jnp.int32, (Q_TILE, Q_TILE), 0)
    col = lax.broadcasted_iota(jnp.int32, (Q_TILE, Q_TILE), 1)
    allowed = col <= row
    out_lane = lax.broadcasted_iota(jnp.int32, (Q_TILE, LANES), 1)
    for i in range(seq // Q_TILE):
        qs, qe = i * Q_TILE, (i + 1) * Q_TILE
        outs = []
        for hh in range(2):
            negc = negc_ref[0, pl.ds(2 * j + hh, 1), :]
            qh = qs_ref[hh, qs:qe, :]
            s_diag = jnp.where(allowed, _dot_nt(qh, k_ref[0, qs:qe, :]) + negc[:, qs:qe], NEG_BIG)
            s_off = (_dot_nt(qh, k_ref[0, 0:qs, :]) + negc[:, 0:qs]) if i > 0 else None
            outs.append(_softmax_pv(s_off, s_diag, v_ref.at[0], qs, qe))
        o_ref[0, qs:qe, :] = jnp.where(out_lane < 64, outs[0], outs[1]).astype(BF16)


def _fox_attn(fq, fk, fv, negc, batch, seq):
    fq, fk, fv = (a.reshape(batch, seq, HEAD_W) for a in (fq, fk, fv))
    spec = pl.BlockSpec((1, seq, LANES), lambda b, j: (b, 0, j))
    return pl.pallas_call(
        _fox_attn_kernel,
        grid=(batch, FOX_HEADS // 2),
        in_specs=[spec, spec, spec, pl.BlockSpec((1, FOX_HEADS, seq), lambda b, j: (b, 0, 0))],
        out_specs=spec,
        out_shape=jax.ShapeDtypeStruct((batch, seq, HEAD_W), BF16),
        scratch_shapes=[pltpu.VMEM((2, seq, LANES), BF16)],
        compiler_params=_params(("parallel", "parallel")),
        name="fox_attn",
    )(fq, fk, fv, negc)


def _lane_max(v):
    return jnp.max(v, axis=1, keepdims=True)


def _first_lane(hit, lane_f):
    return jnp.min(jnp.where(hit, lane_f, float(LANES)), axis=1, keepdims=True)


def _out_proj_kernel(om_ref, of_ref, x_ref, gm_ref, gf_ref, wo_ref, gn_ref, wr_ref, br_ref, tri_ref,
                     x1_ref, h2_ref, rt_ref, cnt_ref, carry_ref):
    @pl.when(pl.program_id(0) == 0)
    def _():
        carry_ref[...] = jnp.zeros_like(carry_ref)

    a = _rms(om_ref[...].astype(F32), gm_ref[...]).astype(BF16)
    b = _rms(of_ref[...].astype(F32), gf_ref[...]).astype(BF16)
    x1 = x_ref[...] + _dot(a, wo_ref[0:HEAD_W, :]) + _dot(b, wo_ref[HEAD_W:2 * HEAD_W, :])
    x1_ref[...] = x1
    h2 = _rms(x1, gn_ref[...])
    h_hi = h2.astype(BF16)
    h2_ref[...] = _pack_rows(h_hi.astype(F32))
    h_lo = (h2 - h_hi.astype(F32)).astype(BF16)
    t = _dot(h_hi, wr_ref[...])
    lg = (t[:, 0:LANES] + t[:, LANES:2 * LANES]) + _dot(h_lo, wr_ref[:, 0:LANES]) + br_ref[...]

    lane = lax.broadcasted_iota(jnp.int32, lg.shape, 1)
    lane_f = lane.astype(F32)
    neg_inf = float("-inf")
    is_group = lane < N_GROUPS
    gl = jnp.where(is_group, lg, neg_inf)
    mg = _lane_max(gl)
    gi = _first_lane(gl == mg, lane_f)
    g_val = 1.0 / jnp.sum(jnp.where(is_group, jnp.exp(lg - mg), 0.0), axis=1, keepdims=True)
    group_of_lane = ((lane - N_GROUPS) >> 3).astype(F32)
    is_expert = (lane >= N_GROUPS) & (lane < N_GROUPS + N_EXPERTS) & (group_of_lane == gi)
    el = jnp.where(is_expert, lg, neg_inf)
    m1 = _lane_max(el)
    i1 = _first_lane(el == m1, lane_f)
    el2 = jnp.where(lane_f == i1, neg_inf, el)
    m2 = _lane_max(el2)
    i2 = _first_lane(el2 == m2, lane_f)
    r = jnp.exp(m2 - m1)
    g0 = g_val / (1.0 + r)
    g1 = g0 * r

    hit1 = lane_f == i1
    hit2 = lane_f == i2
    onehot = jnp.where(hit1 | hit2, 1.0, 0.0)
    carry = carry_ref[...]
    before = _dot(tri_ref[...], onehot.astype(BF16)) + carry
    r0 = jnp.sum(jnp.where(hit1, before, 0.0), axis=1, keepdims=True)
    r1 = jnp.sum(jnp.where(hit2, before, 0.0), axis=1, keepdims=True)
    carry = carry + jnp.sum(onehot, axis=0, keepdims=True)
    carry_ref[...] = carry
    cnt_ref[...] = carry

    vals = (i1 - N_GROUPS, i2 - N_GROUPS, g0, g1, r0, r1)
    rt = jnp.zeros_like(lg)
    for k, v in enumerate(vals):
        rt = jnp.where(lane == k, v, rt)
    rt_ref[...] = rt


def _out_proj(om, of, x2d, gm, gf, wo, gn, wr, br, tri):
    n = x2d.shape[0]
    rows = tri.shape[0]
    row = lambda c: pl.BlockSpec((rows, c), lambda i: (i, 0))
    full = lambda a: pl.BlockSpec(a.shape, lambda i: (0,) * a.ndim)
    return pl.pallas_call(
        _out_proj_kernel,
        grid=(n // rows,),
        in_specs=[row(HEAD_W), row(HEAD_W), row(D_MODEL), full(gm), full(gf), full(wo), full(gn), full(wr),
                  full(br), full(tri)],
        out_specs=(row(D_MODEL), row(D_MODEL // 2), row(LANES), pl.BlockSpec((1, LANES), lambda i: (0, 0))),
        out_shape=(jax.ShapeDtypeStruct((n, D_MODEL), F32), jax.ShapeDtypeStruct((n, D_MODEL // 2), jnp.uint32),
                   jax.ShapeDtypeStruct((n, LANES), F32), jax.ShapeDtypeStruct((1, LANES), F32)),
        scratch_shapes=[pltpu.VMEM((1, LANES), F32)],
        compiler_params=_params(("arbitrary",)),
        name="out_proj",
    )(om, of, x2d, gm, gf, wo, gn, wr, br, tri)


def _dispatch_kernel(dest_ref, h2_ref, xs_zero_ref, xs_ref, sem):
    del xs_zero_ref
    rows = h2_ref.shape[0]

    def copy(r, k):
        slot = dest_ref[0, 0, TOP_K * r + k]
        return pltpu.make_async_copy(h2_ref.at[pl.ds(r, 1)], xs_ref.at[pl.ds(slot, 1)], sem)

    def start(r, c):
        for k in range(TOP_K):
            copy(r, k).start()
        return c

    def wait(r, c):
        for k in range(TOP_K):
            copy(r, k).wait()
        return c

    lax.fori_loop(0, rows, start, 0, unroll=8)
    lax.fori_loop(0, rows, wait, 0, unroll=8)


def _dispatch(h2p, dest, n_slots):
    n, w = h2p.shape
    rows = min(ROW_TILE, n)
    dest3 = dest.reshape(n // rows, 1, TOP_K * rows)
    return pl.pallas_call(
        _dispatch_kernel,
        grid=(n // rows,),
        in_specs=[pl.BlockSpec((1, 1, TOP_K * rows), lambda i: (i, 0, 0), memory_space=pltpu.SMEM),
                  pl.BlockSpec((rows, w), lambda i: (i, 0)),
                  pl.BlockSpec(memory_space=pl.ANY)],
        out_specs=pl.BlockSpec(memory_space=pl.ANY),
        out_shape=jax.ShapeDtypeStruct((n_slots, w), jnp.uint32),
        scratch_shapes=[pltpu.SemaphoreType.DMA],
        input_output_aliases={2: 0},
        compiler_params=_params(("arbitrary",)),
        name="dispatch",
    )(dest3, h2p, jnp.zeros((n_slots, w), jnp.uint32))


def _experts_kernel(be_ref, bv_ref, xs_ref, wg_ref, wu_ref, wd_ref, o_ref, wg_s, wu_s, wd_s):
    i = pl.program_id(0)
    valid = bv_ref[i] != 0

    @pl.when(jnp.logical_and(valid, jnp.logical_or(i == 0, be_ref[i] != be_ref[jnp.maximum(i - 1, 0)])))
    def _():
        wg_s[...] = wg_ref[0, 0].astype(BF16)
        wu_s[...] = wu_ref[0, 0].astype(BF16)
        wd_s[...] = wd_ref[0, 0].astype(BF16)

    @pl.when(valid)
    def _():
        x = _unpack_rows(xs_ref[...]).astype(BF16)
        g = _dot(x, wg_s[...])
        u = _dot(x, wu_s[...])
        act = ((g * jax.nn.sigmoid(g)) * u).astype(BF16)
        o_ref[...] = _dot(act, wd_s[...]).astype(BF16)

    @pl.when(jnp.logical_not(valid))
    def _():
        o_ref[...] = jnp.zeros_like(o_ref)


def _experts(layer, blk_expert, blk_valid, xs, w_gate, w_up, w_down):
    n_slots = xs.shape[0]
    n_blocks = n_slots // EXPERT_TILE
    grid_spec = pltpu.PrefetchScalarGridSpec(
        num_scalar_prefetch=2,
        grid=(n_blocks,),
        in_specs=[pl.BlockSpec((EXPERT_TILE, D_MODEL // 2), lambda i, be, bv: (i, 0)),
                  pl.BlockSpec((1, 1, D_MODEL, D_EXPERT), lambda i, be, bv: (layer, be[i], 0, 0)),
                  pl.BlockSpec((1, 1, D_MODEL, D_EXPERT), lambda i, be, bv: (layer, be[i], 0, 0)),
                  pl.BlockSpec((1, 1, D_EXPERT, D_MODEL), lambda i, be, bv: (layer, be[i], 0, 0))],
        out_specs=pl.BlockSpec((EXPERT_TILE, D_MODEL), lambda i, be, bv: (i, 0)),
        scratch_shapes=[pltpu.VMEM((D_MODEL, D_EXPERT), BF16), pltpu.VMEM((D_MODEL, D_EXPERT), BF16),
                        pltpu.VMEM((D_EXPERT, D_MODEL), BF16)],
    )
    return pl.pallas_call(
        _experts_kernel,
        grid_spec=grid_spec,
        out_shape=jax.ShapeDtypeStruct((n_slots, D_MODEL), BF16),
        compiler_params=_params(("arbitrary",)),
        name="experts",
    )(blk_expert, blk_valid, xs, w_gate, w_up, w_down)


def _combine_kernel(x_ref, y0_ref, y1_ref, rt_ref, g_ref, o_ref, *, final):
    g0 = rt_ref[:, TOP_K:TOP_K + 1]
    g1 = rt_ref[:, TOP_K + 1:TOP_K + 2]
    x = x_ref[...] + (y0_ref[...].astype(F32) * g0 + y1_ref[...].astype(F32) * g1)
    o_ref[...] = _rms(x, g_ref[...]) if final else x


def _combine(x1, y0, y1, route, g, final):
    n = x1.shape[0]
    rows = min(ROW_TILE, n)
    row = pl.BlockSpec((rows, D_MODEL), lambda i: (i, 0))
    return pl.pallas_call(
        functools.partial(_combine_kernel, final=final),
        grid=(n // rows,),
        in_specs=[row, row, row, pl.BlockSpec((rows, LANES), lambda i: (i, 0)),
                  pl.BlockSpec((1, D_MODEL), lambda i: (0, 0))],
        out_specs=row,
        out_shape=jax.ShapeDtypeStruct((n, D_MODEL), F32),
        compiler_params=_params(("parallel",)),
        name="combine",
    )(x1, y0, y1, route, g)


def _swap_halves(w):
    half = w.shape[-1] // 2
    return jnp.concatenate([w[..., half:], w[..., :half]], axis=-1)


def _prep_in_weights(w_in):
    d = w_in.shape[0]
    q_lat, kv_lat, kr = w_in[:, 0:256], w_in[:, 256:384], w_in[:, 384:416]
    fq, fk, fv, fl = w_in[:, 416:928], w_in[:, 928:1440], w_in[:, 1440:1952], w_in[:, 1952:1960]
    slab_a = jnp.concatenate([kr, fl, jnp.zeros((d, LANES - MLA_ROPE - FOX_HEADS), F32)], axis=1)
    slab_b = jnp.concatenate([_swap_halves(kr), jnp.zeros((d, LANES - MLA_ROPE), F32)], axis=1)
    return jnp.concatenate([fq * (FOX_DIM ** -0.5), fk, fv, q_lat, kv_lat, slab_a, slab_b], axis=1).astype(BF16)


def _prep_uq(w_uq):
    w = w_uq.reshape(MLA_Q_RANK, MLA_HEADS, MLA_NOPE + MLA_ROPE)
    nope, pe = w[:, :, :MLA_NOPE], w[:, :, MLA_NOPE:]
    pairs = nope.reshape(MLA_Q_RANK, MLA_HEADS // 2, 2, MLA_NOPE)[:, :, ::-1, :].reshape(MLA_Q_RANK, -1)
    return jnp.concatenate([pairs, pe.reshape(MLA_Q_RANK, -1), _swap_halves(pe).reshape(MLA_Q_RANK, -1)],
                           axis=1).astype(BF16)


def _prep_ukv(w_ukv):
    w = w_ukv.reshape(MLA_KV_RANK, MLA_HEADS // 2, 2, 2, MLA_NOPE)
    even = w[:, :, 0, ::-1, :]
    odd = w[:, :, 1, :, :]
    return jnp.stack([even, odd], axis=2).reshape(MLA_KV_RANK, -1).astype(BF16)


def _prep_router(w_rg, b_rg, w_re, b_re):
    d = w_rg.shape[0]
    w = jnp.concatenate([w_rg, w_re, jnp.zeros((d, LANES - N_GROUPS - N_EXPERTS), F32)], axis=1)
    w_hi = w.astype(BF16)
    w_lo = (w - w_hi.astype(F32)).astype(BF16)
    b = jnp.concatenate([b_rg, b_re, jnp.zeros((LANES - N_GROUPS - N_EXPERTS,), F32)])[None, :]
    return jnp.concatenate([w_hi, w_lo], axis=1), b


def _rope_slabs(positions):
    half = MLA_ROPE // 2
    inv_freq = ROPE_THETA ** (-jnp.arange(half, dtype=F32) / half)
    ang = positions.astype(F32).reshape(-1)[:, None] * inv_freq
    cos, sin = jnp.cos(ang), jnp.sin(ang)
    reps = LANES // MLA_ROPE
    return jnp.tile(jnp.concatenate([cos, cos], axis=1), (1, reps)), jnp.tile(
        jnp.concatenate([-sin, sin], axis=1), (1, reps))


def _slot_layout(route, counts, n_blocks):
    eid = route[:, 0:TOP_K].astype(jnp.int32)
    rank = route[:, 2 * TOP_K:3 * TOP_K].astype(jnp.int32)
    counts = counts[0, N_GROUPS:N_GROUPS + N_EXPERTS].astype(jnp.int32)
    padded = (counts + EXPERT_TILE - 1) // EXPERT_TILE * EXPERT_TILE
    pad_end = jnp.cumsum(padded)
    pad_start = pad_end - padded
    experts = jnp.arange(N_EXPERTS, dtype=jnp.int32)
    dest = jnp.sum(jnp.where(eid[:, :, None] == experts, pad_start, 0), axis=-1) + rank
    blk_start = jnp.arange(n_blocks, dtype=jnp.int32) * EXPERT_TILE
    blk_expert = jnp.minimum(jnp.sum((blk_start[:, None] >= pad_end[None, :]).astype(jnp.int32), axis=1),
                             N_EXPERTS - 1)
    blk_valid = (blk_start < pad_end[-1]).astype(jnp.int32)
    return dest, blk_expert, blk_valid


def kernel(x, positions, attn_norm, w_in, b_forget, q_norm, w_uq, kv_norm, w_ukv, mla_out_norm, fox_out_norm,
           w_out, ffn_norm, w_router_group, b_router_group, w_router_expert, b_router_expert, w_gate, w_up,
           w_down, final_norm):
    batch, seq, d = x.shape
    n = batch * seq
    depth = w_in.shape[0]
    n_blocks = -(-(n * TOP_K) // EXPERT_TILE) + N_EXPERTS
    cc, ss = _rope_slabs(positions)
    tri = (jnp.arange(CUM_TILE)[:, None] <= jnp.arange(CUM_TILE)[None, :]).astype(BF16)
    rows = min(ROW_TILE, n)
    tri_rows = (jnp.arange(rows)[None, :] < jnp.arange(rows)[:, None]).astype(BF16)
    xf = x.reshape(n, d)
    for l in range(depth):
        fq, fk, fv, qn, qpe, kv, kpe, flt = _in_proj(
            xf, attn_norm[l][None, :], _prep_in_weights(w_in[l]), q_norm[l][None, :], _prep_uq(w_uq[l]),
            kv_norm[l][None, :], _prep_ukv(w_ukv[l]), cc, ss)
        negc = _fox_decay(flt, b_forget[l][:, None], tri, batch, seq)
        o_mla = _mla_attn(qn, qpe, kv, kpe, batch, seq).reshape(n, HEAD_W)
        o_fox = _fox_attn(fq, fk, fv, negc, batch, seq).reshape(n, HEAD_W)
        wr, br = _prep_router(w_router_group[l], b_router_group[l], w_router_expert[l], b_router_expert[l])
        x1, h2, route, counts = _out_proj(o_mla, o_fox, xf, mla_out_norm[l][None, :], fox_out_norm[l][None, :],
                                          w_out[l].astype(BF16), ffn_norm[l][None, :], wr, br, tri_rows)
        dest, blk_expert, blk_valid = _slot_layout(route, counts, n_blocks)
        xs = _dispatch(h2, dest, n_blocks * EXPERT_TILE)
        ys = _experts(l, blk_expert, blk_valid, xs, w_gate, w_up, w_down)
        y0 = jnp.take(ys, dest[:, 0], axis=0)
        y1 = jnp.take(ys, dest[:, 1], axis=0)
        final = l == depth - 1
        xf = _combine(x1, y0, y1, route, final_norm[None, :] if final else ffn_norm[l][None, :], final)
    return xf.reshape(batch, seq, d)
```

```python
import functools

import jax
import jax.numpy as jnp
from jax import lax
from jax.experimental import pallas as pl
from jax.experimental.pallas import tpu as pltpu

D_MODEL = 1024
CHUNK = 64
MLA_HEADS = 8
MLA_NOPE = 64
MLA_ROPE = 32
MLA_V = 64
MLA_Q_RANK = 256
MLA_KV_RANK = 128
ROPE_THETA = 10000.0
FOX_HEADS = 8
FOX_DIM = 64
HEAD_W = 512
N_GROUPS = 4
EXPERTS_PER_GROUP = 8
N_EXPERTS = 32
TOP_K = 2
D_EXPERT = 512
NORM_EPS = 1e-6

LANES = 128
IN_COLS = 2176
ROW_TILE = 512
SUB_TILES = 2
Q_TILE = 256
EXPERT_TILE = 256
CUM_TILE = 256
NEG_BIG = -1e30
LOG2E = 1.4426950408889634
VMEM_LIMIT = 48 * 1024 * 1024

F32 = jnp.float32
BF16 = jnp.bfloat16


def _rms(x, g):
    return (x * lax.rsqrt(jnp.mean(x * x, axis=-1, keepdims=True) + NORM_EPS)) * g


def _dot(a, b):
    return jnp.dot(a, b, preferred_element_type=F32)


def _dot_nt(a, b):
    return lax.dot_general(a, b, (((1,), (1,)), ((), ())), preferred_element_type=F32)


def _params(sem):
    return pltpu.CompilerParams(dimension_semantics=sem, vmem_limit_bytes=VMEM_LIMIT)


def _sub_slices(rows):
    sub = rows // SUB_TILES
    return [slice(t * sub, (t + 1) * sub) for t in range(SUB_TILES)]


def _pack_rows(v):
    w = v.shape[1] // 2
    lo = lax.bitcast_convert_type(v[:, :w], jnp.uint32) >> 16
    hi = lax.bitcast_convert_type(v[:, w:], jnp.uint32) & jnp.uint32(0xFFFF0000)
    return hi | lo


def _unpack_rows(p):
    lo = lax.bitcast_convert_type(p << 16, F32)
    hi = lax.bitcast_convert_type(p & jnp.uint32(0xFFFF0000), F32)
    return jnp.concatenate([lo, hi], axis=1)


def _in_proj_kernel(x_ref, g_ref, w_ref, gq_ref, wuq_ref, gkv_ref, wukv_ref, cc_ref, ss_ref,
                    fq_ref, fk_ref, fv_ref, qn_ref, qpe_ref, kv_ref, kpe_ref, flt_ref):
    subs = _sub_slices(x_ref.shape[0])
    ps = [_dot(_rms(x_ref[sl, :], g_ref[...]).astype(BF16), w_ref[...]) for sl in subs]
    scale = (MLA_NOPE + MLA_ROPE) ** -0.5 * LOG2E
    for sl, p in zip(subs, ps):
        fq_ref[sl, :] = (p[:, 0:512] * LOG2E).astype(BF16)
        fk_ref[sl, :] = p[:, 512:1024].astype(BF16)
        fv_ref[sl, :] = p[:, 1024:1536].astype(BF16)

        cc = cc_ref[sl, :]
        ss = ss_ref[sl, :]
        q = _dot(_rms(p[:, 1536:1792], gq_ref[...]).astype(BF16), wuq_ref[...])
        qn_ref[sl, :] = (q[:, 0:512] * scale).astype(BF16)
        cc2 = jnp.concatenate([cc, cc], axis=1)
        ss2 = jnp.concatenate([ss, ss], axis=1)
        qpe_ref[sl, :] = ((q[:, 512:768] * cc2 + q[:, 768:1024] * ss2) * scale).astype(BF16)

        kv_ref[sl, :] = _dot(_rms(p[:, 1792:1920], gkv_ref[...]).astype(BF16), wukv_ref[...]).astype(BF16)

        slab_a = p[:, 1920:2048]
        slab_b = p[:, 2048:2176]
        lane = lax.broadcasted_iota(jnp.int32, slab_a.shape, 1)
        roped = jnp.where(lane < MLA_ROPE, slab_a * cc + slab_b * ss, 0.0)
        tiled = roped + pltpu.roll(roped, 32, 1) + pltpu.roll(roped, 64, 1) + pltpu.roll(roped, 96, 1)
        kpe_ref[sl, :] = tiled.astype(BF16)
        flt_ref[:, sl] = slab_a.T[MLA_ROPE:MLA_ROPE + FOX_HEADS, :]


def _in_proj(x2d, g, w, gq, wuq, gkv, wukv, cc, ss):
    n = x2d.shape[0]
    rows = min(ROW_TILE, n)
    row = lambda c: pl.BlockSpec((rows, c), lambda i: (i, 0))
    full = lambda a: pl.BlockSpec(a.shape, lambda i: (0,) * a.ndim)
    out_shape = (
        jax.ShapeDtypeStruct((n, HEAD_W), BF16), jax.ShapeDtypeStruct((n, HEAD_W), BF16),
        jax.ShapeDtypeStruct((n, HEAD_W), BF16), jax.ShapeDtypeStruct((n, HEAD_W), BF16),
        jax.ShapeDtypeStruct((n, 256), BF16), jax.ShapeDtypeStruct((n, 1024), BF16),
        jax.ShapeDtypeStruct((n, LANES), BF16), jax.ShapeDtypeStruct((FOX_HEADS, n), F32),
    )
    return pl.pallas_call(
        _in_proj_kernel,
        grid=(n // rows,),
        in_specs=[row(D_MODEL), full(g), full(w), full(gq), full(wuq), full(gkv), full(wukv),
                  row(LANES), row(LANES)],
        out_specs=(row(HEAD_W), row(HEAD_W), row(HEAD_W), row(HEAD_W), row(256), row(1024), row(LANES),
                   pl.BlockSpec((FOX_HEADS, rows), lambda i: (0, i))),
        out_shape=out_shape,
        compiler_params=_params(("parallel",)),
        name="in_proj",
    )(x2d, g, w, gq, wuq, gkv, wukv, cc, ss)


def _fox_decay_kernel(fl_ref, b_ref, tri_ref, ck_ref):
    z = fl_ref[...] + b_ref[...]
    lf = jnp.minimum(z, 0.0) - jnp.log1p(jnp.exp(-jnp.abs(z)))
    seq = lf.shape[1]
    tri = tri_ref[...]
    carry = jnp.zeros((FOX_HEADS, 1), F32)
    zeros = jnp.zeros((FOX_HEADS, CUM_TILE), F32)
    for j in range(seq // CUM_TILE):
        v = lf[:, j * CUM_TILE:(j + 1) * CUM_TILE]
        hi = v.astype(BF16).astype(F32)
        r1 = v - hi
        mid = r1.astype(BF16).astype(F32)
        lo = r1 - mid
        parts = _dot(jnp.concatenate([hi, mid, lo, zeros], axis=0).astype(BF16), tri)
        cs = (parts[0:8] + parts[8:16]) + parts[16:24] + carry
        carry = cs[:, CUM_TILE - 1:CUM_TILE]
        d = cs * (-LOG2E)
        d_hi = d.astype(BF16).astype(F32)
        d_r = d - d_hi
        d_mid = d_r.astype(BF16).astype(F32)
        d_lo = d_r - d_mid
        rows = jnp.concatenate([d_hi, d_mid, d_lo, jnp.zeros((LANES - 3 * FOX_HEADS, CUM_TILE), F32)], axis=0)
        ck_ref[0, j * CUM_TILE:(j + 1) * CUM_TILE, :] = rows.T.astype(BF16)


def _fox_decay(flt, b_col, tri, batch, seq):
    return pl.pallas_call(
        _fox_decay_kernel,
        grid=(batch,),
        in_specs=[pl.BlockSpec((FOX_HEADS, seq), lambda b: (0, b)),
                  pl.BlockSpec((FOX_HEADS, 1), lambda b: (0, 0)),
                  pl.BlockSpec((CUM_TILE, CUM_TILE), lambda b: (0, 0))],
        out_specs=pl.BlockSpec((1, seq, LANES), lambda b: (b, 0, 0)),
        out_shape=jax.ShapeDtypeStruct((batch, seq, LANES), BF16),
        compiler_params=_params(("parallel",)),
        name="fox_decay",
    )(flt, b_col, tri)


def _probs(s_off, s_diag):
    m = jnp.max(s_diag, axis=-1, keepdims=True)
    if s_off is not None:
        m = jnp.maximum(m, jnp.max(s_off, axis=-1, keepdims=True))
    p_off = None if s_off is None else jnp.exp2(s_off - m).astype(BF16)
    return p_off, jnp.exp2(s_diag - m).astype(BF16)


def _attend(q_ref, k_ref, v_ref, o_ref, allowed):
    seq = q_ref.shape[1]
    out_lane = lax.broadcasted_iota(jnp.int32, (Q_TILE, LANES), 1)
    units = [(i, hh) for i in range(seq // Q_TILE) for hh in range(2)]

    def scores(i, hh):
        qs, qe = i * Q_TILE, (i + 1) * Q_TILE
        q = q_ref[hh, qs:qe, :]
        s_diag = jnp.where(allowed, _dot_nt(q, k_ref[hh, qs:qe, :]), NEG_BIG)
        s_off = _dot_nt(q, k_ref[hh, 0:qs, :]) if i > 0 else None
        return s_off, s_diag

    def values(i, hh, p_off, p_diag):
        qs, qe = i * Q_TILE, (i + 1) * Q_TILE
        acc = _dot(p_diag, v_ref[hh, qs:qe, :])
        if p_off is not None:
            acc = acc + _dot(p_off, v_ref[hh, 0:qs, :])
        return acc[:, 0:LANES] / acc[:, LANES:2 * LANES]

    n_units = len(units)
    s = {0: scores(*units[0])}
    if n_units > 1:
        s[1] = scores(*units[1])
    p = {0: _probs(*s.pop(0))}
    outs = []
    for n, (i, hh) in enumerate(units):
        if n + 2 < n_units:
            s[n + 2] = scores(*units[n + 2])
        if n + 1 < n_units:
            p[n + 1] = _probs(*s.pop(n + 1))
        outs.append(values(i, hh, *p.pop(n)))
        if hh == 1:
            o_ref[0, i * Q_TILE:(i + 1) * Q_TILE, :] = jnp.where(out_lane < 64, outs[0], outs[1]).astype(BF16)
            outs = []


def _tile_iota():
    row = lax.broadcasted_iota(jnp.int32, (Q_TILE, Q_TILE), 0)
    col = lax.broadcasted_iota(jnp.int32, (Q_TILE, Q_TILE), 1)
    return row, col


def _mla_attn_kernel(qn_ref, qpe_ref, kv_ref, kpe_ref, o_ref, qs_ref, ks_ref, vs_ref):
    j = pl.program_id(1)
    seq = qn_ref.shape[1]
    lane = lax.broadcasted_iota(jnp.int32, (seq, LANES), 1)
    qn = qn_ref[0]
    qpe = qpe_ref[0]
    kpe = kpe_ref[0]
    ones = jnp.ones((seq, LANES), BF16)
    for hh in range(2):
        nope_mask = (lane >= 64) if hh == 0 else (lane < 64)
        pe_slot = 2 * (j % 2) + hh
        pe_mask = (lane // MLA_ROPE) == pe_slot
        qs_ref[hh, :, 0:LANES] = jnp.where(nope_mask, qn, jnp.zeros_like(qn))
        qs_ref[hh, :, LANES:2 * LANES] = jnp.where(pe_mask, qpe, jnp.zeros_like(qpe))
        kvh = kv_ref[0, :, hh * LANES:(hh + 1) * LANES]
        ks_ref[hh, :, 0:LANES] = kvh
        ks_ref[hh, :, LANES:2 * LANES] = kpe
        vs_ref[hh, :, 0:LANES] = kvh
        vs_ref[hh, :, LANES:2 * LANES] = ones
    row, col = _tile_iota()
    _attend(qs_ref, ks_ref, vs_ref, o_ref, (col // CHUNK) <= (row // CHUNK))


def _mla_attn(qn, qpe, kv, kpe, batch, seq):
    qn, qpe, kv, kpe = (a.reshape(batch, seq, a.shape[-1]) for a in (qn, qpe, kv, kpe))
    pair = pltpu.VMEM((2, seq, 2 * LANES), BF16)
    return pl.pallas_call(
        _mla_attn_kernel,
        grid=(batch, MLA_HEADS // 2),
        in_specs=[pl.BlockSpec((1, seq, LANES), lambda b, j: (b, 0, j)),
                  pl.BlockSpec((1, seq, LANES), lambda b, j: (b, 0, j // 2)),
                  pl.BlockSpec((1, seq, 2 * LANES), lambda b, j: (b, 0, j)),
                  pl.BlockSpec((1, seq, LANES), lambda b, j: (b, 0, 0))],
        out_specs=pl.BlockSpec((1, seq, LANES), lambda b, j: (b, 0, j)),
        out_shape=jax.ShapeDtypeStruct((batch, seq, HEAD_W), BF16),
        scratch_shapes=[pair, pair, pair],
        compiler_params=_params(("parallel", "parallel")),
        name="mla_attn",
    )(qn, qpe, kv, kpe)


def _fox_attn_kernel(q_ref, k_ref, v_ref, ck_ref, o_ref, qs_ref, ks_ref, vs_ref):
    j = pl.program_id(1)
    seq = q_ref.shape[1]
    lane = lax.broadcasted_iota(jnp.int32, (seq, LANES), 1)
    q = q_ref[0]
    k = k_ref[0]
    v = v_ref[0]
    ck = ck_ref[0]
    ones = jnp.ones((seq, LANES), BF16)
    for hh in range(2):
        head = 2 * j + hh
        head_mask = (lane < 64) if hh == 0 else (lane >= 64)
        piece = jnp.where(lane < 3 * FOX_HEADS, lane % FOX_HEADS, -1) == head
        qs_ref[hh, :, 0:LANES] = jnp.where(head_mask, q, jnp.zeros_like(q))
        qs_ref[hh, :, LANES:2 * LANES] = jnp.where(piece, 1.0, 0.0).astype(BF16)
        ks_ref[hh, :, 0:LANES] = k
        ks_ref[hh, :, LANES:2 * LANES] = ck
        vs_ref[hh, :, 0:LANES] = v
        vs_ref[hh, :, LANES:2 * LANES] = ones
    row, col = _tile_iota()
    _attend(qs_ref, ks_ref, vs_ref, o_ref, col <= row)


def _fox_attn(fq, fk, fv, ck, batch, seq):
    fq, fk, fv = (a.reshape(batch, seq, HEAD_W) for a in (fq, fk, fv))
    spec = pl.BlockSpec((1, seq, LANES), lambda b, j: (b, 0, j))
    pair = pltpu.VMEM((2, seq, 2 * LANES), BF16)
    return pl.pallas_call(
        _fox_attn_kernel,
        grid=(batch, FOX_HEADS // 2),
        in_specs=[spec, spec, spec, pl.BlockSpec((1, seq, LANES), lambda b, j: (b, 0, 0))],
        out_specs=spec,
        out_shape=jax.ShapeDtypeStruct((batch, seq, HEAD_W), BF16),
        scratch_shapes=[pair, pair, pair],
        compiler_params=_params(("parallel", "parallel")),
        name="fox_attn",
    )(fq, fk, fv, ck)


def _lane_max(v):
    return jnp.max(v, axis=1, keepdims=True)


def _first_lane(hit, lane_f):
    return jnp.min(jnp.where(hit, lane_f, float(LANES)), axis=1, keepdims=True)


def _out_proj_kernel(om_ref, of_ref, x_ref, gm_ref, gf_ref, wo_ref, gn_ref, wr_ref, br_ref, tri_ref,
                     x1_ref, h2_ref, rt_ref, cnt_ref, carry_ref):
    @pl.when(pl.program_id(0) == 0)
    def _():
        carry_ref[...] = jnp.zeros_like(carry_ref)

    subs = _sub_slices(x_ref.shape[0])
    logits = [_project_out(sl, om_ref, of_ref, x_ref, gm_ref, gf_ref, wo_ref, gn_ref, wr_ref, br_ref, x1_ref, h2_ref)
              for sl in subs]
    carry = carry_ref[...]
    for sl, lg in zip(subs, logits):
        rt_ref[sl, :], carry = _route_rows(lg, tri_ref[...], carry)
    carry_ref[...] = carry
    cnt_ref[...] = carry


def _project_out(sl, om_ref, of_ref, x_ref, gm_ref, gf_ref, wo_ref, gn_ref, wr_ref, br_ref, x1_ref, h2_ref):
    a = _rms(om_ref[sl, :].astype(F32), gm_ref[...]).astype(BF16)
    b = _rms(of_ref[sl, :].astype(F32), gf_ref[...]).astype(BF16)
    x1 = x_ref[sl, :] + _dot(a, wo_ref[0:HEAD_W, :]) + _dot(b, wo_ref[HEAD_W:2 * HEAD_W, :])
    x1_ref[sl, :] = x1
    h2 = _rms(x1, gn_ref[...])
    h_hi = h2.astype(BF16)
    h2_ref[sl, :] = _pack_rows(h_hi.astype(F32))
    h_lo = (h2 - h_hi.astype(F32)).astype(BF16)
    t = _dot(h_hi, wr_ref[...])
    return (t[:, 0:LANES] + t[:, LANES:2 * LANES]) + _dot(h_lo, wr_ref[:, 0:LANES]) + br_ref[...]


def _route_rows(lg, tri, carry):
    lane = lax.broadcasted_iota(jnp.int32, lg.shape, 1)
    lane_f = lane.astype(F32)
    neg_inf = float("-inf")
    is_group = lane < N_GROUPS
    gl = jnp.where(is_group, lg, neg_inf)
    mg = _lane_max(gl)
    gi = _first_lane(gl == mg, lane_f)
    g_val = 1.0 / jnp.sum(jnp.where(is_group, jnp.exp(lg - mg), 0.0), axis=1, keepdims=True)
    group_of_lane = ((lane - N_GROUPS) >> 3).astype(F32)
    is_expert = (lane >= N_GROUPS) & (lane < N_GROUPS + N_EXPERTS) & (group_of_lane == gi)
    el = jnp.where(is_expert, lg, neg_inf)
    m1 = _lane_max(el)
    i1 = _first_lane(el == m1, lane_f)
    el2 = jnp.where(lane_f == i1, neg_inf, el)
    m2 = _lane_max(el2)
    i2 = _first_lane(el2 == m2, lane_f)
    r = jnp.exp(m2 - m1)
    g0 = g_val / (1.0 + r)
    g1 = g0 * r

    hit1 = lane_f == i1
    hit2 = lane_f == i2
    onehot = jnp.where(hit1 | hit2, 1.0, 0.0)
    before = _dot(tri, onehot.astype(BF16)) + carry
    r0 = jnp.sum(jnp.where(hit1, before, 0.0), axis=1, keepdims=True)
    r1 = jnp.sum(jnp.where(hit2, before, 0.0), axis=1, keepdims=True)

    vals = (i1 - N_GROUPS, i2 - N_GROUPS, g0, g1, r0, r1)
    rt = jnp.zeros_like(lg)
    for k, v in enumerate(vals):
        rt = jnp.where(lane == k, v, rt)
    return rt, carry + jnp.sum(onehot, axis=0, keepdims=True)


def _out_proj(om, of, x2d, gm, gf, wo, gn, wr, br, tri):
    n = x2d.shape[0]
    rows = tri.shape[0] * SUB_TILES
    row = lambda c: pl.BlockSpec((rows, c), lambda i: (i, 0))
    full = lambda a: pl.BlockSpec(a.shape, lambda i: (0,) * a.ndim)
    return pl.pallas_call(
        _out_proj_kernel,
        grid=(n // rows,),
        in_specs=[row(HEAD_W), row(HEAD_W), row(D_MODEL), full(gm), full(gf), full(wo), full(gn), full(wr),
                  full(br), full(tri)],
        out_specs=(row(D_MODEL), row(D_MODEL // 2), row(LANES), pl.BlockSpec((1, LANES), lambda i: (0, 0))),
        out_shape=(jax.ShapeDtypeStruct((n, D_MODEL), F32), jax.ShapeDtypeStruct((n, D_MODEL // 2), jnp.uint32),
                   jax.ShapeDtypeStruct((n, LANES), F32), jax.ShapeDtypeStruct((1, LANES), F32)),
        scratch_shapes=[pltpu.VMEM((1, LANES), F32)],
        compiler_params=_params(("arbitrary",)),
        name="out_proj",
    )(om, of, x2d, gm, gf, wo, gn, wr, br, tri)


def _dispatch_kernel(dest_ref, h2_ref, xs_zero_ref, xs_ref, sem):
    del xs_zero_ref
    rows = h2_ref.shape[0]

    def copy(r, k):
        slot = dest_ref[0, 0, TOP_K * r + k]
        return pltpu.make_async_copy(h2_ref.at[pl.ds(r, 1)], xs_ref.at[pl.ds(slot, 1)], sem)

    def start(r, c):
        for k in range(TOP_K):
            copy(r, k).start(priority=k)
        return c

    def wait(r, c):
        for k in range(TOP_K):
            copy(r, k).wait()
        return c

    lax.fori_loop(0, rows, start, 0, unroll=8)
    lax.fori_loop(0, rows, wait, 0, unroll=8)


def _dispatch(h2p, dest, n_slots):
    n, w = h2p.shape
    rows = min(ROW_TILE, n)
    dest3 = dest.reshape(n // rows, 1, TOP_K * rows)
    return pl.pallas_call(
        _dispatch_kernel,
        grid=(n // rows,),
        in_specs=[pl.BlockSpec((1, 1, TOP_K * rows), lambda i: (i, 0, 0), memory_space=pltpu.SMEM),
                  pl.BlockSpec((rows, w), lambda i: (i, 0)),
                  pl.BlockSpec(memory_space=pl.ANY)],
        out_specs=pl.BlockSpec(memory_space=pl.ANY),
        out_shape=jax.ShapeDtypeStruct((n_slots, w), jnp.uint32),
        scratch_shapes=[pltpu.SemaphoreType.DMA],
        input_output_aliases={2: 0},
        compiler_params=_params(("arbitrary",)),
        name="dispatch",
    )(dest3, h2p, jnp.zeros((n_slots, w), jnp.uint32))


def _experts_kernel(be_ref, bv_ref, xs_ref, wg_ref, wu_ref, wd_ref, o_ref, wg_s, wu_s, wd_s):
    i = pl.program_id(0)
    valid = bv_ref[i] != 0

    @pl.when(jnp.logical_and(valid, jnp.logical_or(i == 0, be_ref[i] != be_ref[jnp.maximum(i - 1, 0)])))
    def _():
        wg_s[...] = wg_ref[0, 0].astype(BF16)
        wu_s[...] = wu_ref[0, 0].astype(BF16)
        wd_s[...] = wd_ref[0, 0].astype(BF16)

    @pl.when(valid)
    def _():
        x = _unpack_rows(xs_ref[...]).astype(BF16)
        g = _dot(x, wg_s[...])
        u = _dot(x, wu_s[...])
        act = ((g * jax.nn.sigmoid(g)) * u).astype(BF16)
        o_ref[...] = _dot(act, wd_s[...]).astype(BF16)

    @pl.when(jnp.logical_not(valid))
    def _():
        o_ref[...] = jnp.zeros_like(o_ref)


def _experts(layer, blk_expert, blk_valid, xs, w_gate, w_up, w_down):
    n_slots = xs.shape[0]
    n_blocks = n_slots // EXPERT_TILE
    grid_spec = pltpu.PrefetchScalarGridSpec(
        num_scalar_prefetch=2,
        grid=(n_blocks,),
        in_specs=[pl.BlockSpec((EXPERT_TILE, D_MODEL // 2), lambda i, be, bv: (i, 0)),
                  pl.BlockSpec((1, 1, D_MODEL, D_EXPERT), lambda i, be, bv: (layer, be[i], 0, 0)),
                  pl.BlockSpec((1, 1, D_MODEL, D_EXPERT), lambda i, be, bv: (layer, be[i], 0, 0)),
                  pl.BlockSpec((1, 1, D_EXPERT, D_MODEL), lambda i, be, bv: (layer, be[i], 0, 0))],
        out_specs=pl.BlockSpec((EXPERT_TILE, D_MODEL), lambda i, be, bv: (i, 0)),
        scratch_shapes=[pltpu.VMEM((D_MODEL, D_EXPERT), BF16), pltpu.VMEM((D_MODEL, D_EXPERT), BF16),
                        pltpu.VMEM((D_EXPERT, D_MODEL), BF16)],
    )
    return pl.pallas_call(
        _experts_kernel,
        grid_spec=grid_spec,
        out_shape=jax.ShapeDtypeStruct((n_slots, D_MODEL), BF16),
        compiler_params=_params(("arbitrary",)),
        name="experts",
    )(blk_expert, blk_valid, xs, w_gate, w_up, w_down)


def _combine_kernel(x_ref, y0_ref, y1_ref, rt_ref, g_ref, o_ref, *, final):
    g0 = rt_ref[:, TOP_K:TOP_K + 1]
    g1 = rt_ref[:, TOP_K + 1:TOP_K + 2]
    x = x_ref[...] + (y0_ref[...].astype(F32) * g0 + y1_ref[...].astype(F32) * g1)
    o_ref[...] = _rms(x, g_ref[...]) if final else x


def _combine(x1, y0, y1, route, g, final):
    n = x1.shape[0]
    rows = min(ROW_TILE, n)
    row = pl.BlockSpec((rows, D_MODEL), lambda i: (i, 0))
    return pl.pallas_call(
        functools.partial(_combine_kernel, final=final),
        grid=(n // rows,),
        in_specs=[row, row, row, pl.BlockSpec((rows, LANES), lambda i: (i, 0)),
                  pl.BlockSpec((1, D_MODEL), lambda i: (0, 0))],
        out_specs=row,
        out_shape=jax.ShapeDtypeStruct((n, D_MODEL), F32),
        compiler_params=_params(("parallel",)),
        name="combine",
    )(x1, y0, y1, route, g)


def _swap_halves(w):
    half = w.shape[-1] // 2
    return jnp.concatenate([w[..., half:], w[..., :half]], axis=-1)


def _prep_in_weights(w_in):
    d = w_in.shape[0]
    q_lat, kv_lat, kr = w_in[:, 0:256], w_in[:, 256:384], w_in[:, 384:416]
    fq, fk, fv, fl = w_in[:, 416:928], w_in[:, 928:1440], w_in[:, 1440:1952], w_in[:, 1952:1960]
    slab_a = jnp.concatenate([kr, fl, jnp.zeros((d, LANES - MLA_ROPE - FOX_HEADS), F32)], axis=1)
    slab_b = jnp.concatenate([_swap_halves(kr), jnp.zeros((d, LANES - MLA_ROPE), F32)], axis=1)
    return jnp.concatenate([fq * (FOX_DIM ** -0.5), fk, fv, q_lat, kv_lat, slab_a, slab_b], axis=1).astype(BF16)


def _prep_uq(w_uq):
    w = w_uq.reshape(MLA_Q_RANK, MLA_HEADS, MLA_NOPE + MLA_ROPE)
    nope, pe = w[:, :, :MLA_NOPE], w[:, :, MLA_NOPE:]
    pairs = nope.reshape(MLA_Q_RANK, MLA_HEADS // 2, 2, MLA_NOPE)[:, :, ::-1, :].reshape(MLA_Q_RANK, -1)
    return jnp.concatenate([pairs, pe.reshape(MLA_Q_RANK, -1), _swap_halves(pe).reshape(MLA_Q_RANK, -1)],
                           axis=1).astype(BF16)


def _prep_ukv(w_ukv):
    w = w_ukv.reshape(MLA_KV_RANK, MLA_HEADS // 2, 2, 2, MLA_NOPE)
    even = w[:, :, 0, ::-1, :]
    odd = w[:, :, 1, :, :]
    return jnp.stack([even, odd], axis=2).reshape(MLA_KV_RANK, -1).astype(BF16)


def _prep_router(w_rg, b_rg, w_re, b_re):
    d = w_rg.shape[0]
    w = jnp.concatenate([w_rg, w_re, jnp.zeros((d, LANES - N_GROUPS - N_EXPERTS), F32)], axis=1)
    w_hi = w.astype(BF16)
    w_lo = (w - w_hi.astype(F32)).astype(BF16)
    b = jnp.concatenate([b_rg, b_re, jnp.zeros((LANES - N_GROUPS - N_EXPERTS,), F32)])[None, :]
    return jnp.concatenate([w_hi, w_lo], axis=1), b


def _rope_slabs(positions):
    half = MLA_ROPE // 2
    inv_freq = ROPE_THETA ** (-jnp.arange(half, dtype=F32) / half)
    ang = positions.astype(F32).reshape(-1)[:, None] * inv_freq
    cos, sin = jnp.cos(ang), jnp.sin(ang)
    reps = LANES // MLA_ROPE
    return jnp.tile(jnp.concatenate([cos, cos], axis=1), (1, reps)), jnp.tile(
        jnp.concatenate([-sin, sin], axis=1), (1, reps))


def _slot_layout(route, counts, n_blocks):
    eid = route[:, 0:TOP_K].astype(jnp.int32)
    rank = route[:, 2 * TOP_K:3 * TOP_K].astype(jnp.int32)
    counts = counts[0, N_GROUPS:N_GROUPS + N_EXPERTS].astype(jnp.int32)
    padded = (counts + EXPERT_TILE - 1) // EXPERT_TILE * EXPERT_TILE
    pad_end = jnp.cumsum(padded)
    pad_start = pad_end - padded
    experts = jnp.arange(N_EXPERTS, dtype=jnp.int32)
    dest = jnp.sum(jnp.where(eid[:, :, None] == experts, pad_start, 0), axis=-1) + rank
    blk_start = jnp.arange(n_blocks, dtype=jnp.int32) * EXPERT_TILE
    blk_expert = jnp.minimum(jnp.sum((blk_start[:, None] >= pad_end[None, :]).astype(jnp.int32), axis=1),
                             N_EXPERTS - 1)
    blk_valid = (blk_start < pad_end[-1]).astype(jnp.int32)
    return dest, blk_expert, blk_valid


def kernel(x, positions, attn_norm, w_in, b_forget, q_norm, w_uq, kv_norm, w_ukv, mla_out_norm, fox_out_norm,
           w_out, ffn_norm, w_router_group, b_router_group, w_router_expert, b_router_expert, w_gate, w_up,
           w_down, final_norm):
    batch, seq, d = x.shape
    n = batch * seq
    depth = w_in.shape[0]
    n_blocks = -(-(n * TOP_K) // EXPERT_TILE) + N_EXPERTS
    cc, ss = _rope_slabs(positions)
    tri = (jnp.arange(CUM_TILE)[:, None] <= jnp.arange(CUM_TILE)[None, :]).astype(BF16)
    sub = min(ROW_TILE, n) // SUB_TILES
    tri_rows = (jnp.arange(sub)[None, :] < jnp.arange(sub)[:, None]).astype(BF16)
    xf = x.reshape(n, d)
    for l in range(depth):
        fq, fk, fv, qn, qpe, kv, kpe, flt = _in_proj(
            xf, attn_norm[l][None, :], _prep_in_weights(w_in[l]), q_norm[l][None, :], _prep_uq(w_uq[l]),
            kv_norm[l][None, :], _prep_ukv(w_ukv[l]), cc, ss)
        ck = _fox_decay(flt, b_forget[l][:, None], tri, batch, seq)
        o_mla = _mla_attn(qn, qpe, kv, kpe, batch, seq).reshape(n, HEAD_W)
        o_fox = _fox_attn(fq, fk, fv, ck, batch, seq).reshape(n, HEAD_W)
        wr, br = _prep_router(w_router_group[l], b_router_group[l], w_router_expert[l], b_router_expert[l])
        x1, h2, route, counts = _out_proj(o_mla, o_fox, xf, mla_out_norm[l][None, :], fox_out_norm[l][None, :],
                                          w_out[l].astype(BF16), ffn_norm[l][None, :], wr, br, tri_rows)
        dest, blk_expert, blk_valid = _slot_layout(route, counts, n_blocks)
        xs = _dispatch(h2, dest, n_blocks * EXPERT_TILE)
        ys = _experts(l, blk_expert, blk_valid, xs, w_gate, w_up, w_down)
        y0 = ys.at[dest[:, 0]].get(mode="promise_in_bounds")
        y1 = ys.at[dest[:, 1]].get(mode="promise_in_bounds")
        final = l == depth - 1
        xf = _combine(x1, y0, y1, route, final_norm[None, :] if final else ffn_norm[l][None, :], final)
    return xf.reshape(batch, seq, d)
```

```python
import functools

import jax
import jax.numpy as jnp
from jax import lax
from jax.experimental import pallas as pl
from jax.experimental.pallas import tpu as pltpu

D_MODEL = 1024
CHUNK = 64
MLA_HEADS = 8
MLA_NOPE = 64
MLA_ROPE = 32
MLA_V = 64
MLA_Q_RANK = 256
MLA_KV_RANK = 128
ROPE_THETA = 10000.0
FOX_HEADS = 8
FOX_DIM = 64
HEAD_W = 512
N_GROUPS = 4
EXPERTS_PER_GROUP = 8
N_EXPERTS = 32
TOP_K = 2
D_EXPERT = 512
NORM_EPS = 1e-6

LANES = 128
IN_COLS = 2176
ROW_TILE = 512
SUB_TILES = 2
Q_TILE = 256
EXPERT_TILE = 256
CUM_TILE = 256
NEG_BIG = -1e30
LOG2E = 1.4426950408889634
VMEM_LIMIT = 48 * 1024 * 1024

F32 = jnp.float32
BF16 = jnp.bfloat16


def _rms(x, g):
    return (x * lax.rsqrt(jnp.mean(x * x, axis=-1, keepdims=True) + NORM_EPS)) * g


def _dot(a, b):
    return jnp.dot(a, b, preferred_element_type=F32)


def _dot_nt(a, b):
    return lax.dot_general(a, b, (((1,), (1,)), ((), ())), preferred_element_type=F32)


def _params(sem):
    return pltpu.CompilerParams(dimension_semantics=sem, vmem_limit_bytes=VMEM_LIMIT)


def _sub_slices(rows):
    sub = rows // SUB_TILES
    return [slice(t * sub, (t + 1) * sub) for t in range(SUB_TILES)]


def _pack_rows(v):
    w = v.shape[1] // 2
    lo = lax.bitcast_convert_type(v[:, :w], jnp.uint32) >> 16
    hi = lax.bitcast_convert_type(v[:, w:], jnp.uint32) & jnp.uint32(0xFFFF0000)
    return hi | lo


def _unpack_rows(p):
    lo = lax.bitcast_convert_type(p << 16, F32)
    hi = lax.bitcast_convert_type(p & jnp.uint32(0xFFFF0000), F32)
    return jnp.concatenate([lo, hi], axis=1)


def _in_proj_kernel(x_ref, g_ref, w_ref, gq_ref, wuq_ref, gkv_ref, wukv_ref, cc_ref, ss_ref,
                    fq_ref, fk_ref, fv_ref, qn_ref, qpe_ref, kv_ref, kpe_ref, flt_ref):
    subs = _sub_slices(x_ref.shape[0])
    ps = [_dot(_rms(x_ref[sl, :], g_ref[...]).astype(BF16), w_ref[...]) for sl in subs]
    scale = (MLA_NOPE + MLA_ROPE) ** -0.5 * LOG2E
    for sl, p in zip(subs, ps):
        fq_ref[sl, :] = (p[:, 0:512] * LOG2E).astype(BF16)
        fk_ref[sl, :] = p[:, 512:1024].astype(BF16)
        fv_ref[sl, :] = p[:, 1024:1536].astype(BF16)

        cc = cc_ref[sl, :]
        ss = ss_ref[sl, :]
        q = _dot(_rms(p[:, 1536:1792], gq_ref[...]).astype(BF16), wuq_ref[...])
        qn_ref[sl, :] = (q[:, 0:512] * scale).astype(BF16)
        cc2 = jnp.concatenate([cc, cc], axis=1)
        ss2 = jnp.concatenate([ss, ss], axis=1)
        qpe_ref[sl, :] = ((q[:, 512:768] * cc2 + q[:, 768:1024] * ss2) * scale).astype(BF16)

        kv_ref[sl, :] = _dot(_rms(p[:, 1792:1920], gkv_ref[...]).astype(BF16), wukv_ref[...]).astype(BF16)

        slab_a = p[:, 1920:2048]
        slab_b = p[:, 2048:2176]
        lane = lax.broadcasted_iota(jnp.int32, slab_a.shape, 1)
        roped = jnp.where(lane < MLA_ROPE, slab_a * cc + slab_b * ss, 0.0)
        tiled = roped + pltpu.roll(roped, 32, 1) + pltpu.roll(roped, 64, 1) + pltpu.roll(roped, 96, 1)
        kpe_ref[sl, :] = tiled.astype(BF16)
        flt_ref[:, sl] = slab_a.T[MLA_ROPE:MLA_ROPE + FOX_HEADS, :]


def _in_proj(x2d, g, w, gq, wuq, gkv, wukv, cc, ss):
    n = x2d.shape[0]
    rows = min(ROW_TILE, n)
    row = lambda c: pl.BlockSpec((rows, c), lambda i: (i, 0))
    full = lambda a: pl.BlockSpec(a.shape, lambda i: (0,) * a.ndim)
    out_shape = (
        jax.ShapeDtypeStruct((n, HEAD_W), BF16), jax.ShapeDtypeStruct((n, HEAD_W), BF16),
        jax.ShapeDtypeStruct((n, HEAD_W), BF16), jax.ShapeDtypeStruct((n, HEAD_W), BF16),
        jax.ShapeDtypeStruct((n, 256), BF16), jax.ShapeDtypeStruct((n, 1024), BF16),
        jax.ShapeDtypeStruct((n, LANES), BF16), jax.ShapeDtypeStruct((FOX_HEADS, n), F32),
    )
    return pl.pallas_call(
        _in_proj_kernel,
        grid=(n // rows,),
        in_specs=[row(D_MODEL), full(g), full(w), full(gq), full(wuq), full(gkv), full(wukv),
                  row(LANES), row(LANES)],
        out_specs=(row(HEAD_W), row(HEAD_W), row(HEAD_W), row(HEAD_W), row(256), row(1024), row(LANES),
                   pl.BlockSpec((FOX_HEADS, rows), lambda i: (0, i))),
        out_shape=out_shape,
        compiler_params=_params(("parallel",)),
        name="in_proj",
    )(x2d, g, w, gq, wuq, gkv, wukv, cc, ss)


def _fox_decay_kernel(fl_ref, b_ref, tri_ref, ck_ref):
    z = fl_ref[...] + b_ref[...]
    lf = jnp.minimum(z, 0.0) - jnp.log1p(jnp.exp(-jnp.abs(z)))
    seq = lf.shape[1]
    tri = tri_ref[...]
    carry = jnp.zeros((FOX_HEADS, 1), F32)
    zeros = jnp.zeros((FOX_HEADS, CUM_TILE), F32)
    for j in range(seq // CUM_TILE):
        v = lf[:, j * CUM_TILE:(j + 1) * CUM_TILE]
        hi = v.astype(BF16).astype(F32)
        r1 = v - hi
        mid = r1.astype(BF16).astype(F32)
        lo = r1 - mid
        parts = _dot(jnp.concatenate([hi, mid, lo, zeros], axis=0).astype(BF16), tri)
        cs = (parts[0:8] + parts[8:16]) + parts[16:24] + carry
        carry = cs[:, CUM_TILE - 1:CUM_TILE]
        d = cs * (-LOG2E)
        d_hi = d.astype(BF16).astype(F32)
        d_r = d - d_hi
        d_mid = d_r.astype(BF16).astype(F32)
        d_lo = d_r - d_mid
        rows = jnp.concatenate([d_hi, d_mid, d_lo, jnp.zeros((LANES - 3 * FOX_HEADS, CUM_TILE), F32)], axis=0)
        ck_ref[0, j * CUM_TILE:(j + 1) * CUM_TILE, :] = rows.T.astype(BF16)


def _fox_decay(flt, b_col, tri, batch, seq):
    return pl.pallas_call(
        _fox_decay_kernel,
        grid=(batch,),
        in_specs=[pl.BlockSpec((FOX_HEADS, seq), lambda b: (0, b)),
                  pl.BlockSpec((FOX_HEADS, 1), lambda b: (0, 0)),
                  pl.BlockSpec((CUM_TILE, CUM_TILE), lambda b: (0, 0))],
        out_specs=pl.BlockSpec((1, seq, LANES), lambda b: (b, 0, 0)),
        out_shape=jax.ShapeDtypeStruct((batch, seq, LANES), BF16),
        compiler_params=_params(("parallel",)),
        name="fox_decay",
    )(flt, b_col, tri)


def _probs(s_off, s_diag):
    m = jnp.max(s_diag, axis=-1, keepdims=True)
    if s_off is not None:
        m = jnp.maximum(m, jnp.max(s_off, axis=-1, keepdims=True))
    p_off = None if s_off is None else jnp.exp2(s_off - m).astype(BF16)
    return p_off, jnp.exp2(s_diag - m).astype(BF16)


def _attend(q_ref, k_ref, v_ref, o_ref, allowed):
    seq = q_ref.shape[1]
    out_lane = lax.broadcasted_iota(jnp.int32, (Q_TILE, LANES), 1)
    units = [(i, hh) for i in range(seq // Q_TILE) for hh in range(2)]

    def scores(i, hh):
        qs, qe = i * Q_TILE, (i + 1) * Q_TILE
        q = q_ref[hh, qs:qe, :]
        s_diag = jnp.where(allowed, _dot_nt(q, k_ref[hh, qs:qe, :]), NEG_BIG)
        s_off = _dot_nt(q, k_ref[hh, 0:qs, :]) if i > 0 else None
        return s_off, s_diag

    def values(i, hh, p_off, p_diag):
        qs, qe = i * Q_TILE, (i + 1) * Q_TILE
        acc = _dot(p_diag, v_ref[hh, qs:qe, :])
        if p_off is not None:
            acc = acc + _dot(p_off, v_ref[hh, 0:qs, :])
        return acc[:, 0:LANES] / acc[:, LANES:2 * LANES]

    n_units = len(units)
    s = {0: scores(*units[0])}
    if n_units > 1:
        s[1] = scores(*units[1])
    p = {0: _probs(*s.pop(0))}
    outs = []
    for n, (i, hh) in enumerate(units):
        if n + 2 < n_units:
            s[n + 2] = scores(*units[n + 2])
        if n + 1 < n_units:
            p[n + 1] = _probs(*s.pop(n + 1))
        outs.append(values(i, hh, *p.pop(n)))
        if hh == 1:
            o_ref[0, i * Q_TILE:(i + 1) * Q_TILE, :] = jnp.where(out_lane < 64, outs[0], outs[1]).astype(BF16)
            outs = []


def _tile_iota():
    row = lax.broadcasted_iota(jnp.int32, (Q_TILE, Q_TILE), 0)
    col = lax.broadcasted_iota(jnp.int32, (Q_TILE, Q_TILE), 1)
    return row, col


def _mla_attn_kernel(qn_ref, qpe_ref, kv_ref, kpe_ref, o_ref, qs_ref, ks_ref, vs_ref):
    j = pl.program_id(1)
    seq = qn_ref.shape[1]
    lane = lax.broadcasted_iota(jnp.int32, (seq, LANES), 1)
    qn = qn_ref[0]
    qpe = qpe_ref[0]
    kpe = kpe_ref[0]
    ones = jnp.ones((seq, LANES), BF16)
    for hh in range(2):
        nope_mask = (lane >= 64) if hh == 0 else (lane < 64)
        pe_slot = 2 * (j % 2) + hh
        pe_mask = (lane // MLA_ROPE) == pe_slot
        qs_ref[hh, :, 0:LANES] = jnp.where(nope_mask, qn, jnp.zeros_like(qn))
        qs_ref[hh, :, LANES:2 * LANES] = jnp.where(pe_mask, qpe, jnp.zeros_like(qpe))
        kvh = kv_ref[0, :, hh * LANES:(hh + 1) * LANES]
        ks_ref[hh, :, 0:LANES] = kvh
        ks_ref[hh, :, LANES:2 * LANES] = kpe
        vs_ref[hh, :, 0:LANES] = kvh
        vs_ref[hh, :, LANES:2 * LANES] = ones
    row, col = _tile_iota()
    _attend(qs_ref, ks_ref, vs_ref, o_ref, (col // CHUNK) <= (row // CHUNK))


def _mla_attn(qn, qpe, kv, kpe, batch, seq):
    qn, qpe, kv, kpe = (a.reshape(batch, seq, a.shape[-1]) for a in (qn, qpe, kv, kpe))
    pair = pltpu.VMEM((2, seq, 2 * LANES), BF16)
    return pl.pallas_call(
        _mla_attn_kernel,
        grid=(batch, MLA_HEADS // 2),
        in_specs=[pl.BlockSpec((1, seq, LANES), lambda b, j: (b, 0, j)),
                  pl.BlockSpec((1, seq, LANES), lambda b, j: (b, 0, j // 2)),
                  pl.BlockSpec((1, seq, 2 * LANES), lambda b, j: (b, 0, j)),
                  pl.BlockSpec((1, seq, LANES), lambda b, j: (b, 0, 0))],
        out_specs=pl.BlockSpec((1, seq, LANES), lambda b, j: (b, 0, j)),
        out_shape=jax.ShapeDtypeStruct((batch, seq, HEAD_W), BF16),
        scratch_shapes=[pair, pair, pair],
        compiler_params=_params(("parallel", "parallel")),
        name="mla_attn",
    )(qn, qpe, kv, kpe)


def _fox_attn_kernel(q_ref, k_ref, v_ref, ck_ref, o_ref, qs_ref, ks_ref, vs_ref):
    j = pl.program_id(1)
    seq = q_ref.shape[1]
    lane = lax.broadcasted_iota(jnp.int32, (seq, LANES), 1)
    q = q_ref[0]
    k = k_ref[0]
    v = v_ref[0]
    ck = ck_ref[0]
    ones = jnp.ones((seq, LANES), BF16)
    for hh in range(2):
        head = 2 * j + hh
        head_mask = (lane < 64) if hh == 0 else (lane >= 64)
        piece = jnp.where(lane < 3 * FOX_HEADS, lane % FOX_HEADS, -1) == head
        qs_ref[hh, :, 0:LANES] = jnp.where(head_mask, q, jnp.zeros_like(q))
        qs_ref[hh, :, LANES:2 * LANES] = jnp.where(piece, 1.0, 0.0).astype(BF16)
        ks_ref[hh, :, 0:LANES] = k
        ks_ref[hh, :, LANES:2 * LANES] = ck
        vs_ref[hh, :, 0:LANES] = v
        vs_ref[hh, :, LANES:2 * LANES] = ones
    row, col = _tile_iota()
    _attend(qs_ref, ks_ref, vs_ref, o_ref, col <= row)


def _fox_attn(fq, fk, fv, ck, batch, seq):
    fq, fk, fv = (a.reshape(batch, seq, HEAD_W) for a in (fq, fk, fv))
    spec = pl.BlockSpec((1, seq, LANES), lambda b, j: (b, 0, j))
    pair = pltpu.VMEM((2, seq, 2 * LANES), BF16)
    return pl.pallas_call(
        _fox_attn_kernel,
        grid=(batch, FOX_HEADS // 2),
        in_specs=[spec, spec, spec, pl.BlockSpec((1, seq, LANES), lambda b, j: (b, 0, 0))],
        out_specs=spec,
        out_shape=jax.ShapeDtypeStruct((batch, seq, HEAD_W), BF16),
        scratch_shapes=[pair, pair, pair],
        compiler_params=_params(("parallel", "parallel")),
        name="fox_attn",
    )(fq, fk, fv, ck)


def _lane_max(v):
    return jnp.max(v, axis=1, keepdims=True)


def _first_lane(hit, lane_f):
    return jnp.min(jnp.where(hit, lane_f, float(LANES)), axis=1, keepdims=True)


def _out_proj_kernel(om_ref, of_ref, x_ref, gm_ref, gf_ref, wo_ref, gn_ref, wr_ref, br_ref, tri_ref,
                     x1_ref, h2_ref, rt_ref, cnt_ref, carry_ref):
    @pl.when(pl.program_id(0) == 0)
    def _():
        carry_ref[...] = jnp.zeros_like(carry_ref)

    subs = _sub_slices(x_ref.shape[0])
    logits = [_project_out(sl, om_ref, of_ref, x_ref, gm_ref, gf_ref, wo_ref, gn_ref, wr_ref, br_ref, x1_ref, h2_ref)
              for sl in subs]
    carry = carry_ref[...]
    for sl, lg in zip(subs, logits):
        rt_ref[sl, :], carry = _route_rows(lg, tri_ref[...], carry)
    carry_ref[...] = carry
    cnt_ref[...] = carry


def _project_out(sl, om_ref, of_ref, x_ref, gm_ref, gf_ref, wo_ref, gn_ref, wr_ref, br_ref, x1_ref, h2_ref):
    a = _rms(om_ref[sl, :].astype(F32), gm_ref[...]).astype(BF16)
    b = _rms(of_ref[sl, :].astype(F32), gf_ref[...]).astype(BF16)
    x1 = x_ref[sl, :] + _dot(a, wo_ref[0:HEAD_W, :]) + _dot(b, wo_ref[HEAD_W:2 * HEAD_W, :])
    x1_ref[sl, :] = x1
    h2 = _rms(x1, gn_ref[...])
    h_hi = h2.astype(BF16)
    h2_ref[sl, :] = _pack_rows(h_hi.astype(F32))
    h_lo = (h2 - h_hi.astype(F32)).astype(BF16)
    t = _dot(h_hi, wr_ref[...])
    return (t[:, 0:LANES] + t[:, LANES:2 * LANES]) + _dot(h_lo, wr_ref[:, 0:LANES]) + br_ref[...]


def _route_rows(lg, tri, carry):
    lane = lax.broadcasted_iota(jnp.int32, lg.shape, 1)
    lane_f = lane.astype(F32)
    neg_inf = float("-inf")
    is_group = lane < N_GROUPS
    gl = jnp.where(is_group, lg, neg_inf)
    mg = _lane_max(gl)
    gi = _first_lane(gl == mg, lane_f)
    g_val = 1.0 / jnp.sum(jnp.where(is_group, jnp.exp(lg - mg), 0.0), axis=1, keepdims=True)
    group_of_lane = ((lane - N_GROUPS) >> 3).astype(F32)
    is_expert = (lane >= N_GROUPS) & (lane < N_GROUPS + N_EXPERTS) & (group_of_lane == gi)
    el = jnp.where(is_expert, lg, neg_inf)
    m1 = _lane_max(el)
    i1 = _first_lane(el == m1, lane_f)
    el2 = jnp.where(lane_f == i1, neg_inf, el)
    m2 = _lane_max(el2)
    i2 = _first_lane(el2 == m2, lane_f)
    r = jnp.exp(m2 - m1)
    g0 = g_val / (1.0 + r)
    g1 = g0 * r

    hit1 = lane_f == i1
    hit2 = lane_f == i2
    onehot = jnp.where(hit1 | hit2, 1.0, 0.0)
    before = _dot(tri, onehot.astype(BF16)) + carry
    r0 = jnp.sum(jnp.where(hit1, before, 0.0), axis=1, keepdims=True)
    r1 = jnp.sum(jnp.where(hit2, before, 0.0), axis=1, keepdims=True)

    vals = (i1 - N_GROUPS, i2 - N_GROUPS, g0, g1, r0, r1)
    rt = jnp.zeros_like(lg)
    for k, v in enumerate(vals):
        rt = jnp.where(lane == k, v, rt)
    return rt, carry + jnp.sum(onehot, axis=0, keepdims=True)


def _out_proj(om, of, x2d, gm, gf, wo, gn, wr, br, tri):
    n = x2d.shape[0]
    rows = tri.shape[0] * SUB_TILES
    row = lambda c: pl.BlockSpec((rows, c), lambda i: (i, 0))
    full = lambda a: pl.BlockSpec(a.shape, lambda i: (0,) * a.ndim)
    return pl.pallas_call(
        _out_proj_kernel,
        grid=(n // rows,),
        in_specs=[row(HEAD_W), row(HEAD_W), row(D_MODEL), full(gm), full(gf), full(wo), full(gn), full(wr),
                  full(br), full(tri)],
        out_specs=(row(D_MODEL), row(D_MODEL // 2), row(LANES), pl.BlockSpec((1, LANES), lambda i: (0, 0))),
        out_shape=(jax.ShapeDtypeStruct((n, D_MODEL), F32), jax.ShapeDtypeStruct((n, D_MODEL // 2), jnp.uint32),
                   jax.ShapeDtypeStruct((n, LANES), F32), jax.ShapeDtypeStruct((1, LANES), F32)),
        scratch_shapes=[pltpu.VMEM((1, LANES), F32)],
        compiler_params=_params(("arbitrary",)),
        name="out_proj",
    )(om, of, x2d, gm, gf, wo, gn, wr, br, tri)


def _experts_kernel(be_ref, tok_ref, tok_next_ref, h2_ref, wg_ref, wu_ref, wd_ref, o_ref,
                    x_buf, sems, x_s, wg_s, wu_s, wd_s):
    i = pl.program_id(0)
    last = pl.num_programs(0) - 1
    cur = i % 2

    def start_gather(tok, buf):
        for r in range(EXPERT_TILE):
            pltpu.make_async_copy(h2_ref.at[pl.ds(tok[0, 0, r], 1)], x_buf.at[buf, pl.ds(r, 1)], sems.at[buf]).start()

    def wait_gather(buf):
        pltpu.make_async_copy(h2_ref.at[pl.ds(0, EXPERT_TILE)], x_buf.at[buf], sems.at[buf]).wait()

    @pl.when(i == 0)
    def _():
        start_gather(tok_ref, 0)

    @pl.when(jnp.logical_or(i == 0, be_ref[i] != be_ref[jnp.maximum(i - 1, 0)]))
    def _():
        wg_s[...] = wg_ref[0, 0].astype(BF16)
        wu_s[...] = wu_ref[0, 0].astype(BF16)
        wd_s[...] = wd_ref[0, 0].astype(BF16)

    wait_gather(cur)
    x_s[...] = _unpack_rows(x_buf[cur]).astype(BF16)
    start_gather(tok_next_ref, 1 - cur)
    x = x_s[...]
    g = _dot(x, wg_s[...])
    u = _dot(x, wu_s[...])
    act = ((g * jax.nn.sigmoid(g)) * u).astype(BF16)
    o_ref[...] = _dot(act, wd_s[...]).astype(BF16)

    @pl.when(i == last)
    def _():
        wait_gather(1 - cur)


def _experts(layer, blk_expert, slot_tok, h2p, w_gate, w_up, w_down):
    n_blocks = slot_tok.shape[0]
    tok_spec = lambda index: pl.BlockSpec((1, 1, EXPERT_TILE), index, memory_space=pltpu.SMEM)
    weight = lambda rows, cols: pl.BlockSpec((1, 1, rows, cols), lambda i, be: (layer, be[i], 0, 0))
    grid_spec = pltpu.PrefetchScalarGridSpec(
        num_scalar_prefetch=1,
        grid=(n_blocks,),
        in_specs=[tok_spec(lambda i, be: (i, 0, 0)),
                  tok_spec(lambda i, be: (jnp.minimum(i + 1, n_blocks - 1), 0, 0)),
                  pl.BlockSpec(memory_space=pl.ANY),
                  weight(D_MODEL, D_EXPERT), weight(D_MODEL, D_EXPERT), weight(D_EXPERT, D_MODEL)],
        out_specs=pl.BlockSpec((EXPERT_TILE, D_MODEL), lambda i, be: (i, 0)),
        scratch_shapes=[pltpu.VMEM((2, EXPERT_TILE, D_MODEL // 2), jnp.uint32), pltpu.SemaphoreType.DMA((2,)),
                        pltpu.VMEM((EXPERT_TILE, D_MODEL), BF16),
                        pltpu.VMEM((D_MODEL, D_EXPERT), BF16), pltpu.VMEM((D_MODEL, D_EXPERT), BF16),
                        pltpu.VMEM((D_EXPERT, D_MODEL), BF16)],
    )
    return pl.pallas_call(
        _experts_kernel,
        grid_spec=grid_spec,
        out_shape=jax.ShapeDtypeStruct((n_blocks * EXPERT_TILE, D_MODEL), BF16),
        compiler_params=_params(("arbitrary",)),
        name="experts",
    )(blk_expert, slot_tok, slot_tok, h2p, w_gate, w_up, w_down)


def _combine_kernel(x_ref, y0_ref, y1_ref, rt_ref, g_ref, o_ref, *, final):
    g0 = rt_ref[:, TOP_K:TOP_K + 1]
    g1 = rt_ref[:, TOP_K + 1:TOP_K + 2]
    x = x_ref[...] + (y0_ref[...].astype(F32) * g0 + y1_ref[...].astype(F32) * g1)
    o_ref[...] = _rms(x, g_ref[...]) if final else x


def _combine(x1, y0, y1, route, g, final):
    n = x1.shape[0]
    rows = min(ROW_TILE, n)
    row = pl.BlockSpec((rows, D_MODEL), lambda i: (i, 0))
    return pl.pallas_call(
        functools.partial(_combine_kernel, final=final),
        grid=(n // rows,),
        in_specs=[row, row, row, pl.BlockSpec((rows, LANES), lambda i: (i, 0)),
                  pl.BlockSpec((1, D_MODEL), lambda i: (0, 0))],
        out_specs=row,
        out_shape=jax.ShapeDtypeStruct((n, D_MODEL), F32),
        compiler_params=_params(("parallel",)),
        name="combine",
    )(x1, y0, y1, route, g)


def _swap_halves(w):
    half = w.shape[-1] // 2
    return jnp.concatenate([w[..., half:], w[..., :half]], axis=-1)


def _prep_in_weights(w_in):
    d = w_in.shape[0]
    q_lat, kv_lat, kr = w_in[:, 0:256], w_in[:, 256:384], w_in[:, 384:416]
    fq, fk, fv, fl = w_in[:, 416:928], w_in[:, 928:1440], w_in[:, 1440:1952], w_in[:, 1952:1960]
    slab_a = jnp.concatenate([kr, fl, jnp.zeros((d, LANES - MLA_ROPE - FOX_HEADS), F32)], axis=1)
    slab_b = jnp.concatenate([_swap_halves(kr), jnp.zeros((d, LANES - MLA_ROPE), F32)], axis=1)
    return jnp.concatenate([fq * (FOX_DIM ** -0.5), fk, fv, q_lat, kv_lat, slab_a, slab_b], axis=1).astype(BF16)


def _prep_uq(w_uq):
    w = w_uq.reshape(MLA_Q_RANK, MLA_HEADS, MLA_NOPE + MLA_ROPE)
    nope, pe = w[:, :, :MLA_NOPE], w[:, :, MLA_NOPE:]
    pairs = nope.reshape(MLA_Q_RANK, MLA_HEADS // 2, 2, MLA_NOPE)[:, :, ::-1, :].reshape(MLA_Q_RANK, -1)
    return jnp.concatenate([pairs, pe.reshape(MLA_Q_RANK, -1), _swap_halves(pe).reshape(MLA_Q_RANK, -1)],
                           axis=1).astype(BF16)


def _prep_ukv(w_ukv):
    w = w_ukv.reshape(MLA_KV_RANK, MLA_HEADS // 2, 2, 2, MLA_NOPE)
    even = w[:, :, 0, ::-1, :]
    odd = w[:, :, 1, :, :]
    return jnp.stack([even, odd], axis=2).reshape(MLA_KV_RANK, -1).astype(BF16)


def _prep_router(w_rg, b_rg, w_re, b_re):
    d = w_rg.shape[0]
    w = jnp.concatenate([w_rg, w_re, jnp.zeros((d, LANES - N_GROUPS - N_EXPERTS), F32)], axis=1)
    w_hi = w.astype(BF16)
    w_lo = (w - w_hi.astype(F32)).astype(BF16)
    b = jnp.concatenate([b_rg, b_re, jnp.zeros((LANES - N_GROUPS - N_EXPERTS,), F32)])[None, :]
    return jnp.concatenate([w_hi, w_lo], axis=1), b


def _rope_slabs(positions):
    half = MLA_ROPE // 2
    inv_freq = ROPE_THETA ** (-jnp.arange(half, dtype=F32) / half)
    ang = positions.astype(F32).reshape(-1)[:, None] * inv_freq
    cos, sin = jnp.cos(ang), jnp.sin(ang)
    reps = LANES // MLA_ROPE
    return jnp.tile(jnp.concatenate([cos, cos], axis=1), (1, reps)), jnp.tile(
        jnp.concatenate([-sin, sin], axis=1), (1, reps))


def _slot_layout(route, counts, n_blocks):
    eid = route[:, 0:TOP_K].astype(jnp.int32)
    rank = route[:, 2 * TOP_K:3 * TOP_K].astype(jnp.int32)
    counts = counts[0, N_GROUPS:N_GROUPS + N_EXPERTS].astype(jnp.int32)
    padded = (counts + EXPERT_TILE - 1) // EXPERT_TILE * EXPERT_TILE
    pad_end = jnp.cumsum(padded)
    pad_start = pad_end - padded
    experts = jnp.arange(N_EXPERTS, dtype=jnp.int32)
    dest = jnp.sum(jnp.where(eid[:, :, None] == experts, pad_start, 0), axis=-1) + rank
    blk_start = jnp.arange(n_blocks, dtype=jnp.int32) * EXPERT_TILE
    blk_expert = jnp.minimum(jnp.sum((blk_start[:, None] >= pad_end[None, :]).astype(jnp.int32), axis=1),
                             N_EXPERTS - 1)
    n_assign = dest.size
    tok_sorted = (jnp.argsort(dest.reshape(-1)) // TOP_K).astype(jnp.int32)
    compact_shift = jnp.cumsum(counts) - counts - pad_start
    blk_shift = jnp.sum(jnp.where(blk_expert[:, None] == experts, compact_shift, 0), axis=-1)
    pos = blk_start[:, None] + jnp.arange(EXPERT_TILE, dtype=jnp.int32)[None, :] + blk_shift[:, None]
    slot_tok = tok_sorted[jnp.clip(pos, 0, n_assign - 1)].reshape(n_blocks, 1, EXPERT_TILE)
    return dest, blk_expert, slot_tok


def kernel(x, positions, attn_norm, w_in, b_forget, q_norm, w_uq, kv_norm, w_ukv, mla_out_norm, fox_out_norm,
           w_out, ffn_norm, w_router_group, b_router_group, w_router_expert, b_router_expert, w_gate, w_up,
           w_down, final_norm):
    batch, seq, d = x.shape
    n = batch * seq
    depth = w_in.shape[0]
    n_blocks = -(-(n * TOP_K) // EXPERT_TILE) + N_EXPERTS
    cc, ss = _rope_slabs(positions)
    tri = (jnp.arange(CUM_TILE)[:, None] <= jnp.arange(CUM_TILE)[None, :]).astype(BF16)
    sub = min(ROW_TILE, n) // SUB_TILES
    tri_rows = (jnp.arange(sub)[None, :] < jnp.arange(sub)[:, None]).astype(BF16)
    xf = x.reshape(n, d)
    for l in range(depth):
        fq, fk, fv, qn, qpe, kv, kpe, flt = _in_proj(
            xf, attn_norm[l][None, :], _prep_in_weights(w_in[l]), q_norm[l][None, :], _prep_uq(w_uq[l]),
            kv_norm[l][None, :], _prep_ukv(w_ukv[l]), cc, ss)
        ck = _fox_decay(flt, b_forget[l][:, None], tri, batch, seq)
        o_mla = _mla_attn(qn, qpe, kv, kpe, batch, seq).reshape(n, HEAD_W)
        o_fox = _fox_attn(fq, fk, fv, ck, batch, seq).reshape(n, HEAD_W)
        wr, br = _prep_router(w_router_group[l], b_router_group[l], w_router_expert[l], b_router_expert[l])
        x1, h2, route, counts = _out_proj(o_mla, o_fox, xf, mla_out_norm[l][None, :], fox_out_norm[l][None, :],
                                          w_out[l].astype(BF16), ffn_norm[l][None, :], wr, br, tri_rows)
        dest, blk_expert, slot_tok = _slot_layout(route, counts, n_blocks)
        ys = _experts(l, blk_expert, slot_tok, h2, w_gate, w_up, w_down)
        y0 = ys.at[dest[:, 0]].get(mode="promise_in_bounds")
        y1 = ys.at[dest[:, 1]].get(mode="promise_in_bounds")
        final = l == depth - 1
        xf = _combine(x1, y0, y1, route, final_norm[None, :] if final else ffn_norm[l][None, :], final)
    return xf.reshape(batch, seq, d)
```

```python
import functools

import jax
import jax.numpy as jnp
from jax import lax
from jax.experimental import pallas as pl
from jax.experimental.pallas import tpu as pltpu
from jax.experimental.pallas import tpu_sc as plsc

D_MODEL = 1024
CHUNK = 64
MLA_HEADS = 8
MLA_NOPE = 64
MLA_ROPE = 32
MLA_V = 64
MLA_Q_RANK = 256
MLA_KV_RANK = 128
ROPE_THETA = 10000.0
FOX_HEADS = 8
FOX_DIM = 64
HEAD_W = 512
N_GROUPS = 4
EXPERTS_PER_GROUP = 8
N_EXPERTS = 32
TOP_K = 2
D_EXPERT = 512
NORM_EPS = 1e-6

LANES = 128
IN_COLS = 2176
ROW_TILE = 512
SUB_TILES = 2
Q_TILE = 256
EXPERT_TILE = 256
CUM_TILE = 256
NEG_BIG = -1e30
LOG2E = 1.4426950408889634
VMEM_LIMIT = 48 * 1024 * 1024
SC_CORES = 2
SC_SUBCORES = 16
GATHER_WINDOW = 64

F32 = jnp.float32
BF16 = jnp.bfloat16


def _rms(x, g):
    return (x * lax.rsqrt(jnp.mean(x * x, axis=-1, keepdims=True) + NORM_EPS)) * g


def _dot(a, b):
    return jnp.dot(a, b, preferred_element_type=F32)


def _dot_nt(a, b):
    return lax.dot_general(a, b, (((1,), (1,)), ((), ())), preferred_element_type=F32)


def _params(sem):
    return pltpu.CompilerParams(dimension_semantics=sem, vmem_limit_bytes=VMEM_LIMIT)


def _sub_slices(rows):
    sub = rows // SUB_TILES
    return [slice(t * sub, (t + 1) * sub) for t in range(SUB_TILES)]


def _pack_rows(v):
    w = v.shape[1] // 2
    lo = lax.bitcast_convert_type(v[:, :w], jnp.uint32) >> 16
    hi = lax.bitcast_convert_type(v[:, w:], jnp.uint32) & jnp.uint32(0xFFFF0000)
    return hi | lo


def _unpack_rows(p):
    lo = lax.bitcast_convert_type(p << 16, F32)
    hi = lax.bitcast_convert_type(p & jnp.uint32(0xFFFF0000), F32)
    return jnp.concatenate([lo, hi], axis=1)


def _in_proj_kernel(x_ref, g_ref, w_ref, gq_ref, wuq_ref, gkv_ref, wukv_ref, cc_ref, ss_ref,
                    fq_ref, fk_ref, fv_ref, qn_ref, qpe_ref, kv_ref, kpe_ref, flt_ref):
    subs = _sub_slices(x_ref.shape[0])
    ps = [_dot(_rms(x_ref[sl, :], g_ref[...]).astype(BF16), w_ref[...]) for sl in subs]
    scale = (MLA_NOPE + MLA_ROPE) ** -0.5 * LOG2E
    for sl, p in zip(subs, ps):
        fq_ref[sl, :] = (p[:, 0:512] * LOG2E).astype(BF16)
        fk_ref[sl, :] = p[:, 512:1024].astype(BF16)
        fv_ref[sl, :] = p[:, 1024:1536].astype(BF16)

        cc = cc_ref[sl, :]
        ss = ss_ref[sl, :]
        q = _dot(_rms(p[:, 1536:1792], gq_ref[...]).astype(BF16), wuq_ref[...])
        qn_ref[sl, :] = (q[:, 0:512] * scale).astype(BF16)
        cc2 = jnp.concatenate([cc, cc], axis=1)
        ss2 = jnp.concatenate([ss, ss], axis=1)
        qpe_ref[sl, :] = ((q[:, 512:768] * cc2 + q[:, 768:1024] * ss2) * scale).astype(BF16)

        kv_ref[sl, :] = _dot(_rms(p[:, 1792:1920], gkv_ref[...]).astype(BF16), wukv_ref[...]).astype(BF16)

        slab_a = p[:, 1920:2048]
        slab_b = p[:, 2048:2176]
        lane = lax.broadcasted_iota(jnp.int32, slab_a.shape, 1)
        roped = jnp.where(lane < MLA_ROPE, slab_a * cc + slab_b * ss, 0.0)
        tiled = roped + pltpu.roll(roped, 32, 1) + pltpu.roll(roped, 64, 1) + pltpu.roll(roped, 96, 1)
        kpe_ref[sl, :] = tiled.astype(BF16)
        flt_ref[:, sl] = slab_a.T[MLA_ROPE:MLA_ROPE + FOX_HEADS, :]


def _in_proj(x2d, g, w, gq, wuq, gkv, wukv, cc, ss):
    n = x2d.shape[0]
    rows = min(ROW_TILE, n)
    row = lambda c: pl.BlockSpec((rows, c), lambda i: (i, 0))
    full = lambda a: pl.BlockSpec(a.shape, lambda i: (0,) * a.ndim)
    out_shape = (
        jax.ShapeDtypeStruct((n, HEAD_W), BF16), jax.ShapeDtypeStruct((n, HEAD_W), BF16),
        jax.ShapeDtypeStruct((n, HEAD_W), BF16), jax.ShapeDtypeStruct((n, HEAD_W), BF16),
        jax.ShapeDtypeStruct((n, 256), BF16), jax.ShapeDtypeStruct((n, 1024), BF16),
        jax.ShapeDtypeStruct((n, LANES), BF16), jax.ShapeDtypeStruct((FOX_HEADS, n), F32),
    )
    return pl.pallas_call(
        _in_proj_kernel,
        grid=(n // rows,),
        in_specs=[row(D_MODEL), full(g), full(w), full(gq), full(wuq), full(gkv), full(wukv),
                  row(LANES), row(LANES)],
        out_specs=(row(HEAD_W), row(HEAD_W), row(HEAD_W), row(HEAD_W), row(256), row(1024), row(LANES),
                   pl.BlockSpec((FOX_HEADS, rows), lambda i: (0, i))),
        out_shape=out_shape,
        compiler_params=_params(("parallel",)),
        name="in_proj",
    )(x2d, g, w, gq, wuq, gkv, wukv, cc, ss)


def _fox_decay_kernel(fl_ref, b_ref, tri_ref, ck_ref):
    z = fl_ref[...] + b_ref[...]
    lf = jnp.minimum(z, 0.0) - jnp.log1p(jnp.exp(-jnp.abs(z)))
    seq = lf.shape[1]
    tri = tri_ref[...]
    carry = jnp.zeros((FOX_HEADS, 1), F32)
    zeros = jnp.zeros((FOX_HEADS, CUM_TILE), F32)
    for j in range(seq // CUM_TILE):
        v = lf[:, j * CUM_TILE:(j + 1) * CUM_TILE]
        hi = v.astype(BF16).astype(F32)
        r1 = v - hi
        mid = r1.astype(BF16).astype(F32)
        lo = r1 - mid
        parts = _dot(jnp.concatenate([hi, mid, lo, zeros], axis=0).astype(BF16), tri)
        cs = (parts[0:8] + parts[8:16]) + parts[16:24] + carry
        carry = cs[:, CUM_TILE - 1:CUM_TILE]
        d = cs * (-LOG2E)
        d_hi = d.astype(BF16).astype(F32)
        d_r = d - d_hi
        d_mid = d_r.astype(BF16).astype(F32)
        d_lo = d_r - d_mid
        rows = jnp.concatenate([d_hi, d_mid, d_lo, jnp.zeros((LANES - 3 * FOX_HEADS, CUM_TILE), F32)], axis=0)
        ck_ref[0, j * CUM_TILE:(j + 1) * CUM_TILE, :] = rows.T.astype(BF16)


def _fox_decay(flt, b_col, tri, batch, seq):
    return pl.pallas_call(
        _fox_decay_kernel,
        grid=(batch,),
        in_specs=[pl.BlockSpec((FOX_HEADS, seq), lambda b: (0, b)),
                  pl.BlockSpec((FOX_HEADS, 1), lambda b: (0, 0)),
                  pl.BlockSpec((CUM_TILE, CUM_TILE), lambda b: (0, 0))],
        out_specs=pl.BlockSpec((1, seq, LANES), lambda b: (b, 0, 0)),
        out_shape=jax.ShapeDtypeStruct((batch, seq, LANES), BF16),
        compiler_params=_params(("parallel",)),
        name="fox_decay",
    )(flt, b_col, tri)


def _probs(s_off, s_diag):
    m = jnp.max(s_diag, axis=-1, keepdims=True)
    if s_off is not None:
        m = jnp.maximum(m, jnp.max(s_off, axis=-1, keepdims=True))
    p_off = None if s_off is None else jnp.exp2(s_off - m).astype(BF16)
    return p_off, jnp.exp2(s_diag - m).astype(BF16)


def _attend(q_ref, k_ref, v_ref, o_ref, allowed):
    seq = q_ref.shape[1]
    out_lane = lax.broadcasted_iota(jnp.int32, (Q_TILE, LANES), 1)
    units = [(i, hh) for i in range(seq // Q_TILE) for hh in range(2)]

    def scores(i, hh):
        qs, qe = i * Q_TILE, (i + 1) * Q_TILE
        q = q_ref[hh, qs:qe, :]
        s_diag = jnp.where(allowed, _dot_nt(q, k_ref[hh, qs:qe, :]), NEG_BIG)
        s_off = _dot_nt(q, k_ref[hh, 0:qs, :]) if i > 0 else None
        return s_off, s_diag

    def values(i, hh, p_off, p_diag):
        qs, qe = i * Q_TILE, (i + 1) * Q_TILE
        acc = _dot(p_diag, v_ref[hh, qs:qe, :])
        if p_off is not None:
            acc = acc + _dot(p_off, v_ref[hh, 0:qs, :])
        return acc[:, 0:LANES] / acc[:, LANES:2 * LANES]

    n_units = len(units)
    s = {0: scores(*units[0])}
    if n_units > 1:
        s[1] = scores(*units[1])
    p = {0: _probs(*s.pop(0))}
    outs = []
    for n, (i, hh) in enumerate(units):
        if n + 2 < n_units:
            s[n + 2] = scores(*units[n + 2])
        if n + 1 < n_units:
            p[n + 1] = _probs(*s.pop(n + 1))
        outs.append(values(i, hh, *p.pop(n)))
        if hh == 1:
            o_ref[0, i * Q_TILE:(i + 1) * Q_TILE, :] = jnp.where(out_lane < 64, outs[0], outs[1]).astype(BF16)
            outs = []


def _tile_iota():
    row = lax.broadcasted_iota(jnp.int32, (Q_TILE, Q_TILE), 0)
    col = lax.broadcasted_iota(jnp.int32, (Q_TILE, Q_TILE), 1)
    return row, col


def _mla_attn_kernel(qn_ref, qpe_ref, kv_ref, kpe_ref, o_ref, qs_ref, ks_ref, vs_ref):
    j = pl.program_id(1)
    seq = qn_ref.shape[1]
    lane = lax.broadcasted_iota(jnp.int32, (seq, LANES), 1)
    qn = qn_ref[0]
    qpe = qpe_ref[0]
    kpe = kpe_ref[0]
    ones = jnp.ones((seq, LANES), BF16)
    for hh in range(2):
        nope_mask = (lane >= 64) if hh == 0 else (lane < 64)
        pe_slot = 2 * (j % 2) + hh
        pe_mask = (lane // MLA_ROPE) == pe_slot
        qs_ref[hh, :, 0:LANES] = jnp.where(nope_mask, qn, jnp.zeros_like(qn))
        qs_ref[hh, :, LANES:2 * LANES] = jnp.where(pe_mask, qpe, jnp.zeros_like(qpe))
        kvh = kv_ref[0, :, hh * LANES:(hh + 1) * LANES]
        ks_ref[hh, :, 0:LANES] = kvh
        ks_ref[hh, :, LANES:2 * LANES] = kpe
        vs_ref[hh, :, 0:LANES] = kvh
        vs_ref[hh, :, LANES:2 * LANES] = ones
    row, col = _tile_iota()
    _attend(qs_ref, ks_ref, vs_ref, o_ref, (col // CHUNK) <= (row // CHUNK))


def _mla_attn(qn, qpe, kv, kpe, batch, seq):
    qn, qpe, kv, kpe = (a.reshape(batch, seq, a.shape[-1]) for a in (qn, qpe, kv, kpe))
    pair = pltpu.VMEM((2, seq, 2 * LANES), BF16)
    return pl.pallas_call(
        _mla_attn_kernel,
        grid=(batch, MLA_HEADS // 2),
        in_specs=[pl.BlockSpec((1, seq, LANES), lambda b, j: (b, 0, j)),
                  pl.BlockSpec((1, seq, LANES), lambda b, j: (b, 0, j // 2)),
                  pl.BlockSpec((1, seq, 2 * LANES), lambda b, j: (b, 0, j)),
                  pl.BlockSpec((1, seq, LANES), lambda b, j: (b, 0, 0))],
        out_specs=pl.BlockSpec((1, seq, LANES), lambda b, j: (b, 0, j)),
        out_shape=jax.ShapeDtypeStruct((batch, seq, HEAD_W), BF16),
        scratch_shapes=[pair, pair, pair],
        compiler_params=_params(("parallel", "parallel")),
        name="mla_attn",
    )(qn, qpe, kv, kpe)


def _fox_attn_kernel(q_ref, k_ref, v_ref, ck_ref, o_ref, qs_ref, ks_ref, vs_ref):
    j = pl.program_id(1)
    seq = q_ref.shape[1]
    lane = lax.broadcasted_iota(jnp.int32, (seq, LANES), 1)
    q = q_ref[0]
    k = k_ref[0]
    v = v_ref[0]
    ck = ck_ref[0]
    ones = jnp.ones((seq, LANES), BF16)
    for hh in range(2):
        head = 2 * j + hh
        head_mask = (lane < 64) if hh == 0 else (lane >= 64)
        piece = jnp.where(lane < 3 * FOX_HEADS, lane % FOX_HEADS, -1) == head
        qs_ref[hh, :, 0:LANES] = jnp.where(head_mask, q, jnp.zeros_like(q))
        qs_ref[hh, :, LANES:2 * LANES] = jnp.where(piece, 1.0, 0.0).astype(BF16)
        ks_ref[hh, :, 0:LANES] = k
        ks_ref[hh, :, LANES:2 * LANES] = ck
        vs_ref[hh, :, 0:LANES] = v
        vs_ref[hh, :, LANES:2 * LANES] = ones
    row, col = _tile_iota()
    _attend(qs_ref, ks_ref, vs_ref, o_ref, col <= row)


def _fox_attn(fq, fk, fv, ck, batch, seq):
    fq, fk, fv = (a.reshape(batch, seq, HEAD_W) for a in (fq, fk, fv))
    spec = pl.BlockSpec((1, seq, LANES), lambda b, j: (b, 0, j))
    pair = pltpu.VMEM((2, seq, 2 * LANES), BF16)
    return pl.pallas_call(
        _fox_attn_kernel,
        grid=(batch, FOX_HEADS // 2),
        in_specs=[spec, spec, spec, pl.BlockSpec((1, seq, LANES), lambda b, j: (b, 0, 0))],
        out_specs=spec,
        out_shape=jax.ShapeDtypeStruct((batch, seq, HEAD_W), BF16),
        scratch_shapes=[pair, pair, pair],
        compiler_params=_params(("parallel", "parallel")),
        name="fox_attn",
    )(fq, fk, fv, ck)


def _lane_max(v):
    return jnp.max(v, axis=1, keepdims=True)


def _first_lane(hit, lane_f):
    return jnp.min(jnp.where(hit, lane_f, float(LANES)), axis=1, keepdims=True)


def _out_proj_kernel(om_ref, of_ref, x_ref, gm_ref, gf_ref, wo_ref, gn_ref, wr_ref, br_ref, tri_ref,
                     x1_ref, h2_ref, rt_ref, cnt_ref, carry_ref):
    @pl.when(pl.program_id(0) == 0)
    def _():
        carry_ref[...] = jnp.zeros_like(carry_ref)

    subs = _sub_slices(x_ref.shape[0])
    logits = [_project_out(sl, om_ref, of_ref, x_ref, gm_ref, gf_ref, wo_ref, gn_ref, wr_ref, br_ref, x1_ref, h2_ref)
              for sl in subs]
    carry = carry_ref[...]
    for sl, lg in zip(subs, logits):
        rt_ref[sl, :], carry = _route_rows(lg, tri_ref[...], carry)
    carry_ref[...] = carry
    cnt_ref[...] = carry


def _project_out(sl, om_ref, of_ref, x_ref, gm_ref, gf_ref, wo_ref, gn_ref, wr_ref, br_ref, x1_ref, h2_ref):
    a = _rms(om_ref[sl, :].astype(F32), gm_ref[...]).astype(BF16)
    b = _rms(of_ref[sl, :].astype(F32), gf_ref[...]).astype(BF16)
    x1 = x_ref[sl, :] + _dot(a, wo_ref[0:HEAD_W, :]) + _dot(b, wo_ref[HEAD_W:2 * HEAD_W, :])
    x1_ref[sl, :] = x1
    h2 = _rms(x1, gn_ref[...])
    h_hi = h2.astype(BF16)
    h2_ref[sl, :] = _pack_rows(h_hi.astype(F32))
    h_lo = (h2 - h_hi.astype(F32)).astype(BF16)
    t = _dot(h_hi, wr_ref[...])
    return (t[:, 0:LANES] + t[:, LANES:2 * LANES]) + _dot(h_lo, wr_ref[:, 0:LANES]) + br_ref[...]


def _route_rows(lg, tri, carry):
    lane = lax.broadcasted_iota(jnp.int32, lg.shape, 1)
    lane_f = lane.astype(F32)
    neg_inf = float("-inf")
    is_group = lane < N_GROUPS
    gl = jnp.where(is_group, lg, neg_inf)
    mg = _lane_max(gl)
    gi = _first_lane(gl == mg, lane_f)
    g_val = 1.0 / jnp.sum(jnp.where(is_group, jnp.exp(lg - mg), 0.0), axis=1, keepdims=True)
    group_of_lane = ((lane - N_GROUPS) >> 3).astype(F32)
    is_expert = (lane >= N_GROUPS) & (lane < N_GROUPS + N_EXPERTS) & (group_of_lane == gi)
    el = jnp.where(is_expert, lg, neg_inf)
    m1 = _lane_max(el)
    i1 = _first_lane(el == m1, lane_f)
    el2 = jnp.where(lane_f == i1, neg_inf, el)
    m2 = _lane_max(el2)
    i2 = _first_lane(el2 == m2, lane_f)
    r = jnp.exp(m2 - m1)
    g0 = g_val / (1.0 + r)
    g1 = g0 * r

    hit1 = lane_f == i1
    hit2 = lane_f == i2
    onehot = jnp.where(hit1 | hit2, 1.0, 0.0)
    before = _dot(tri, onehot.astype(BF16)) + carry
    r0 = jnp.sum(jnp.where(hit1, before, 0.0), axis=1, keepdims=True)
    r1 = jnp.sum(jnp.where(hit2, before, 0.0), axis=1, keepdims=True)

    vals = (i1 - N_GROUPS, i2 - N_GROUPS, g0, g1, r0, r1)
    rt = jnp.zeros_like(lg)
    for k, v in enumerate(vals):
        rt = jnp.where(lane == k, v, rt)
    return rt, carry + jnp.sum(onehot, axis=0, keepdims=True)


def _out_proj(om, of, x2d, gm, gf, wo, gn, wr, br, tri):
    n = x2d.shape[0]
    rows = tri.shape[0] * SUB_TILES
    row = lambda c: pl.BlockSpec((rows, c), lambda i: (i, 0))
    full = lambda a: pl.BlockSpec(a.shape, lambda i: (0,) * a.ndim)
    return pl.pallas_call(
        _out_proj_kernel,
        grid=(n // rows,),
        in_specs=[row(HEAD_W), row(HEAD_W), row(D_MODEL), full(gm), full(gf), full(wo), full(gn), full(wr),
                  full(br), full(tri)],
        out_specs=(row(D_MODEL), row(D_MODEL // 2), row(LANES), pl.BlockSpec((1, LANES), lambda i: (0, 0))),
        out_shape=(jax.ShapeDtypeStruct((n, D_MODEL), F32), jax.ShapeDtypeStruct((n, D_MODEL // 2), jnp.uint32),
                   jax.ShapeDtypeStruct((n, LANES), F32), jax.ShapeDtypeStruct((1, LANES), F32)),
        scratch_shapes=[pltpu.VMEM((1, LANES), F32)],
        compiler_params=_params(("arbitrary",)),
        name="out_proj",
    )(om, of, x2d, gm, gf, wo, gn, wr, br, tri)


def _gather_rows(table, idx):
    m, w = idx.shape[0], table.shape[1]
    workers = SC_CORES * SC_SUBCORES
    per_worker = m // workers
    assert per_worker * workers == m and per_worker % GATHER_WINDOW == 0
    mesh = plsc.VectorSubcoreMesh(core_axis_name="core", subcore_axis_name="subcore")

    @functools.partial(
        pl.kernel, mesh=mesh, out_type=jax.ShapeDtypeStruct((m, w), table.dtype),
        scratch_types=[pltpu.VMEM((GATHER_WINDOW,), jnp.int32), pltpu.VMEM((GATHER_WINDOW, w), table.dtype),
                       pltpu.SemaphoreType.DMA],
        name="gather_rows")
    def gather(table_hbm, idx_hbm, out_hbm, idx_v, rows_v, sem):
        base = (lax.axis_index("subcore") * SC_CORES + lax.axis_index("core")) * per_worker

        @pl.loop(0, per_worker // GATHER_WINDOW)
        def _(j):
            off = base + j * GATHER_WINDOW
            pltpu.sync_copy(idx_hbm.at[pl.ds(off, GATHER_WINDOW)], idx_v)
            pltpu.async_copy(table_hbm.at[idx_v], rows_v, sem).wait()
            pltpu.sync_copy(rows_v, out_hbm.at[pl.ds(off, GATHER_WINDOW)])

    return gather(table, idx)


def _experts_kernel(be_ref, bv_ref, xs_ref, wg_ref, wu_ref, wd_ref, o_ref, wg_s, wu_s, wd_s):
    i = pl.program_id(0)
    valid = bv_ref[i] != 0

    @pl.when(jnp.logical_and(valid, jnp.logical_or(i == 0, be_ref[i] != be_ref[jnp.maximum(i - 1, 0)])))
    def _():
        wg_s[...] = wg_ref[0, 0].astype(BF16)
        wu_s[...] = wu_ref[0, 0].astype(BF16)
        wd_s[...] = wd_ref[0, 0].astype(BF16)

    @pl.when(valid)
    def _():
        x = _unpack_rows(xs_ref[...]).astype(BF16)
        g = _dot(x, wg_s[...])
        u = _dot(x, wu_s[...])
        act = ((g * jax.nn.sigmoid(g)) * u).astype(BF16)
        o_ref[...] = _pack_rows(_dot(act, wd_s[...]).astype(BF16).astype(F32))

    @pl.when(jnp.logical_not(valid))
    def _():
        o_ref[...] = jnp.zeros_like(o_ref)


def _experts(layer, blk_expert, blk_valid, xs, w_gate, w_up, w_down):
    n_slots = xs.shape[0]
    n_blocks = n_slots // EXPERT_TILE
    grid_spec = pltpu.PrefetchScalarGridSpec(
        num_scalar_prefetch=2,
        grid=(n_blocks,),
        in_specs=[pl.BlockSpec((EXPERT_TILE, D_MODEL // 2), lambda i, be, bv: (i, 0)),
                  pl.BlockSpec((1, 1, D_MODEL, D_EXPERT), lambda i, be, bv: (layer, be[i], 0, 0)),
                  pl.BlockSpec((1, 1, D_MODEL, D_EXPERT), lambda i, be, bv: (layer, be[i], 0, 0)),
                  pl.BlockSpec((1, 1, D_EXPERT, D_MODEL), lambda i, be, bv: (layer, be[i], 0, 0))],
        out_specs=pl.BlockSpec((EXPERT_TILE, D_MODEL // 2), lambda i, be, bv: (i, 0)),
        scratch_shapes=[pltpu.VMEM((D_MODEL, D_EXPERT), BF16), pltpu.VMEM((D_MODEL, D_EXPERT), BF16),
                        pltpu.VMEM((D_EXPERT, D_MODEL), BF16)],
    )
    return pl.pallas_call(
        _experts_kernel,
        grid_spec=grid_spec,
        out_shape=jax.ShapeDtypeStruct((n_slots, D_MODEL // 2), jnp.uint32),
        compiler_params=_params(("arbitrary",)),
        name="experts",
    )(blk_expert, blk_valid, xs, w_gate, w_up, w_down)


def _combine_kernel(x_ref, y0_ref, y1_ref, rt_ref, g_ref, o_ref, *, final):
    g0 = rt_ref[:, TOP_K:TOP_K + 1]
    g1 = rt_ref[:, TOP_K + 1:TOP_K + 2]
    x = x_ref[...] + (_unpack_rows(y0_ref[...]) * g0 + _unpack_rows(y1_ref[...]) * g1)
    o_ref[...] = _rms(x, g_ref[...]) if final else x


def _combine(x1, y01, route, g, final):
    n = x1.shape[0]
    rows = min(ROW_TILE, n)
    steps = n // rows
    row = pl.BlockSpec((rows, D_MODEL), lambda i: (i, 0))
    return pl.pallas_call(
        functools.partial(_combine_kernel, final=final),
        grid=(steps,),
        in_specs=[row, pl.BlockSpec((rows, D_MODEL // 2), lambda i: (i, 0)),
                  pl.BlockSpec((rows, D_MODEL // 2), lambda i: (i + steps, 0)),
                  pl.BlockSpec((rows, LANES), lambda i: (i, 0)),
                  pl.BlockSpec((1, D_MODEL), lambda i: (0, 0))],
        out_specs=row,
        out_shape=jax.ShapeDtypeStruct((n, D_MODEL), F32),
        compiler_params=_params(("parallel",)),
        name="combine",
    )(x1, y01, y01, route, g)


def _swap_halves(w):
    half = w.shape[-1] // 2
    return jnp.concatenate([w[..., half:], w[..., :half]], axis=-1)


def _prep_in_weights(w_in):
    d = w_in.shape[0]
    q_lat, kv_lat, kr = w_in[:, 0:256], w_in[:, 256:384], w_in[:, 384:416]
    fq, fk, fv, fl = w_in[:, 416:928], w_in[:, 928:1440], w_in[:, 1440:1952], w_in[:, 1952:1960]
    slab_a = jnp.concatenate([kr, fl, jnp.zeros((d, LANES - MLA_ROPE - FOX_HEADS), F32)], axis=1)
    slab_b = jnp.concatenate([_swap_halves(kr), jnp.zeros((d, LANES - MLA_ROPE), F32)], axis=1)
    return jnp.concatenate([fq * (FOX_DIM ** -0.5), fk, fv, q_lat, kv_lat, slab_a, slab_b], axis=1).astype(BF16)


def _prep_uq(w_uq):
    w = w_uq.reshape(MLA_Q_RANK, MLA_HEADS, MLA_NOPE + MLA_ROPE)
    nope, pe = w[:, :, :MLA_NOPE], w[:, :, MLA_NOPE:]
    pairs = nope.reshape(MLA_Q_RANK, MLA_HEADS // 2, 2, MLA_NOPE)[:, :, ::-1, :].reshape(MLA_Q_RANK, -1)
    return jnp.concatenate([pairs, pe.reshape(MLA_Q_RANK, -1), _swap_halves(pe).reshape(MLA_Q_RANK, -1)],
                           axis=1).astype(BF16)


def _prep_ukv(w_ukv):
    w = w_ukv.reshape(MLA_KV_RANK, MLA_HEADS // 2, 2, 2, MLA_NOPE)
    even = w[:, :, 0, ::-1, :]
    odd = w[:, :, 1, :, :]
    return jnp.stack([even, odd], axis=2).reshape(MLA_KV_RANK, -1).astype(BF16)


def _prep_router(w_rg, b_rg, w_re, b_re):
    d = w_rg.shape[0]
    w = jnp.concatenate([w_rg, w_re, jnp.zeros((d, LANES - N_GROUPS - N_EXPERTS), F32)], axis=1)
    w_hi = w.astype(BF16)
    w_lo = (w - w_hi.astype(F32)).astype(BF16)
    b = jnp.concatenate([b_rg, b_re, jnp.zeros((LANES - N_GROUPS - N_EXPERTS,), F32)])[None, :]
    return jnp.concatenate([w_hi, w_lo], axis=1), b


def _rope_slabs(positions):
    half = MLA_ROPE // 2
    inv_freq = ROPE_THETA ** (-jnp.arange(half, dtype=F32) / half)
    ang = positions.astype(F32).reshape(-1)[:, None] * inv_freq
    cos, sin = jnp.cos(ang), jnp.sin(ang)
    reps = LANES // MLA_ROPE
    return jnp.tile(jnp.concatenate([cos, cos], axis=1), (1, reps)), jnp.tile(
        jnp.concatenate([-sin, sin], axis=1), (1, reps))


def _slot_layout(route, counts, n_blocks):
    eid = route[:, 0:TOP_K].astype(jnp.int32)
    rank = route[:, 2 * TOP_K:3 * TOP_K].astype(jnp.int32)
    counts = counts[0, N_GROUPS:N_GROUPS + N_EXPERTS].astype(jnp.int32)
    padded = (counts + EXPERT_TILE - 1) // EXPERT_TILE * EXPERT_TILE
    pad_end = jnp.cumsum(padded)
    pad_start = pad_end - padded
    experts = jnp.arange(N_EXPERTS, dtype=jnp.int32)
    dest = jnp.sum(jnp.where(eid[:, :, None] == experts, pad_start, 0), axis=-1) + rank
    blk_start = jnp.arange(n_blocks, dtype=jnp.int32) * EXPERT_TILE
    blk_expert = jnp.minimum(jnp.sum((blk_start[:, None] >= pad_end[None, :]).astype(jnp.int32), axis=1),
                             N_EXPERTS - 1)
    blk_valid = (blk_start < pad_end[-1]).astype(jnp.int32)
    n_assign = dest.size
    tok_sorted = (jnp.argsort(dest.reshape(-1)) // TOP_K).astype(jnp.int32)
    compact_shift = jnp.cumsum(counts) - counts - pad_start
    blk_shift = jnp.sum(jnp.where(blk_expert[:, None] == experts, compact_shift, 0), axis=-1)
    pos = blk_start[:, None] + jnp.arange(EXPERT_TILE, dtype=jnp.int32)[None, :] + blk_shift[:, None]
    slot_tok = tok_sorted[jnp.clip(pos, 0, n_assign - 1)].reshape(-1)
    return dest, blk_expert, blk_valid, slot_tok


def kernel(x, positions, attn_norm, w_in, b_forget, q_norm, w_uq, kv_norm, w_ukv, mla_out_norm, fox_out_norm,
           w_out, ffn_norm, w_router_group, b_router_group, w_router_expert, b_router_expert, w_gate, w_up,
           w_down, final_norm):
    batch, seq, d = x.shape
    n = batch * seq
    depth = w_in.shape[0]
    n_blocks = -(-(n * TOP_K) // EXPERT_TILE) + N_EXPERTS
    cc, ss = _rope_slabs(positions)
    tri = (jnp.arange(CUM_TILE)[:, None] <= jnp.arange(CUM_TILE)[None, :]).astype(BF16)
    sub = min(ROW_TILE, n) // SUB_TILES
    tri_rows = (jnp.arange(sub)[None, :] < jnp.arange(sub)[:, None]).astype(BF16)
    xf = x.reshape(n, d)
    for l in range(depth):
        fq, fk, fv, qn, qpe, kv, kpe, flt = _in_proj(
            xf, attn_norm[l][None, :], _prep_in_weights(w_in[l]), q_norm[l][None, :], _prep_uq(w_uq[l]),
            kv_norm[l][None, :], _prep_ukv(w_ukv[l]), cc, ss)
        ck = _fox_decay(flt, b_forget[l][:, None], tri, batch, seq)
        o_mla = _mla_attn(qn, qpe, kv, kpe, batch, seq).reshape(n, HEAD_W)
        o_fox = _fox_attn(fq, fk, fv, ck, batch, seq).reshape(n, HEAD_W)
        wr, br = _prep_router(w_router_group[l], b_router_group[l], w_router_expert[l], b_router_expert[l])
        x1, h2, route, counts = _out_proj(o_mla, o_fox, xf, mla_out_norm[l][None, :], fox_out_norm[l][None, :],
                                          w_out[l].astype(BF16), ffn_norm[l][None, :], wr, br, tri_rows)
        dest, blk_expert, blk_valid, slot_tok = _slot_layout(route, counts, n_blocks)
        xs = _gather_rows(h2, slot_tok)
        ys = _experts(l, blk_expert, blk_valid, xs, w_gate, w_up, w_down)
        y01 = _gather_rows(ys, dest.T.reshape(-1))
        final = l == depth - 1
        xf = _combine(x1, y01, route, final_norm[None, :] if final else ffn_norm[l][None, :], final)
    return xf.reshape(batch, seq, d)
```

```python
import functools

import jax
import jax.numpy as jnp
from jax import lax
from jax.experimental import pallas as pl
from jax.experimental.pallas import tpu as pltpu
from jax.experimental.pallas import tpu_sc as plsc

D_MODEL = 1024
CHUNK = 64
MLA_HEADS = 8
MLA_NOPE = 64
MLA_ROPE = 32
MLA_V = 64
MLA_Q_RANK = 256
MLA_KV_RANK = 128
ROPE_THETA = 10000.0
FOX_HEADS = 8
FOX_DIM = 64
HEAD_W = 512
N_GROUPS = 4
EXPERTS_PER_GROUP = 8
N_EXPERTS = 32
TOP_K = 2
D_EXPERT = 512
NORM_EPS = 1e-6

LANES = 128
IN_COLS = 2176
ROW_TILE = 512
SUB_TILES = 2
Q_TILE = 256
EXPERT_TILE = 256
CUM_TILE = 256
NEG_BIG = -1e30
LOG2E = 1.4426950408889634
VMEM_LIMIT = 48 * 1024 * 1024
SC_CORES = 2
SC_SUBCORES = 16
GATHER_WINDOW = 64

F32 = jnp.float32
BF16 = jnp.bfloat16


def _rms(x, g):
    return (x * lax.rsqrt(jnp.mean(x * x, axis=-1, keepdims=True) + NORM_EPS)) * g


def _dot(a, b):
    return jnp.dot(a, b, preferred_element_type=F32)


def _dot_nt(a, b):
    return lax.dot_general(a, b, (((1,), (1,)), ((), ())), preferred_element_type=F32)


def _params(sem):
    return pltpu.CompilerParams(dimension_semantics=sem, vmem_limit_bytes=VMEM_LIMIT)


def _sub_slices(rows):
    sub = rows // SUB_TILES
    return [slice(t * sub, (t + 1) * sub) for t in range(SUB_TILES)]


def _pack_rows(v):
    w = v.shape[1] // 2
    lo = lax.bitcast_convert_type(v[:, :w], jnp.uint32) >> 16
    hi = lax.bitcast_convert_type(v[:, w:], jnp.uint32) & jnp.uint32(0xFFFF0000)
    return hi | lo


def _unpack_rows(p):
    lo = lax.bitcast_convert_type(p << 16, F32)
    hi = lax.bitcast_convert_type(p & jnp.uint32(0xFFFF0000), F32)
    return jnp.concatenate([lo, hi], axis=1)


def _in_proj_kernel(x_ref, g_ref, w_ref, gq_ref, wuq_ref, gkv_ref, wukv_ref, cc_ref, ss_ref,
                    fq_ref, fk_ref, fv_ref, qn_ref, qpe_ref, kv_ref, kpe_ref, flt_ref):
    subs = _sub_slices(x_ref.shape[0])
    ps = [_dot(_rms(x_ref[sl, :], g_ref[...]).astype(BF16), w_ref[...]) for sl in subs]
    scale = (MLA_NOPE + MLA_ROPE) ** -0.5 * LOG2E
    for sl, p in zip(subs, ps):
        fq_ref[sl, :] = (p[:, 0:512] * LOG2E).astype(BF16)
        fk_ref[sl, :] = p[:, 512:1024].astype(BF16)
        fv_ref[sl, :] = p[:, 1024:1536].astype(BF16)

        cc = cc_ref[sl, :]
        ss = ss_ref[sl, :]
        q = _dot(_rms(p[:, 1536:1792], gq_ref[...]).astype(BF16), wuq_ref[...])
        qn_ref[sl, :] = (q[:, 0:512] * scale).astype(BF16)
        cc2 = jnp.concatenate([cc, cc], axis=1)
        ss2 = jnp.concatenate([ss, ss], axis=1)
        qpe_ref[sl, :] = ((q[:, 512:768] * cc2 + q[:, 768:1024] * ss2) * scale).astype(BF16)

        kv_ref[sl, :] = _dot(_rms(p[:, 1792:1920], gkv_ref[...]).astype(BF16), wukv_ref[...]).astype(BF16)

        slab_a = p[:, 1920:2048]
        slab_b = p[:, 2048:2176]
        lane = lax.broadcasted_iota(jnp.int32, slab_a.shape, 1)
        roped = jnp.where(lane < MLA_ROPE, slab_a * cc + slab_b * ss, 0.0)
        tiled = roped + pltpu.roll(roped, 32, 1) + pltpu.roll(roped, 64, 1) + pltpu.roll(roped, 96, 1)
        kpe_ref[sl, :] = tiled.astype(BF16)
        flt_ref[:, sl] = slab_a.T[MLA_ROPE:MLA_ROPE + FOX_HEADS, :]


def _in_proj(x2d, g, w, gq, wuq, gkv, wukv, cc, ss):
    n = x2d.shape[0]
    rows = min(ROW_TILE, n)
    row = lambda c: pl.BlockSpec((rows, c), lambda i: (i, 0))
    full = lambda a: pl.BlockSpec(a.shape, lambda i: (0,) * a.ndim)
    out_shape = (
        jax.ShapeDtypeStruct((n, HEAD_W), BF16), jax.ShapeDtypeStruct((n, HEAD_W), BF16),
        jax.ShapeDtypeStruct((n, HEAD_W), BF16), jax.ShapeDtypeStruct((n, HEAD_W), BF16),
        jax.ShapeDtypeStruct((n, 256), BF16), jax.ShapeDtypeStruct((n, 1024), BF16),
        jax.ShapeDtypeStruct((n, LANES), BF16), jax.ShapeDtypeStruct((FOX_HEADS, n), F32),
    )
    return pl.pallas_call(
        _in_proj_kernel,
        grid=(n // rows,),
        in_specs=[row(D_MODEL), full(g), full(w), full(gq), full(wuq), full(gkv), full(wukv),
                  row(LANES), row(LANES)],
        out_specs=(row(HEAD_W), row(HEAD_W), row(HEAD_W), row(HEAD_W), row(256), row(1024), row(LANES),
                   pl.BlockSpec((FOX_HEADS, rows), lambda i: (0, i))),
        out_shape=out_shape,
        compiler_params=_params(("parallel",)),
        name="in_proj",
    )(x2d, g, w, gq, wuq, gkv, wukv, cc, ss)


def _fox_decay_kernel(fl_ref, b_ref, tri_ref, ck_ref):
    z = fl_ref[...] + b_ref[...]
    lf = jnp.minimum(z, 0.0) - jnp.log1p(jnp.exp(-jnp.abs(z)))
    seq = lf.shape[1]
    tri = tri_ref[...]
    carry = jnp.zeros((FOX_HEADS, 1), F32)
    zeros = jnp.zeros((FOX_HEADS, CUM_TILE), F32)
    for j in range(seq // CUM_TILE):
        v = lf[:, j * CUM_TILE:(j + 1) * CUM_TILE]
        hi = v.astype(BF16).astype(F32)
        r1 = v - hi
        mid = r1.astype(BF16).astype(F32)
        lo = r1 - mid
        parts = _dot(jnp.concatenate([hi, mid, lo, zeros], axis=0).astype(BF16), tri)
        cs = (parts[0:8] + parts[8:16]) + parts[16:24] + carry
        carry = cs[:, CUM_TILE - 1:CUM_TILE]
        d = cs * (-LOG2E)
        d_hi = d.astype(BF16).astype(F32)
        d_r = d - d_hi
        d_mid = d_r.astype(BF16).astype(F32)
        d_lo = d_r - d_mid
        rows = jnp.concatenate([d_hi, d_mid, d_lo, jnp.zeros((LANES - 3 * FOX_HEADS, CUM_TILE), F32)], axis=0)
        ck_ref[0, j * CUM_TILE:(j + 1) * CUM_TILE, :] = rows.T.astype(BF16)


def _fox_decay(flt, b_col, tri, batch, seq):
    return pl.pallas_call(
        _fox_decay_kernel,
        grid=(batch,),
        in_specs=[pl.BlockSpec((FOX_HEADS, seq), lambda b: (0, b)),
                  pl.BlockSpec((FOX_HEADS, 1), lambda b: (0, 0)),
                  pl.BlockSpec((CUM_TILE, CUM_TILE), lambda b: (0, 0))],
        out_specs=pl.BlockSpec((1, seq, LANES), lambda b: (b, 0, 0)),
        out_shape=jax.ShapeDtypeStruct((batch, seq, LANES), BF16),
        compiler_params=_params(("parallel",)),
        name="fox_decay",
    )(flt, b_col, tri)


def _probs(s_off, s_diag):
    m = jnp.max(s_diag, axis=-1, keepdims=True)
    if s_off is not None:
        m = jnp.maximum(m, jnp.max(s_off, axis=-1, keepdims=True))
    p_off = None if s_off is None else jnp.exp2(s_off - m).astype(BF16)
    return p_off, jnp.exp2(s_diag - m).astype(BF16)


def _attend(q_ref, k_ref, v_ref, o_ref, allowed):
    seq = q_ref.shape[1]
    out_lane = lax.broadcasted_iota(jnp.int32, (Q_TILE, LANES), 1)
    units = [(i, hh) for i in range(seq // Q_TILE) for hh in range(2)]

    def scores(i, hh):
        qs, qe = i * Q_TILE, (i + 1) * Q_TILE
        q = q_ref[hh, qs:qe, :]
        s_diag = jnp.where(allowed, _dot_nt(q, k_ref[hh, qs:qe, :]), NEG_BIG)
        s_off = _dot_nt(q, k_ref[hh, 0:qs, :]) if i > 0 else None
        return s_off, s_diag

    def values(i, hh, p_off, p_diag):
        qs, qe = i * Q_TILE, (i + 1) * Q_TILE
        acc = _dot(p_diag, v_ref[hh, qs:qe, :])
        if p_off is not None:
            acc = acc + _dot(p_off, v_ref[hh, 0:qs, :])
        return acc[:, 0:LANES] / acc[:, LANES:2 * LANES]

    n_units = len(units)
    s = {0: scores(*units[0])}
    if n_units > 1:
        s[1] = scores(*units[1])
    p = {0: _probs(*s.pop(0))}
    outs = []
    for n, (i, hh) in enumerate(units):
        if n + 2 < n_units:
            s[n + 2] = scores(*units[n + 2])
        if n + 1 < n_units:
            p[n + 1] = _probs(*s.pop(n + 1))
        outs.append(values(i, hh, *p.pop(n)))
        if hh == 1:
            o_ref[0, i * Q_TILE:(i + 1) * Q_TILE, :] = jnp.where(out_lane < 64, outs[0], outs[1]).astype(BF16)
            outs = []


def _tile_iota():
    row = lax.broadcasted_iota(jnp.int32, (Q_TILE, Q_TILE), 0)
    col = lax.broadcasted_iota(jnp.int32, (Q_TILE, Q_TILE), 1)
    return row, col


def _mla_attn_kernel(qn_ref, qpe_ref, kv_ref, kpe_ref, o_ref, qs_ref, ks_ref, vs_ref):
    j = pl.program_id(1)
    seq = qn_ref.shape[1]
    lane = lax.broadcasted_iota(jnp.int32, (seq, LANES), 1)
    qn = qn_ref[0]
    qpe = qpe_ref[0]
    kpe = kpe_ref[0]
    ones = jnp.ones((seq, LANES), BF16)
    for hh in range(2):
        nope_mask = (lane >= 64) if hh == 0 else (lane < 64)
        pe_slot = 2 * (j % 2) + hh
        pe_mask = (lane // MLA_ROPE) == pe_slot
        qs_ref[hh, :, 0:LANES] = jnp.where(nope_mask, qn, jnp.zeros_like(qn))
        qs_ref[hh, :, LANES:2 * LANES] = jnp.where(pe_mask, qpe, jnp.zeros_like(qpe))
        kvh = kv_ref[0, :, hh * LANES:(hh + 1) * LANES]
        ks_ref[hh, :, 0:LANES] = kvh
        ks_ref[hh, :, LANES:2 * LANES] = kpe
        vs_ref[hh, :, 0:LANES] = kvh
        vs_ref[hh, :, LANES:2 * LANES] = ones
    row, col = _tile_iota()
    _attend(qs_ref, ks_ref, vs_ref, o_ref, (col // CHUNK) <= (row // CHUNK))


def _mla_attn(qn, qpe, kv, kpe, batch, seq):
    qn, qpe, kv, kpe = (a.reshape(batch, seq, a.shape[-1]) for a in (qn, qpe, kv, kpe))
    pair = pltpu.VMEM((2, seq, 2 * LANES), BF16)
    return pl.pallas_call(
        _mla_attn_kernel,
        grid=(batch, MLA_HEADS // 2),
        in_specs=[pl.BlockSpec((1, seq, LANES), lambda b, j: (b, 0, j)),
                  pl.BlockSpec((1, seq, LANES), lambda b, j: (b, 0, j // 2)),
                  pl.BlockSpec((1, seq, 2 * LANES), lambda b, j: (b, 0, j)),
                  pl.BlockSpec((1, seq, LANES), lambda b, j: (b, 0, 0))],
        out_specs=pl.BlockSpec((1, seq, LANES), lambda b, j: (b, 0, j)),
        out_shape=jax.ShapeDtypeStruct((batch, seq, HEAD_W), BF16),
        scratch_shapes=[pair, pair, pair],
        compiler_params=_params(("parallel", "parallel")),
        name="mla_attn",
    )(qn, qpe, kv, kpe)


def _fox_attn_kernel(q_ref, k_ref, v_ref, ck_ref, o_ref, qs_ref, ks_ref, vs_ref):
    j = pl.program_id(1)
    seq = q_ref.shape[1]
    lane = lax.broadcasted_iota(jnp.int32, (seq, LANES), 1)
    q = q_ref[0]
    k = k_ref[0]
    v = v_ref[0]
    ck = ck_ref[0]
    ones = jnp.ones((seq, LANES), BF16)
    for hh in range(2):
        head = 2 * j + hh
        head_mask = (lane < 64) if hh == 0 else (lane >= 64)
        piece = jnp.where(lane < 3 * FOX_HEADS, lane % FOX_HEADS, -1) == head
        qs_ref[hh, :, 0:LANES] = jnp.where(head_mask, q, jnp.zeros_like(q))
        qs_ref[hh, :, LANES:2 * LANES] = jnp.where(piece, 1.0, 0.0).astype(BF16)
        ks_ref[hh, :, 0:LANES] = k
        ks_ref[hh, :, LANES:2 * LANES] = ck
        vs_ref[hh, :, 0:LANES] = v
        vs_ref[hh, :, LANES:2 * LANES] = ones
    row, col = _tile_iota()
    _attend(qs_ref, ks_ref, vs_ref, o_ref, col <= row)


def _fox_attn(fq, fk, fv, ck, batch, seq):
    fq, fk, fv = (a.reshape(batch, seq, HEAD_W) for a in (fq, fk, fv))
    spec = pl.BlockSpec((1, seq, LANES), lambda b, j: (b, 0, j))
    pair = pltpu.VMEM((2, seq, 2 * LANES), BF16)
    return pl.pallas_call(
        _fox_attn_kernel,
        grid=(batch, FOX_HEADS // 2),
        in_specs=[spec, spec, spec, pl.BlockSpec((1, seq, LANES), lambda b, j: (b, 0, 0))],
        out_specs=spec,
        out_shape=jax.ShapeDtypeStruct((batch, seq, HEAD_W), BF16),
        scratch_shapes=[pair, pair, pair],
        compiler_params=_params(("parallel", "parallel")),
        name="fox_attn",
    )(fq, fk, fv, ck)


def _lane_max(v):
    return jnp.max(v, axis=1, keepdims=True)


def _first_lane(hit, lane_f):
    return jnp.min(jnp.where(hit, lane_f, float(LANES)), axis=1, keepdims=True)


def _out_proj_kernel(om_ref, of_ref, x_ref, gm_ref, gf_ref, wo_ref, gn_ref, wr_ref, br_ref, tri_ref,
                     x1_ref, h2_ref, rt_ref, cnt_ref, carry_ref):
    @pl.when(pl.program_id(0) == 0)
    def _():
        carry_ref[...] = jnp.zeros_like(carry_ref)

    subs = _sub_slices(x_ref.shape[0])
    logits = [_project_out(sl, om_ref, of_ref, x_ref, gm_ref, gf_ref, wo_ref, gn_ref, wr_ref, br_ref, x1_ref, h2_ref)
              for sl in subs]
    carry = carry_ref[...]
    for sl, lg in zip(subs, logits):
        rt_ref[sl, :], carry = _route_rows(lg, tri_ref[...], carry)
    carry_ref[...] = carry
    cnt_ref[...] = carry


def _project_out(sl, om_ref, of_ref, x_ref, gm_ref, gf_ref, wo_ref, gn_ref, wr_ref, br_ref, x1_ref, h2_ref):
    a = _rms(om_ref[sl, :].astype(F32), gm_ref[...]).astype(BF16)
    b = _rms(of_ref[sl, :].astype(F32), gf_ref[...]).astype(BF16)
    x1 = x_ref[sl, :] + _dot(a, wo_ref[0:HEAD_W, :]) + _dot(b, wo_ref[HEAD_W:2 * HEAD_W, :])
    x1_ref[sl, :] = x1
    h2 = _rms(x1, gn_ref[...])
    h_hi = h2.astype(BF16)
    h2_ref[sl, :] = _pack_rows(h_hi.astype(F32))
    h_lo = (h2 - h_hi.astype(F32)).astype(BF16)
    t = _dot(h_hi, wr_ref[...])
    return (t[:, 0:LANES] + t[:, LANES:2 * LANES]) + _dot(h_lo, wr_ref[:, 0:LANES]) + br_ref[...]


def _route_rows(lg, tri, carry):
    lane = lax.broadcasted_iota(jnp.int32, lg.shape, 1)
    lane_f = lane.astype(F32)
    neg_inf = float("-inf")
    is_group = lane < N_GROUPS
    gl = jnp.where(is_group, lg, neg_inf)
    mg = _lane_max(gl)
    gi = _first_lane(gl == mg, lane_f)
    g_val = 1.0 / jnp.sum(jnp.where(is_group, jnp.exp(lg - mg), 0.0), axis=1, keepdims=True)
    group_of_lane = ((lane - N_GROUPS) >> 3).astype(F32)
    is_expert = (lane >= N_GROUPS) & (lane < N_GROUPS + N_EXPERTS) & (group_of_lane == gi)
    el = jnp.where(is_expert, lg, neg_inf)
    m1 = _lane_max(el)
    i1 = _first_lane(el == m1, lane_f)
    el2 = jnp.where(lane_f == i1, neg_inf, el)
    m2 = _lane_max(el2)
    i2 = _first_lane(el2 == m2, lane_f)
    r = jnp.exp(m2 - m1)
    g0 = g_val / (1.0 + r)
    g1 = g0 * r

    hit1 = lane_f == i1
    hit2 = lane_f == i2
    onehot = jnp.where(hit1 | hit2, 1.0, 0.0)
    before = _dot(tri, onehot.astype(BF16)) + carry
    r0 = jnp.sum(jnp.where(hit1, before, 0.0), axis=1, keepdims=True)
    r1 = jnp.sum(jnp.where(hit2, before, 0.0), axis=1, keepdims=True)

    vals = (i1 - N_GROUPS, i2 - N_GROUPS, g0, g1, r0, r1)
    rt = jnp.zeros_like(lg)
    for k, v in enumerate(vals):
        rt = jnp.where(lane == k, v, rt)
    return rt, carry + jnp.sum(onehot, axis=0, keepdims=True)


def _out_proj(om, of, x2d, gm, gf, wo, gn, wr, br, tri):
    n = x2d.shape[0]
    rows = tri.shape[0] * SUB_TILES
    row = lambda c: pl.BlockSpec((rows, c), lambda i: (i, 0))
    full = lambda a: pl.BlockSpec(a.shape, lambda i: (0,) * a.ndim)
    return pl.pallas_call(
        _out_proj_kernel,
        grid=(n // rows,),
        in_specs=[row(HEAD_W), row(HEAD_W), row(D_MODEL), full(gm), full(gf), full(wo), full(gn), full(wr),
                  full(br), full(tri)],
        out_specs=(row(D_MODEL), row(D_MODEL // 2), row(LANES), pl.BlockSpec((1, LANES), lambda i: (0, 0))),
        out_shape=(jax.ShapeDtypeStruct((n, D_MODEL), F32), jax.ShapeDtypeStruct((n, D_MODEL // 2), jnp.uint32),
                   jax.ShapeDtypeStruct((n, LANES), F32), jax.ShapeDtypeStruct((1, LANES), F32)),
        scratch_shapes=[pltpu.VMEM((1, LANES), F32)],
        compiler_params=_params(("arbitrary",)),
        name="out_proj",
    )(om, of, x2d, gm, gf, wo, gn, wr, br, tri)


def _gather_rows(table, idx):
    m, w = idx.shape[0], table.shape[1]
    workers = SC_CORES * SC_SUBCORES
    per_worker = m // workers
    n_win = per_worker // GATHER_WINDOW
    assert per_worker * workers == m and n_win * GATHER_WINDOW == per_worker and n_win % 2 == 0
    mesh = plsc.VectorSubcoreMesh(core_axis_name="core", subcore_axis_name="subcore")

    @functools.partial(
        pl.kernel, mesh=mesh, out_type=jax.ShapeDtypeStruct((m, w), table.dtype),
        scratch_types=[pltpu.VMEM((per_worker,), jnp.int32), pltpu.VMEM((2, GATHER_WINDOW, w), table.dtype),
                       pltpu.SemaphoreType.DMA((2,)), pltpu.SemaphoreType.DMA((2,))],
        name="gather_rows")
    def gather(table_hbm, idx_hbm, out_hbm, idx_v, rows_v, gather_sems, write_sems):
        base = (lax.axis_index("subcore") * SC_CORES + lax.axis_index("core")) * per_worker
        pltpu.sync_copy(idx_hbm.at[pl.ds(base, per_worker)], idx_v)

        def gather_copy(win, buf):
            rows = idx_v.at[pl.ds(win * GATHER_WINDOW, GATHER_WINDOW)]
            return pltpu.make_async_copy(table_hbm.at[rows], rows_v.at[buf], gather_sems.at[buf])

        def write_copy(win, buf):
            dst = out_hbm.at[pl.ds(base + win * GATHER_WINDOW, GATHER_WINDOW)]
            return pltpu.make_async_copy(rows_v.at[buf], dst, write_sems.at[buf])

        gather_copy(0, 0).start()

        @pl.loop(0, n_win, step=2)
        def _(j):
            for buf in range(2):
                win = j + buf
                gather_copy(win, buf).wait()
                write_copy(win, buf).start()

                @pl.when(win + 1 < n_win)
                def _():
                    @pl.when(win >= 1)
                    def _():
                        write_copy(win - 1, 1 - buf).wait()

                    gather_copy(win + 1, 1 - buf).start()

        write_copy(n_win - 2, 0).wait()
        write_copy(n_win - 1, 1).wait()

    return gather(table, idx)


def _experts_kernel(be_ref, bv_ref, xs_ref, wg_ref, wu_ref, wd_ref, o_ref, wg_s, wu_s, wd_s):
    i = pl.program_id(0)
    valid = bv_ref[i] != 0

    @pl.when(jnp.logical_and(valid, jnp.logical_or(i == 0, be_ref[i] != be_ref[jnp.maximum(i - 1, 0)])))
    def _():
        wg_s[...] = wg_ref[0, 0].astype(BF16)
        wu_s[...] = wu_ref[0, 0].astype(BF16)
        wd_s[...] = wd_ref[0, 0].astype(BF16)

    @pl.when(valid)
    def _():
        x = _unpack_rows(xs_ref[...]).astype(BF16)
        g = _dot(x, wg_s[...])
        u = _dot(x, wu_s[...])
        act = ((g * jax.nn.sigmoid(g)) * u).astype(BF16)
        o_ref[...] = _pack_rows(_dot(act, wd_s[...]).astype(BF16).astype(F32))

    @pl.when(jnp.logical_not(valid))
    def _():
        o_ref[...] = jnp.zeros_like(o_ref)


def _experts(layer, blk_expert, blk_valid, xs, w_gate, w_up, w_down):
    n_slots = xs.shape[0]
    n_blocks = n_slots // EXPERT_TILE
    grid_spec = pltpu.PrefetchScalarGridSpec(
        num_scalar_prefetch=2,
        grid=(n_blocks,),
        in_specs=[pl.BlockSpec((EXPERT_TILE, D_MODEL // 2), lambda i, be, bv: (i, 0)),
                  pl.BlockSpec((1, 1, D_MODEL, D_EXPERT), lambda i, be, bv: (layer, be[i], 0, 0)),
                  pl.BlockSpec((1, 1, D_MODEL, D_EXPERT), lambda i, be, bv: (layer, be[i], 0, 0)),
                  pl.BlockSpec((1, 1, D_EXPERT, D_MODEL), lambda i, be, bv: (layer, be[i], 0, 0))],
        out_specs=pl.BlockSpec((EXPERT_TILE, D_MODEL // 2), lambda i, be, bv: (i, 0)),
        scratch_shapes=[pltpu.VMEM((D_MODEL, D_EXPERT), BF16), pltpu.VMEM((D_MODEL, D_EXPERT), BF16),
                        pltpu.VMEM((D_EXPERT, D_MODEL), BF16)],
    )
    return pl.pallas_call(
        _experts_kernel,
        grid_spec=grid_spec,
        out_shape=jax.ShapeDtypeStruct((n_slots, D_MODEL // 2), jnp.uint32),
        compiler_params=_params(("arbitrary",)),
        name="experts",
    )(blk_expert, blk_valid, xs, w_gate, w_up, w_down)


def _combine_kernel(x_ref, y0_ref, y1_ref, rt_ref, g_ref, o_ref, *, final):
    g0 = rt_ref[:, TOP_K:TOP_K + 1]
    g1 = rt_ref[:, TOP_K + 1:TOP_K + 2]
    x = x_ref[...] + (_unpack_rows(y0_ref[...]) * g0 + _unpack_rows(y1_ref[...]) * g1)
    o_ref[...] = _rms(x, g_ref[...]) if final else x


def _combine(x1, y01, route, g, final):
    n = x1.shape[0]
    rows = min(ROW_TILE, n)
    steps = n // rows
    row = pl.BlockSpec((rows, D_MODEL), lambda i: (i, 0))
    return pl.pallas_call(
        functools.partial(_combine_kernel, final=final),
        grid=(steps,),
        in_specs=[row, pl.BlockSpec((rows, D_MODEL // 2), lambda i: (i, 0)),
                  pl.BlockSpec((rows, D_MODEL // 2), lambda i: (i + steps, 0)),
                  pl.BlockSpec((rows, LANES), lambda i: (i, 0)),
                  pl.BlockSpec((1, D_MODEL), lambda i: (0, 0))],
        out_specs=row,
        out_shape=jax.ShapeDtypeStruct((n, D_MODEL), F32),
        compiler_params=_params(("parallel",)),
        name="combine",
    )(x1, y01, y01, route, g)


def _swap_halves(w):
    half = w.shape[-1] // 2
    return jnp.concatenate([w[..., half:], w[..., :half]], axis=-1)


def _prep_in_weights(w_in):
    d = w_in.shape[0]
    q_lat, kv_lat, kr = w_in[:, 0:256], w_in[:, 256:384], w_in[:, 384:416]
    fq, fk, fv, fl = w_in[:, 416:928], w_in[:, 928:1440], w_in[:, 1440:1952], w_in[:, 1952:1960]
    slab_a = jnp.concatenate([kr, fl, jnp.zeros((d, LANES - MLA_ROPE - FOX_HEADS), F32)], axis=1)
    slab_b = jnp.concatenate([_swap_halves(kr), jnp.zeros((d, LANES - MLA_ROPE), F32)], axis=1)
    return jnp.concatenate([fq * (FOX_DIM ** -0.5), fk, fv, q_lat, kv_lat, slab_a, slab_b], axis=1).astype(BF16)


def _prep_uq(w_uq):
    w = w_uq.reshape(MLA_Q_RANK, MLA_HEADS, MLA_NOPE + MLA_ROPE)
    nope, pe = w[:, :, :MLA_NOPE], w[:, :, MLA_NOPE:]
    pairs = nope.reshape(MLA_Q_RANK, MLA_HEADS // 2, 2, MLA_NOPE)[:, :, ::-1, :].reshape(MLA_Q_RANK, -1)
    return jnp.concatenate([pairs, pe.reshape(MLA_Q_RANK, -1), _swap_halves(pe).reshape(MLA_Q_RANK, -1)],
                           axis=1).astype(BF16)


def _prep_ukv(w_ukv):
    w = w_ukv.reshape(MLA_KV_RANK, MLA_HEADS // 2, 2, 2, MLA_NOPE)
    even = w[:, :, 0, ::-1, :]
    odd = w[:, :, 1, :, :]
    return jnp.stack([even, odd], axis=2).reshape(MLA_KV_RANK, -1).astype(BF16)


def _prep_router(w_rg, b_rg, w_re, b_re):
    d = w_rg.shape[0]
    w = jnp.concatenate([w_rg, w_re, jnp.zeros((d, LANES - N_GROUPS - N_EXPERTS), F32)], axis=1)
    w_hi = w.astype(BF16)
    w_lo = (w - w_hi.astype(F32)).astype(BF16)
    b = jnp.concatenate([b_rg, b_re, jnp.zeros((LANES - N_GROUPS - N_EXPERTS,), F32)])[None, :]
    return jnp.concatenate([w_hi, w_lo], axis=1), b


def _rope_slabs(positions):
    half = MLA_ROPE // 2
    inv_freq = ROPE_THETA ** (-jnp.arange(half, dtype=F32) / half)
    ang = positions.astype(F32).reshape(-1)[:, None] * inv_freq
    cos, sin = jnp.cos(ang), jnp.sin(ang)
    reps = LANES // MLA_ROPE
    return jnp.tile(jnp.concatenate([cos, cos], axis=1), (1, reps)), jnp.tile(
        jnp.concatenate([-sin, sin], axis=1), (1, reps))


def _slot_layout(route, counts, n_blocks):
    eid = route[:, 0:TOP_K].astype(jnp.int32)
    rank = route[:, 2 * TOP_K:3 * TOP_K].astype(jnp.int32)
    counts = counts[0, N_GROUPS:N_GROUPS + N_EXPERTS].astype(jnp.int32)
    padded = (counts + EXPERT_TILE - 1) // EXPERT_TILE * EXPERT_TILE
    pad_end = jnp.cumsum(padded)
    pad_start = pad_end - padded
    experts = jnp.arange(N_EXPERTS, dtype=jnp.int32)
    dest = jnp.sum(jnp.where(eid[:, :, None] == experts, pad_start, 0), axis=-1) + rank
    blk_start = jnp.arange(n_blocks, dtype=jnp.int32) * EXPERT_TILE
    blk_expert = jnp.minimum(jnp.sum((blk_start[:, None] >= pad_end[None, :]).astype(jnp.int32), axis=1),
                             N_EXPERTS - 1)
    blk_valid = (blk_start < pad_end[-1]).astype(jnp.int32)
    n_assign = dest.size
    tok_sorted = (jnp.argsort(dest.reshape(-1)) // TOP_K).astype(jnp.int32)
    compact_shift = jnp.cumsum(counts) - counts - pad_start
    blk_shift = jnp.sum(jnp.where(blk_expert[:, None] == experts, compact_shift, 0), axis=-1)
    pos = blk_start[:, None] + jnp.arange(EXPERT_TILE, dtype=jnp.int32)[None, :] + blk_shift[:, None]
    slot_tok = tok_sorted[jnp.clip(pos, 0, n_assign - 1)].reshape(-1)
    return dest, blk_expert, blk_valid, slot_tok


def kernel(x, positions, attn_norm, w_in, b_forget, q_norm, w_uq, kv_norm, w_ukv, mla_out_norm, fox_out_norm,
           w_out, ffn_norm, w_router_group, b_router_group, w_router_expert, b_router_expert, w_gate, w_up,
           w_down, final_norm):
    batch, seq, d = x.shape
    n = batch * seq
    depth = w_in.shape[0]
    n_blocks = -(-(n * TOP_K) // EXPERT_TILE) + N_EXPERTS
    cc, ss = _rope_slabs(positions)
    tri = (jnp.arange(CUM_TILE)[:, None] <= jnp.arange(CUM_TILE)[None, :]).astype(BF16)
    sub = min(ROW_TILE, n) // SUB_TILES
    tri_rows = (jnp.arange(sub)[None, :] < jnp.arange(sub)[:, None]).astype(BF16)
    xf = x.reshape(n, d)
    for l in range(depth):
        fq, fk, fv, qn, qpe, kv, kpe, flt = _in_proj(
            xf, attn_norm[l][None, :], _prep_in_weights(w_in[l]), q_norm[l][None, :], _prep_uq(w_uq[l]),
            kv_norm[l][None, :], _prep_ukv(w_ukv[l]), cc, ss)
        ck = _fox_decay(flt, b_forget[l][:, None], tri, batch, seq)
        o_mla = _mla_attn(qn, qpe, kv, kpe, batch, seq).reshape(n, HEAD_W)
        o_fox = _fox_attn(fq, fk, fv, ck, batch, seq).reshape(n, HEAD_W)
        wr, br = _prep_router(w_router_group[l], b_router_group[l], w_router_expert[l], b_router_expert[l])
        x1, h2, route, counts = _out_proj(o_mla, o_fox, xf, mla_out_norm[l][None, :], fox_out_norm[l][None, :],
                                          w_out[l].astype(BF16), ffn_norm[l][None, :], wr, br, tri_rows)
        dest, blk_expert, blk_valid, slot_tok = _slot_layout(route, counts, n_blocks)
        xs = _gather_rows(h2, slot_tok)
        ys = _experts(l, blk_expert, blk_valid, xs, w_gate, w_up, w_down)
        y01 = _gather_rows(ys, dest.T.reshape(-1))
        final = l == depth - 1
        xf = _combine(x1, y01, route, final_norm[None, :] if final else ffn_norm[l][None, :], final)
    return xf.reshape(batch, seq, d)
```

```python
import functools

import jax
import jax.numpy as jnp
from jax import lax
from jax.experimental import pallas as pl
from jax.experimental.pallas import tpu as pltpu
from jax.experimental.pallas import tpu_sc as plsc

D_MODEL = 1024
CHUNK = 64
MLA_HEADS = 8
MLA_NOPE = 64
MLA_ROPE = 32
MLA_V = 64
MLA_Q_RANK = 256
MLA_KV_RANK = 128
ROPE_THETA = 10000.0
FOX_HEADS = 8
FOX_DIM = 64
HEAD_W = 512
N_GROUPS = 4
EXPERTS_PER_GROUP = 8
N_EXPERTS = 32
TOP_K = 2
D_EXPERT = 512
NORM_EPS = 1e-6

LANES = 128
IN_COLS = 2176
ROW_TILE = 512
SUB_TILES = 2
Q_TILE = 256
EXPERT_TILE = 256
CUM_TILE = 256
NEG_BIG = -1e30
LOG2E = 1.4426950408889634
VMEM_LIMIT = 48 * 1024 * 1024
SC_CORES = 2
SC_SUBCORES = 16
GATHER_WINDOW = 64

F32 = jnp.float32
BF16 = jnp.bfloat16


def _rms(x, g):
    return (x * lax.rsqrt(jnp.mean(x * x, axis=-1, keepdims=True) + NORM_EPS)) * g


def _dot(a, b):
    return jnp.dot(a, b, preferred_element_type=F32)


def _dot_nt(a, b):
    return lax.dot_general(a, b, (((1,), (1,)), ((), ())), preferred_element_type=F32)


def _params(sem):
    return pltpu.CompilerParams(dimension_semantics=sem, vmem_limit_bytes=VMEM_LIMIT)


def _sub_slices(rows):
    sub = rows // SUB_TILES
    return [slice(t * sub, (t + 1) * sub) for t in range(SUB_TILES)]


def _pack_rows(v):
    w = v.shape[1] // 2
    lo = lax.bitcast_convert_type(v[:, :w], jnp.uint32) >> 16
    hi = lax.bitcast_convert_type(v[:, w:], jnp.uint32) & jnp.uint32(0xFFFF0000)
    return hi | lo


def _unpack_rows(p):
    lo = lax.bitcast_convert_type(p << 16, F32)
    hi = lax.bitcast_convert_type(p & jnp.uint32(0xFFFF0000), F32)
    return jnp.concatenate([lo, hi], axis=1)


def _in_proj_kernel(x_ref, g_ref, w_ref, gq_ref, wuq_ref, gkv_ref, wukv_ref, cc_ref, ss_ref,
                    fq_ref, fk_ref, fv_ref, qn_ref, qpe_ref, kv_ref, kpe_ref, flt_ref):
    subs = _sub_slices(x_ref.shape[0])
    ps = [_dot(_rms(x_ref[sl, :], g_ref[...]).astype(BF16), w_ref[...]) for sl in subs]
    scale = (MLA_NOPE + MLA_ROPE) ** -0.5 * LOG2E
    for sl, p in zip(subs, ps):
        fq_ref[sl, :] = (p[:, 0:512] * LOG2E).astype(BF16)
        fk_ref[sl, :] = p[:, 512:1024].astype(BF16)
        fv_ref[sl, :] = p[:, 1024:1536].astype(BF16)

        cc = cc_ref[sl, :]
        ss = ss_ref[sl, :]
        q = _dot(_rms(p[:, 1536:1792], gq_ref[...]).astype(BF16), wuq_ref[...])
        qn_ref[sl, :] = (q[:, 0:512] * scale).astype(BF16)
        cc2 = jnp.concatenate([cc, cc], axis=1)
        ss2 = jnp.concatenate([ss, ss], axis=1)
        qpe_ref[sl, :] = ((q[:, 512:768] * cc2 + q[:, 768:1024] * ss2) * scale).astype(BF16)

        kv_ref[sl, :] = _dot(_rms(p[:, 1792:1920], gkv_ref[...]).astype(BF16), wukv_ref[...]).astype(BF16)

        slab_a = p[:, 1920:2048]
        slab_b = p[:, 2048:2176]
        lane = lax.broadcasted_iota(jnp.int32, slab_a.shape, 1)
        roped = jnp.where(lane < MLA_ROPE, slab_a * cc + slab_b * ss, 0.0)
        tiled = roped + pltpu.roll(roped, 32, 1) + pltpu.roll(roped, 64, 1) + pltpu.roll(roped, 96, 1)
        kpe_ref[sl, :] = tiled.astype(BF16)
        flt_ref[:, sl] = slab_a.T[MLA_ROPE:MLA_ROPE + FOX_HEADS, :]


def _in_proj(x2d, g, w, gq, wuq, gkv, wukv, cc, ss):
    n = x2d.shape[0]
    rows = min(ROW_TILE, n)
    row = lambda c: pl.BlockSpec((rows, c), lambda i: (i, 0))
    full = lambda a: pl.BlockSpec(a.shape, lambda i: (0,) * a.ndim)
    out_shape = (
        jax.ShapeDtypeStruct((n, HEAD_W), BF16), jax.ShapeDtypeStruct((n, HEAD_W), BF16),
        jax.ShapeDtypeStruct((n, HEAD_W), BF16), jax.ShapeDtypeStruct((n, HEAD_W), BF16),
        jax.ShapeDtypeStruct((n, 256), BF16), jax.ShapeDtypeStruct((n, 1024), BF16),
        jax.ShapeDtypeStruct((n, LANES), BF16), jax.ShapeDtypeStruct((FOX_HEADS, n), F32),
    )
    return pl.pallas_call(
        _in_proj_kernel,
        grid=(n // rows,),
        in_specs=[row(D_MODEL), full(g), full(w), full(gq), full(wuq), full(gkv), full(wukv),
                  row(LANES), row(LANES)],
        out_specs=(row(HEAD_W), row(HEAD_W), row(HEAD_W), row(HEAD_W), row(256), row(1024), row(LANES),
                   pl.BlockSpec((FOX_HEADS, rows), lambda i: (0, i))),
        out_shape=out_shape,
        compiler_params=_params(("parallel",)),
        name="in_proj",
    )(x2d, g, w, gq, wuq, gkv, wukv, cc, ss)


def _fox_decay_kernel(fl_ref, b_ref, tri_ref, ck_ref):
    z = fl_ref[...] + b_ref[...]
    lf = jnp.minimum(z, 0.0) - jnp.log1p(jnp.exp(-jnp.abs(z)))
    seq = lf.shape[1]
    tri = tri_ref[...]
    carry = jnp.zeros((FOX_HEADS, 1), F32)
    zeros = jnp.zeros((FOX_HEADS, CUM_TILE), F32)
    for j in range(seq // CUM_TILE):
        v = lf[:, j * CUM_TILE:(j + 1) * CUM_TILE]
        hi = v.astype(BF16).astype(F32)
        r1 = v - hi
        mid = r1.astype(BF16).astype(F32)
        lo = r1 - mid
        parts = _dot(jnp.concatenate([hi, mid, lo, zeros], axis=0).astype(BF16), tri)
        cs = (parts[0:8] + parts[8:16]) + parts[16:24] + carry
        carry = cs[:, CUM_TILE - 1:CUM_TILE]
        d = cs * (-LOG2E)
        d_hi = d.astype(BF16).astype(F32)
        d_r = d - d_hi
        d_mid = d_r.astype(BF16).astype(F32)
        d_lo = d_r - d_mid
        rows = jnp.concatenate([d_hi, d_mid, d_lo, jnp.zeros((LANES - 3 * FOX_HEADS, CUM_TILE), F32)], axis=0)
        ck_ref[0, j * CUM_TILE:(j + 1) * CUM_TILE, :] = rows.T.astype(BF16)


def _fox_decay(flt, b_col, tri, batch, seq):
    return pl.pallas_call(
        _fox_decay_kernel,
        grid=(batch,),
        in_specs=[pl.BlockSpec((FOX_HEADS, seq), lambda b: (0, b)),
                  pl.BlockSpec((FOX_HEADS, 1), lambda b: (0, 0)),
                  pl.BlockSpec((CUM_TILE, CUM_TILE), lambda b: (0, 0))],
        out_specs=pl.BlockSpec((1, seq, LANES), lambda b: (b, 0, 0)),
        out_shape=jax.ShapeDtypeStruct((batch, seq, LANES), BF16),
        compiler_params=_params(("parallel",)),
        name="fox_decay",
    )(flt, b_col, tri)


def _probs(s_off, s_diag):
    m = jnp.max(s_diag, axis=-1, keepdims=True)
    if s_off is not None:
        m = jnp.maximum(m, jnp.max(s_off, axis=-1, keepdims=True))
    p_off = None if s_off is None else jnp.exp2(s_off - m).astype(BF16)
    return p_off, jnp.exp2(s_diag - m).astype(BF16)


def _attend(q_ref, k_ref, v_ref, o_ref, allowed):
    seq = q_ref.shape[1]
    out_lane = lax.broadcasted_iota(jnp.int32, (Q_TILE, LANES), 1)
    units = [(i, hh) for i in range(seq // Q_TILE) for hh in range(2)]

    def scores(i, hh):
        qs, qe = i * Q_TILE, (i + 1) * Q_TILE
        q = q_ref[hh, qs:qe, :]
        s_diag = jnp.where(allowed, _dot_nt(q, k_ref[hh, qs:qe, :]), NEG_BIG)
        s_off = _dot_nt(q, k_ref[hh, 0:qs, :]) if i > 0 else None
        return s_off, s_diag

    def values(i, hh, p_off, p_diag):
        qs, qe = i * Q_TILE, (i + 1) * Q_TILE
        acc = _dot(p_diag, v_ref[hh, qs:qe, :])
        if p_off is not None:
            acc = acc + _dot(p_off, v_ref[hh, 0:qs, :])
        return acc[:, 0:LANES] / acc[:, LANES:2 * LANES]

    n_units = len(units)
    s = {0: scores(*units[0])}
    if n_units > 1:
        s[1] = scores(*units[1])
    p = {0: _probs(*s.pop(0))}
    outs = []
    for n, (i, hh) in enumerate(units):
        if n + 2 < n_units:
            s[n + 2] = scores(*units[n + 2])
        if n + 1 < n_units:
            p[n + 1] = _probs(*s.pop(n + 1))
        outs.append(values(i, hh, *p.pop(n)))
        if hh == 1:
            o_ref[0, i * Q_TILE:(i + 1) * Q_TILE, :] = jnp.where(out_lane < 64, outs[0], outs[1]).astype(BF16)
            outs = []


def _tile_iota():
    row = lax.broadcasted_iota(jnp.int32, (Q_TILE, Q_TILE), 0)
    col = lax.broadcasted_iota(jnp.int32, (Q_TILE, Q_TILE), 1)
    return row, col


def _mla_attn_kernel(qn_ref, qpe_ref, kv_ref, kpe_ref, o_ref, qs_ref, ks_ref, vs_ref):
    j = pl.program_id(1)
    seq = qn_ref.shape[1]
    lane = lax.broadcasted_iota(jnp.int32, (seq, LANES), 1)
    qn = qn_ref[0]
    qpe = qpe_ref[0]
    kpe = kpe_ref[0]
    ones = jnp.ones((seq, LANES), BF16)
    for hh in range(2):
        nope_mask = (lane >= 64) if hh == 0 else (lane < 64)
        pe_slot = 2 * (j % 2) + hh
        pe_mask = (lane // MLA_ROPE) == pe_slot
        qs_ref[hh, :, 0:LANES] = jnp.where(nope_mask, qn, jnp.zeros_like(qn))
        qs_ref[hh, :, LANES:2 * LANES] = jnp.where(pe_mask, qpe, jnp.zeros_like(qpe))
        kvh = kv_ref[0, :, hh * LANES:(hh + 1) * LANES]
        ks_ref[hh, :, 0:LANES] = kvh
        ks_ref[hh, :, LANES:2 * LANES] = kpe
        vs_ref[hh, :, 0:LANES] = kvh
        vs_ref[hh, :, LANES:2 * LANES] = ones
    row, col = _tile_iota()
    _attend(qs_ref, ks_ref, vs_ref, o_ref, (col // CHUNK) <= (row // CHUNK))


def _mla_attn(qn, qpe, kv, kpe, batch, seq):
    qn, qpe, kv, kpe = (a.reshape(batch, seq, a.shape[-1]) for a in (qn, qpe, kv, kpe))
    pair = pltpu.VMEM((2, seq, 2 * LANES), BF16)
    return pl.pallas_call(
        _mla_attn_kernel,
        grid=(batch, MLA_HEADS // 2),
        in_specs=[pl.BlockSpec((1, seq, LANES), lambda b, j: (b, 0, j)),
                  pl.BlockSpec((1, seq, LANES), lambda b, j: (b, 0, j // 2)),
                  pl.BlockSpec((1, seq, 2 * LANES), lambda b, j: (b, 0, j)),
                  pl.BlockSpec((1, seq, LANES), lambda b, j: (b, 0, 0))],
        out_specs=pl.BlockSpec((1, seq, LANES), lambda b, j: (b, 0, j)),
        out_shape=jax.ShapeDtypeStruct((batch, seq, HEAD_W), BF16),
        scratch_shapes=[pair, pair, pair],
        compiler_params=_params(("parallel", "parallel")),
        name="mla_attn",
    )(qn, qpe, kv, kpe)


def _fox_attn_kernel(q_ref, k_ref, v_ref, ck_ref, o_ref, qs_ref, ks_ref, vs_ref):
    j = pl.program_id(1)
    seq = q_ref.shape[1]
    lane = lax.broadcasted_iota(jnp.int32, (seq, LANES), 1)
    q = q_ref[0]
    k = k_ref[0]
    v = v_ref[0]
    ck = ck_ref[0]
    ones = jnp.ones((seq, LANES), BF16)
    for hh in range(2):
        head = 2 * j + hh
        head_mask = (lane < 64) if hh == 0 else (lane >= 64)
        piece = jnp.where(lane < 3 * FOX_HEADS, lane % FOX_HEADS, -1) == head
        qs_ref[hh, :, 0:LANES] = jnp.where(head_mask, q, jnp.zeros_like(q))
        qs_ref[hh, :, LANES:2 * LANES] = jnp.where(piece, 1.0, 0.0).astype(BF16)
        ks_ref[hh, :, 0:LANES] = k
        ks_ref[hh, :, LANES:2 * LANES] = ck
        vs_ref[hh, :, 0:LANES] = v
        vs_ref[hh, :, LANES:2 * LANES] = ones
    row, col = _tile_iota()
    _attend(qs_ref, ks_ref, vs_ref, o_ref, col <= row)


def _fox_attn(fq, fk, fv, ck, batch, seq):
    fq, fk, fv = (a.reshape(batch, seq, HEAD_W) for a in (fq, fk, fv))
    spec = pl.BlockSpec((1, seq, LANES), lambda b, j: (b, 0, j))
    pair = pltpu.VMEM((2, seq, 2 * LANES), BF16)
    return pl.pallas_call(
        _fox_attn_kernel,
        grid=(batch, FOX_HEADS // 2),
        in_specs=[spec, spec, spec, pl.BlockSpec((1, seq, LANES), lambda b, j: (b, 0, 0))],
        out_specs=spec,
        out_shape=jax.ShapeDtypeStruct((batch, seq, HEAD_W), BF16),
        scratch_shapes=[pair, pair, pair],
        compiler_params=_params(("parallel", "parallel")),
        name="fox_attn",
    )(fq, fk, fv, ck)


def _lane_max(v):
    return jnp.max(v, axis=1, keepdims=True)


def _first_lane(hit, lane_f):
    return jnp.min(jnp.where(hit, lane_f, float(LANES)), axis=1, keepdims=True)


def _out_proj_kernel(om_ref, of_ref, x_ref, gm_ref, gf_ref, wo_ref, gn_ref, wr_ref, br_ref, tri_ref,
                     x1_ref, h2_ref, rt_ref, cnt_ref, carry_ref):
    @pl.when(pl.program_id(0) == 0)
    def _():
        carry_ref[...] = jnp.zeros_like(carry_ref)

    subs = _sub_slices(x_ref.shape[0])
    logits = [_project_out(sl, om_ref, of_ref, x_ref, gm_ref, gf_ref, wo_ref, gn_ref, wr_ref, br_ref, x1_ref, h2_ref)
              for sl in subs]
    carry = carry_ref[...]
    for sl, lg in zip(subs, logits):
        rt_ref[sl, :], carry = _route_rows(lg, tri_ref[...], carry)
    carry_ref[...] = carry
    cnt_ref[...] = carry


def _project_out(sl, om_ref, of_ref, x_ref, gm_ref, gf_ref, wo_ref, gn_ref, wr_ref, br_ref, x1_ref, h2_ref):
    a = _rms(om_ref[sl, :].astype(F32), gm_ref[...]).astype(BF16)
    b = _rms(of_ref[sl, :].astype(F32), gf_ref[...]).astype(BF16)
    x1 = x_ref[sl, :] + _dot(a, wo_ref[0:HEAD_W, :]) + _dot(b, wo_ref[HEAD_W:2 * HEAD_W, :])
    x1_ref[sl, :] = x1
    h2 = _rms(x1, gn_ref[...])
    h_hi = h2.astype(BF16)
    h2_ref[sl, :] = _pack_rows(h_hi.astype(F32))
    h_lo = (h2 - h_hi.astype(F32)).astype(BF16)
    t = _dot(h_hi, wr_ref[...])
    return (t[:, 0:LANES] + t[:, LANES:2 * LANES]) + _dot(h_lo, wr_ref[:, 0:LANES]) + br_ref[...]


def _route_rows(lg, tri, carry):
    lane = lax.broadcasted_iota(jnp.int32, lg.shape, 1)
    lane_f = lane.astype(F32)
    neg_inf = float("-inf")
    is_group = lane < N_GROUPS
    gl = jnp.where(is_group, lg, neg_inf)
    mg = _lane_max(gl)
    gi = _first_lane(gl == mg, lane_f)
    g_val = 1.0 / jnp.sum(jnp.where(is_group, jnp.exp(lg - mg), 0.0), axis=1, keepdims=True)
    group_of_lane = ((lane - N_GROUPS) >> 3).astype(F32)
    is_expert = (lane >= N_GROUPS) & (lane < N_GROUPS + N_EXPERTS) & (group_of_lane == gi)
    el = jnp.where(is_expert, lg, neg_inf)
    m1 = _lane_max(el)
    i1 = _first_lane(el == m1, lane_f)
    el2 = jnp.where(lane_f == i1, neg_inf, el)
    m2 = _lane_max(el2)
    i2 = _first_lane(el2 == m2, lane_f)
    r = jnp.exp(m2 - m1)
    g0 = g_val / (1.0 + r)
    g1 = g0 * r

    hit1 = lane_f == i1
    hit2 = lane_f == i2
    onehot = jnp.where(hit1 | hit2, 1.0, 0.0)
    before = _dot(tri, onehot.astype(BF16)) + carry
    r0 = jnp.sum(jnp.where(hit1, before, 0.0), axis=1, keepdims=True)
    r1 = jnp.sum(jnp.where(hit2, before, 0.0), axis=1, keepdims=True)

    vals = (i1 - N_GROUPS, i2 - N_GROUPS, g0, g1, r0, r1)
    rt = jnp.zeros_like(lg)
    for k, v in enumerate(vals):
        rt = jnp.where(lane == k, v, rt)
    return rt, carry + jnp.sum(onehot, axis=0, keepdims=True)


def _out_proj(om, of, x2d, gm, gf, wo, gn, wr, br, tri):
    n = x2d.shape[0]
    rows = tri.shape[0] * SUB_TILES
    row = lambda c: pl.BlockSpec((rows, c), lambda i: (i, 0))
    full = lambda a: pl.BlockSpec(a.shape, lambda i: (0,) * a.ndim)
    return pl.pallas_call(
        _out_proj_kernel,
        grid=(n // rows,),
        in_specs=[row(HEAD_W), row(HEAD_W), row(D_MODEL), full(gm), full(gf), full(wo), full(gn), full(wr),
                  full(br), full(tri)],
        out_specs=(row(D_MODEL), row(D_MODEL // 2), row(LANES), pl.BlockSpec((1, LANES), lambda i: (0, 0))),
        out_shape=(jax.ShapeDtypeStruct((n, D_MODEL), F32), jax.ShapeDtypeStruct((n, D_MODEL // 2), jnp.uint32),
                   jax.ShapeDtypeStruct((n, LANES), F32), jax.ShapeDtypeStruct((1, LANES), F32)),
        scratch_shapes=[pltpu.VMEM((1, LANES), F32)],
        compiler_params=_params(("arbitrary",)),
        name="out_proj",
    )(om, of, x2d, gm, gf, wo, gn, wr, br, tri)


def _gather_rows(table, idx):
    m, w = idx.shape[0], table.shape[1]
    workers = SC_CORES * SC_SUBCORES
    per_worker = m // workers
    n_win = per_worker // GATHER_WINDOW
    assert per_worker * workers == m and n_win * GATHER_WINDOW == per_worker and n_win % 2 == 0
    mesh = plsc.VectorSubcoreMesh(core_axis_name="core", subcore_axis_name="subcore")

    @functools.partial(
        pl.kernel, mesh=mesh, out_type=jax.ShapeDtypeStruct((m, w), table.dtype),
        scratch_types=[pltpu.VMEM((per_worker,), jnp.int32), pltpu.VMEM((2, GATHER_WINDOW, w), table.dtype),
                       pltpu.SemaphoreType.DMA((2,)), pltpu.SemaphoreType.DMA((2,))],
        name="gather_rows")
    def gather(table_hbm, idx_hbm, out_hbm, idx_v, rows_v, gather_sems, write_sems):
        base = (lax.axis_index("subcore") * SC_CORES + lax.axis_index("core")) * per_worker
        pltpu.sync_copy(idx_hbm.at[pl.ds(base, per_worker)], idx_v)

        def gather_copy(win, buf):
            rows = idx_v.at[pl.ds(win * GATHER_WINDOW, GATHER_WINDOW)]
            return pltpu.make_async_copy(table_hbm.at[rows], rows_v.at[buf], gather_sems.at[buf])

        def write_copy(win, buf):
            dst = out_hbm.at[pl.ds(base + win * GATHER_WINDOW, GATHER_WINDOW)]
            return pltpu.make_async_copy(rows_v.at[buf], dst, write_sems.at[buf])

        gather_copy(0, 0).start()

        @pl.loop(0, n_win, step=2)
        def _(j):
            for buf in range(2):
                win = j + buf
                gather_copy(win, buf).wait()
                write_copy(win, buf).start()

                @pl.when(win + 1 < n_win)
                def _():
                    @pl.when(win >= 1)
                    def _():
                        write_copy(win - 1, 1 - buf).wait()

                    gather_copy(win + 1, 1 - buf).start()

        write_copy(n_win - 2, 0).wait()
        write_copy(n_win - 1, 1).wait()

    return gather(table, idx)


def _scatter_rows(rows, dests, n_out):
    n, w = rows.shape
    workers = SC_CORES * SC_SUBCORES
    per_worker = n // workers
    n_win = per_worker // GATHER_WINDOW
    n_lists = len(dests)
    assert per_worker * workers == n and n_win * GATHER_WINDOW == per_worker and n_win % 2 == 0
    mesh = plsc.VectorSubcoreMesh(core_axis_name="core", subcore_axis_name="subcore")

    @functools.partial(
        pl.kernel, mesh=mesh, out_type=jax.ShapeDtypeStruct((n_out, w), rows.dtype),
        scratch_types=[pltpu.VMEM((per_worker,), jnp.int32)] * n_lists + [
            pltpu.VMEM((2, GATHER_WINDOW, w), rows.dtype), pltpu.SemaphoreType.DMA((2,)),
            pltpu.SemaphoreType.DMA((2,))],
        name="scatter_rows")
    def scatter(rows_hbm, *refs):
        dest_hbm, out_hbm = refs[:n_lists], refs[n_lists]
        dest_v = refs[n_lists + 1:2 * n_lists + 1]
        rows_v, read_sems, write_sems = refs[2 * n_lists + 1:]
        base = (lax.axis_index("subcore") * SC_CORES + lax.axis_index("core")) * per_worker
        for d_hbm, d_v in zip(dest_hbm, dest_v):
            pltpu.sync_copy(d_hbm.at[pl.ds(base, per_worker)], d_v)

        def read_copy(win, buf):
            src = rows_hbm.at[pl.ds(base + win * GATHER_WINDOW, GATHER_WINDOW)]
            return pltpu.make_async_copy(src, rows_v.at[buf], read_sems.at[buf])

        def write_copies(win, buf):
            return [pltpu.make_async_copy(rows_v.at[buf],
                                          out_hbm.at[d_v.at[pl.ds(win * GATHER_WINDOW, GATHER_WINDOW)]],
                                          write_sems.at[buf]) for d_v in dest_v]

        read_copy(0, 0).start()

        @pl.loop(0, n_win, step=2)
        def _(j):
            for buf in range(2):
                win = j + buf
                read_copy(win, buf).wait()
                for c in write_copies(win, buf):
                    c.start()

                @pl.when(win + 1 < n_win)
                def _():
                    @pl.when(win >= 1)
                    def _():
                        for c in write_copies(win - 1, 1 - buf):
                            c.wait()

                    read_copy(win + 1, 1 - buf).start()

        for c in write_copies(n_win - 2, 0) + write_copies(n_win - 1, 1):
            c.wait()

    return scatter(rows, *dests)


def _experts_kernel(be_ref, bv_ref, xs_ref, wg_ref, wu_ref, wd_ref, o_ref, wg_s, wu_s, wd_s):
    i = pl.program_id(0)
    valid = bv_ref[i] != 0

    @pl.when(jnp.logical_and(valid, jnp.logical_or(i == 0, be_ref[i] != be_ref[jnp.maximum(i - 1, 0)])))
    def _():
        wg_s[...] = wg_ref[0, 0].astype(BF16)
        wu_s[...] = wu_ref[0, 0].astype(BF16)
        wd_s[...] = wd_ref[0, 0].astype(BF16)

    @pl.when(valid)
    def _():
        row = lax.broadcasted_iota(jnp.int32, xs_ref.shape, 0)
        x = _unpack_rows(jnp.where(row < bv_ref[i], xs_ref[...], jnp.uint32(0))).astype(BF16)
        g = _dot(x, wg_s[...])
        u = _dot(x, wu_s[...])
        act = ((g * jax.nn.sigmoid(g)) * u).astype(BF16)
        o_ref[...] = _pack_rows(_dot(act, wd_s[...]).astype(BF16).astype(F32))

    @pl.when(jnp.logical_not(valid))
    def _():
        o_ref[...] = jnp.zeros_like(o_ref)


def _experts(layer, blk_expert, blk_rows, xs, w_gate, w_up, w_down):
    n_slots = xs.shape[0]
    n_blocks = n_slots // EXPERT_TILE
    grid_spec = pltpu.PrefetchScalarGridSpec(
        num_scalar_prefetch=2,
        grid=(n_blocks,),
        in_specs=[pl.BlockSpec((EXPERT_TILE, D_MODEL // 2), lambda i, be, bv: (i, 0)),
                  pl.BlockSpec((1, 1, D_MODEL, D_EXPERT), lambda i, be, bv: (layer, be[i], 0, 0)),
                  pl.BlockSpec((1, 1, D_MODEL, D_EXPERT), lambda i, be, bv: (layer, be[i], 0, 0)),
                  pl.BlockSpec((1, 1, D_EXPERT, D_MODEL), lambda i, be, bv: (layer, be[i], 0, 0))],
        out_specs=pl.BlockSpec((EXPERT_TILE, D_MODEL // 2), lambda i, be, bv: (i, 0)),
        scratch_shapes=[pltpu.VMEM((D_MODEL, D_EXPERT), BF16), pltpu.VMEM((D_MODEL, D_EXPERT), BF16),
                        pltpu.VMEM((D_EXPERT, D_MODEL), BF16)],
    )
    return pl.pallas_call(
        _experts_kernel,
        grid_spec=grid_spec,
        out_shape=jax.ShapeDtypeStruct((n_slots, D_MODEL // 2), jnp.uint32),
        compiler_params=_params(("arbitrary",)),
        name="experts",
    )(blk_expert, blk_rows, xs, w_gate, w_up, w_down)


def _combine_kernel(x_ref, y0_ref, y1_ref, rt_ref, g_ref, o_ref, *, final):
    g0 = rt_ref[:, TOP_K:TOP_K + 1]
    g1 = rt_ref[:, TOP_K + 1:TOP_K + 2]
    x = x_ref[...] + (_unpack_rows(y0_ref[...]) * g0 + _unpack_rows(y1_ref[...]) * g1)
    o_ref[...] = _rms(x, g_ref[...]) if final else x


def _combine(x1, y01, route, g, final):
    n = x1.shape[0]
    rows = min(ROW_TILE, n)
    steps = n // rows
    row = pl.BlockSpec((rows, D_MODEL), lambda i: (i, 0))
    return pl.pallas_call(
        functools.partial(_combine_kernel, final=final),
        grid=(steps,),
        in_specs=[row, pl.BlockSpec((rows, D_MODEL // 2), lambda i: (i, 0)),
                  pl.BlockSpec((rows, D_MODEL // 2), lambda i: (i + steps, 0)),
                  pl.BlockSpec((rows, LANES), lambda i: (i, 0)),
                  pl.BlockSpec((1, D_MODEL), lambda i: (0, 0))],
        out_specs=row,
        out_shape=jax.ShapeDtypeStruct((n, D_MODEL), F32),
        compiler_params=_params(("parallel",)),
        name="combine",
    )(x1, y01, y01, route, g)


def _swap_halves(w):
    half = w.shape[-1] // 2
    return jnp.concatenate([w[..., half:], w[..., :half]], axis=-1)


def _prep_in_weights(w_in):
    d = w_in.shape[0]
    q_lat, kv_lat, kr = w_in[:, 0:256], w_in[:, 256:384], w_in[:, 384:416]
    fq, fk, fv, fl = w_in[:, 416:928], w_in[:, 928:1440], w_in[:, 1440:1952], w_in[:, 1952:1960]
    slab_a = jnp.concatenate([kr, fl, jnp.zeros((d, LANES - MLA_ROPE - FOX_HEADS), F32)], axis=1)
    slab_b = jnp.concatenate([_swap_halves(kr), jnp.zeros((d, LANES - MLA_ROPE), F32)], axis=1)
    return jnp.concatenate([fq * (FOX_DIM ** -0.5), fk, fv, q_lat, kv_lat, slab_a, slab_b], axis=1).astype(BF16)


def _prep_uq(w_uq):
    w = w_uq.reshape(MLA_Q_RANK, MLA_HEADS, MLA_NOPE + MLA_ROPE)
    nope, pe = w[:, :, :MLA_NOPE], w[:, :, MLA_NOPE:]
    pairs = nope.reshape(MLA_Q_RANK, MLA_HEADS // 2, 2, MLA_NOPE)[:, :, ::-1, :].reshape(MLA_Q_RANK, -1)
    return jnp.concatenate([pairs, pe.reshape(MLA_Q_RANK, -1), _swap_halves(pe).reshape(MLA_Q_RANK, -1)],
                           axis=1).astype(BF16)


def _prep_ukv(w_ukv):
    w = w_ukv.reshape(MLA_KV_RANK, MLA_HEADS // 2, 2, 2, MLA_NOPE)
    even = w[:, :, 0, ::-1, :]
    odd = w[:, :, 1, :, :]
    return jnp.stack([even, odd], axis=2).reshape(MLA_KV_RANK, -1).astype(BF16)


def _prep_router(w_rg, b_rg, w_re, b_re):
    d = w_rg.shape[0]
    w = jnp.concatenate([w_rg, w_re, jnp.zeros((d, LANES - N_GROUPS - N_EXPERTS), F32)], axis=1)
    w_hi = w.astype(BF16)
    w_lo = (w - w_hi.astype(F32)).astype(BF16)
    b = jnp.concatenate([b_rg, b_re, jnp.zeros((LANES - N_GROUPS - N_EXPERTS,), F32)])[None, :]
    return jnp.concatenate([w_hi, w_lo], axis=1), b


def _rope_slabs(positions):
    half = MLA_ROPE // 2
    inv_freq = ROPE_THETA ** (-jnp.arange(half, dtype=F32) / half)
    ang = positions.astype(F32).reshape(-1)[:, None] * inv_freq
    cos, sin = jnp.cos(ang), jnp.sin(ang)
    reps = LANES // MLA_ROPE
    return jnp.tile(jnp.concatenate([cos, cos], axis=1), (1, reps)), jnp.tile(
        jnp.concatenate([-sin, sin], axis=1), (1, reps))


def _slot_layout(route, counts, n_blocks):
    eid = route[:, 0:TOP_K].astype(jnp.int32)
    rank = route[:, 2 * TOP_K:3 * TOP_K].astype(jnp.int32)
    counts = counts[0, N_GROUPS:N_GROUPS + N_EXPERTS].astype(jnp.int32)
    padded = (counts + EXPERT_TILE - 1) // EXPERT_TILE * EXPERT_TILE
    pad_end = jnp.cumsum(padded)
    pad_start = pad_end - padded
    experts = jnp.arange(N_EXPERTS, dtype=jnp.int32)
    dest = jnp.sum(jnp.where(eid[:, :, None] == experts, pad_start, 0), axis=-1) + rank
    blk_start = jnp.arange(n_blocks, dtype=jnp.int32) * EXPERT_TILE
    blk_expert = jnp.minimum(jnp.sum((blk_start[:, None] >= pad_end[None, :]).astype(jnp.int32), axis=1),
                             N_EXPERTS - 1)
    in_expert = jnp.sum(jnp.where(blk_expert[:, None] == experts, counts + pad_start, 0), axis=-1) - blk_start
    blk_rows = jnp.where(blk_start < pad_end[-1], jnp.clip(in_expert, 0, EXPERT_TILE), 0).astype(jnp.int32)
    return dest, blk_expert, blk_rows


def kernel(x, positions, attn_norm, w_in, b_forget, q_norm, w_uq, kv_norm, w_ukv, mla_out_norm, fox_out_norm,
           w_out, ffn_norm, w_router_group, b_router_group, w_router_expert, b_router_expert, w_gate, w_up,
           w_down, final_norm):
    batch, seq, d = x.shape
    n = batch * seq
    depth = w_in.shape[0]
    n_blocks = -(-(n * TOP_K) // EXPERT_TILE) + N_EXPERTS
    cc, ss = _rope_slabs(positions)
    tri = (jnp.arange(CUM_TILE)[:, None] <= jnp.arange(CUM_TILE)[None, :]).astype(BF16)
    sub = min(ROW_TILE, n) // SUB_TILES
    tri_rows = (jnp.arange(sub)[None, :] < jnp.arange(sub)[:, None]).astype(BF16)
    xf = x.reshape(n, d)
    for l in range(depth):
        fq, fk, fv, qn, qpe, kv, kpe, flt = _in_proj(
            xf, attn_norm[l][None, :], _prep_in_weights(w_in[l]), q_norm[l][None, :], _prep_uq(w_uq[l]),
            kv_norm[l][None, :], _prep_ukv(w_ukv[l]), cc, ss)
        ck = _fox_decay(flt, b_forget[l][:, None], tri, batch, seq)
        o_mla = _mla_attn(qn, qpe, kv, kpe, batch, seq).reshape(n, HEAD_W)
        o_fox = _fox_attn(fq, fk, fv, ck, batch, seq).reshape(n, HEAD_W)
        wr, br = _prep_router(w_router_group[l], b_router_group[l], w_router_expert[l], b_router_expert[l])
        x1, h2, route, counts = _out_proj(o_mla, o_fox, xf, mla_out_norm[l][None, :], fox_out_norm[l][None, :],
                                          w_out[l].astype(BF16), ffn_norm[l][None, :], wr, br, tri_rows)
        dest, blk_expert, blk_rows = _slot_layout(route, counts, n_blocks)
        dest_t = dest.T
        xs = _scatter_rows(h2, [dest_t[k] for k in range(TOP_K)], n_blocks * EXPERT_TILE)
        ys = _experts(l, blk_expert, blk_rows, xs, w_gate, w_up, w_down)
        y01 = _gather_rows(ys, dest_t.reshape(-1))
        final = l == depth - 1
        xf = _combine(x1, y01, route, final_norm[None, :] if final else ffn_norm[l][None, :], final)
    return xf.reshape(batch, seq, d)
```

```python
import functools

import jax
import jax.numpy as jnp
from jax import lax
from jax.experimental import pallas as pl
from jax.experimental.pallas import tpu as pltpu
from jax.experimental.pallas import tpu_sc as plsc

D_MODEL = 1024
CHUNK = 64
MLA_HEADS = 8
MLA_NOPE = 64
MLA_ROPE = 32
MLA_V = 64
MLA_Q_RANK = 256
MLA_KV_RANK = 128
ROPE_THETA = 10000.0
FOX_HEADS = 8
FOX_DIM = 64
HEAD_W = 512
N_GROUPS = 4
EXPERTS_PER_GROUP = 8
N_EXPERTS = 32
TOP_K = 2
D_EXPERT = 512
NORM_EPS = 1e-6

LANES = 128
IN_COLS = 2176
ROW_TILE = 512
SUB_TILES = 2
Q_TILE = 256
EXPERT_TILE = 512
CUM_TILE = 256
NEG_BIG = -1e30
LOG2E = 1.4426950408889634
VMEM_LIMIT = 48 * 1024 * 1024
SC_CORES = 2
SC_SUBCORES = 16
GATHER_WINDOW = 64
ROUTE_ROWS = 8

F32 = jnp.float32
BF16 = jnp.bfloat16


def _rms(x, g):
    return (x * lax.rsqrt(jnp.mean(x * x, axis=-1, keepdims=True) + NORM_EPS)) * g


def _dot(a, b):
    return jnp.dot(a, b, preferred_element_type=F32)


def _dot_nt(a, b):
    return lax.dot_general(a, b, (((1,), (1,)), ((), ())), preferred_element_type=F32)


def _params(sem):
    return pltpu.CompilerParams(dimension_semantics=sem, vmem_limit_bytes=VMEM_LIMIT)


def _sub_slices(rows):
    sub = rows // SUB_TILES
    return [slice(t * sub, (t + 1) * sub) for t in range(SUB_TILES)]


def _pack_rows(v):
    w = v.shape[1] // 2
    lo = lax.bitcast_convert_type(v[:, :w], jnp.uint32) >> 16
    hi = lax.bitcast_convert_type(v[:, w:], jnp.uint32) & jnp.uint32(0xFFFF0000)
    return hi | lo


def _unpack_rows(p):
    lo = lax.bitcast_convert_type(p << 16, F32)
    hi = lax.bitcast_convert_type(p & jnp.uint32(0xFFFF0000), F32)
    return jnp.concatenate([lo, hi], axis=1)


def _in_proj_kernel(x_ref, g_ref, w_ref, gq_ref, wuq_ref, gkv_ref, wukv_ref, cc_ref, ss_ref,
                    fq_ref, fk_ref, fv_ref, qn_ref, qpe_ref, kv_ref, kpe_ref, flt_ref):
    subs = _sub_slices(x_ref.shape[0])
    ps = [_dot(_rms(x_ref[sl, :], g_ref[...]).astype(BF16), w_ref[...]) for sl in subs]
    scale = (MLA_NOPE + MLA_ROPE) ** -0.5 * LOG2E
    for sl, p in zip(subs, ps):
        fq_ref[sl, :] = (p[:, 0:512] * LOG2E).astype(BF16)
        fk_ref[sl, :] = p[:, 512:1024].astype(BF16)
        fv_ref[sl, :] = p[:, 1024:1536].astype(BF16)

        cc = cc_ref[sl, :]
        ss = ss_ref[sl, :]
        q = _dot(_rms(p[:, 1536:1792], gq_ref[...]).astype(BF16), wuq_ref[...])
        qn_ref[sl, :] = (q[:, 0:512] * scale).astype(BF16)
        cc2 = jnp.concatenate([cc, cc], axis=1)
        ss2 = jnp.concatenate([ss, ss], axis=1)
        qpe_ref[sl, :] = ((q[:, 512:768] * cc2 + q[:, 768:1024] * ss2) * scale).astype(BF16)

        kv_ref[sl, :] = _dot(_rms(p[:, 1792:1920], gkv_ref[...]).astype(BF16), wukv_ref[...]).astype(BF16)

        slab_a = p[:, 1920:2048]
        slab_b = p[:, 2048:2176]
        lane = lax.broadcasted_iota(jnp.int32, slab_a.shape, 1)
        roped = jnp.where(lane < MLA_ROPE, slab_a * cc + slab_b * ss, 0.0)
        tiled = roped + pltpu.roll(roped, 32, 1) + pltpu.roll(roped, 64, 1) + pltpu.roll(roped, 96, 1)
        kpe_ref[sl, :] = tiled.astype(BF16)
        flt_ref[:, sl] = slab_a.T[MLA_ROPE:MLA_ROPE + FOX_HEADS, :]


def _in_proj(x2d, g, w, gq, wuq, gkv, wukv, cc, ss):
    n = x2d.shape[0]
    rows = min(ROW_TILE, n)
    row = lambda c: pl.BlockSpec((rows, c), lambda i: (i, 0))
    full = lambda a: pl.BlockSpec(a.shape, lambda i: (0,) * a.ndim)
    out_shape = (
        jax.ShapeDtypeStruct((n, HEAD_W), BF16), jax.ShapeDtypeStruct((n, HEAD_W), BF16),
        jax.ShapeDtypeStruct((n, HEAD_W), BF16), jax.ShapeDtypeStruct((n, HEAD_W), BF16),
        jax.ShapeDtypeStruct((n, 256), BF16), jax.ShapeDtypeStruct((n, 1024), BF16),
        jax.ShapeDtypeStruct((n, LANES), BF16), jax.ShapeDtypeStruct((FOX_HEADS, n), F32),
    )
    return pl.pallas_call(
        _in_proj_kernel,
        grid=(n // rows,),
        in_specs=[row(D_MODEL), full(g), full(w), full(gq), full(wuq), full(gkv), full(wukv),
                  row(LANES), row(LANES)],
        out_specs=(row(HEAD_W), row(HEAD_W), row(HEAD_W), row(HEAD_W), row(256), row(1024), row(LANES),
                   pl.BlockSpec((FOX_HEADS, rows), lambda i: (0, i))),
        out_shape=out_shape,
        compiler_params=_params(("parallel",)),
        name="in_proj",
    )(x2d, g, w, gq, wuq, gkv, wukv, cc, ss)


def _fox_decay_kernel(fl_ref, b_ref, tri_ref, ck_ref):
    z = fl_ref[...] + b_ref[...]
    lf = jnp.minimum(z, 0.0) - jnp.log1p(jnp.exp(-jnp.abs(z)))
    seq = lf.shape[1]
    tri = tri_ref[...]
    carry = jnp.zeros((FOX_HEADS, 1), F32)
    zeros = jnp.zeros((FOX_HEADS, CUM_TILE), F32)
    for j in range(seq // CUM_TILE):
        v = lf[:, j * CUM_TILE:(j + 1) * CUM_TILE]
        hi = v.astype(BF16).astype(F32)
        r1 = v - hi
        mid = r1.astype(BF16).astype(F32)
        lo = r1 - mid
        parts = _dot(jnp.concatenate([hi, mid, lo, zeros], axis=0).astype(BF16), tri)
        cs = (parts[0:8] + parts[8:16]) + parts[16:24] + carry
        carry = cs[:, CUM_TILE - 1:CUM_TILE]
        d = cs * (-LOG2E)
        d_hi = d.astype(BF16).astype(F32)
        d_r = d - d_hi
        d_mid = d_r.astype(BF16).astype(F32)
        d_lo = d_r - d_mid
        rows = jnp.concatenate([d_hi, d_mid, d_lo, jnp.zeros((LANES - 3 * FOX_HEADS, CUM_TILE), F32)], axis=0)
        ck_ref[0, j * CUM_TILE:(j + 1) * CUM_TILE, :] = rows.T.astype(BF16)


def _fox_decay(flt, b_col, tri, batch, seq):
    return pl.pallas_call(
        _fox_decay_kernel,
        grid=(batch,),
        in_specs=[pl.BlockSpec((FOX_HEADS, seq), lambda b: (0, b)),
                  pl.BlockSpec((FOX_HEADS, 1), lambda b: (0, 0)),
                  pl.BlockSpec((CUM_TILE, CUM_TILE), lambda b: (0, 0))],
        out_specs=pl.BlockSpec((1, seq, LANES), lambda b: (b, 0, 0)),
        out_shape=jax.ShapeDtypeStruct((batch, seq, LANES), BF16),
        compiler_params=_params(("parallel",)),
        name="fox_decay",
    )(flt, b_col, tri)


def _probs(s_off, s_diag):
    m = jnp.max(s_diag, axis=-1, keepdims=True)
    if s_off is not None:
        m = jnp.maximum(m, jnp.max(s_off, axis=-1, keepdims=True))
    p_off = None if s_off is None else jnp.exp2(s_off - m).astype(BF16)
    return p_off, jnp.exp2(s_diag - m).astype(BF16)


def _attend(q_ref, k_ref, v_ref, o_ref, allowed):
    seq = q_ref.shape[1]
    out_lane = lax.broadcasted_iota(jnp.int32, (Q_TILE, LANES), 1)
    units = [(i, hh) for i in range(seq // Q_TILE) for hh in range(2)]

    def scores(i, hh):
        qs, qe = i * Q_TILE, (i + 1) * Q_TILE
        q = q_ref[hh, qs:qe, :]
        s_diag = jnp.where(allowed, _dot_nt(q, k_ref[hh, qs:qe, :]), NEG_BIG)
        s_off = _dot_nt(q, k_ref[hh, 0:qs, :]) if i > 0 else None
        return s_off, s_diag

    def values(i, hh, p_off, p_diag):
        qs, qe = i * Q_TILE, (i + 1) * Q_TILE
        acc = _dot(p_diag, v_ref[hh, qs:qe, :])
        if p_off is not None:
            acc = acc + _dot(p_off, v_ref[hh, 0:qs, :])
        return acc[:, 0:LANES] / acc[:, LANES:2 * LANES]

    n_units = len(units)
    s = {0: scores(*units[0])}
    if n_units > 1:
        s[1] = scores(*units[1])
    p = {0: _probs(*s.pop(0))}
    outs = []
    for n, (i, hh) in enumerate(units):
        if n + 2 < n_units:
            s[n + 2] = scores(*units[n + 2])
        if n + 1 < n_units:
            p[n + 1] = _probs(*s.pop(n + 1))
        outs.append(values(i, hh, *p.pop(n)))
        if hh == 1:
            o_ref[0, i * Q_TILE:(i + 1) * Q_TILE, :] = jnp.where(out_lane < 64, outs[0], outs[1]).astype(BF16)
            outs = []


def _tile_iota():
    row = lax.broadcasted_iota(jnp.int32, (Q_TILE, Q_TILE), 0)
    col = lax.broadcasted_iota(jnp.int32, (Q_TILE, Q_TILE), 1)
    return row, col


def _mla_attn_kernel(qn_ref, qpe_ref, kv_ref, kpe_ref, o_ref, qs_ref, ks_ref, vs_ref):
    j = pl.program_id(1)
    seq = qn_ref.shape[1]
    lane = lax.broadcasted_iota(jnp.int32, (seq, LANES), 1)
    qn = qn_ref[0]
    qpe = qpe_ref[0]
    kpe = kpe_ref[0]
    ones = jnp.ones((seq, LANES), BF16)
    for hh in range(2):
        nope_mask = (lane >= 64) if hh == 0 else (lane < 64)
        pe_slot = 2 * (j % 2) + hh
        pe_mask = (lane // MLA_ROPE) == pe_slot
        qs_ref[hh, :, 0:LANES] = jnp.where(nope_mask, qn, jnp.zeros_like(qn))
        qs_ref[hh, :, LANES:2 * LANES] = jnp.where(pe_mask, qpe, jnp.zeros_like(qpe))
        kvh = kv_ref[0, :, hh * LANES:(hh + 1) * LANES]
        ks_ref[hh, :, 0:LANES] = kvh
        ks_ref[hh, :, LANES:2 * LANES] = kpe
        vs_ref[hh, :, 0:LANES] = kvh
        vs_ref[hh, :, LANES:2 * LANES] = ones
    row, col = _tile_iota()
    _attend(qs_ref, ks_ref, vs_ref, o_ref, (col // CHUNK) <= (row // CHUNK))


def _mla_attn(qn, qpe, kv, kpe, batch, seq):
    qn, qpe, kv, kpe = (a.reshape(batch, seq, a.shape[-1]) for a in (qn, qpe, kv, kpe))
    pair = pltpu.VMEM((2, seq, 2 * LANES), BF16)
    return pl.pallas_call(
        _mla_attn_kernel,
        grid=(batch, MLA_HEADS // 2),
        in_specs=[pl.BlockSpec((1, seq, LANES), lambda b, j: (b, 0, j)),
                  pl.BlockSpec((1, seq, LANES), lambda b, j: (b, 0, j // 2)),
                  pl.BlockSpec((1, seq, 2 * LANES), lambda b, j: (b, 0, j)),
                  pl.BlockSpec((1, seq, LANES), lambda b, j: (b, 0, 0))],
        out_specs=pl.BlockSpec((1, seq, LANES), lambda b, j: (b, 0, j)),
        out_shape=jax.ShapeDtypeStruct((batch, seq, HEAD_W), BF16),
        scratch_shapes=[pair, pair, pair],
        compiler_params=_params(("parallel", "parallel")),
        name="mla_attn",
    )(qn, qpe, kv, kpe)


def _fox_attn_kernel(q_ref, k_ref, v_ref, ck_ref, o_ref, qs_ref, ks_ref, vs_ref):
    j = pl.program_id(1)
    seq = q_ref.shape[1]
    lane = lax.broadcasted_iota(jnp.int32, (seq, LANES), 1)
    q = q_ref[0]
    k = k_ref[0]
    v = v_ref[0]
    ck = ck_ref[0]
    ones = jnp.ones((seq, LANES), BF16)
    for hh in range(2):
        head = 2 * j + hh
        head_mask = (lane < 64) if hh == 0 else (lane >= 64)
        piece = jnp.where(lane < 3 * FOX_HEADS, lane % FOX_HEADS, -1) == head
        qs_ref[hh, :, 0:LANES] = jnp.where(head_mask, q, jnp.zeros_like(q))
        qs_ref[hh, :, LANES:2 * LANES] = jnp.where(piece, 1.0, 0.0).astype(BF16)
        ks_ref[hh, :, 0:LANES] = k
        ks_ref[hh, :, LANES:2 * LANES] = ck
        vs_ref[hh, :, 0:LANES] = v
        vs_ref[hh, :, LANES:2 * LANES] = ones
    row, col = _tile_iota()
    _attend(qs_ref, ks_ref, vs_ref, o_ref, col <= row)


def _fox_attn(fq, fk, fv, ck, batch, seq):
    fq, fk, fv = (a.reshape(batch, seq, HEAD_W) for a in (fq, fk, fv))
    spec = pl.BlockSpec((1, seq, LANES), lambda b, j: (b, 0, j))
    pair = pltpu.VMEM((2, seq, 2 * LANES), BF16)
    return pl.pallas_call(
        _fox_attn_kernel,
        grid=(batch, FOX_HEADS // 2),
        in_specs=[spec, spec, spec, pl.BlockSpec((1, seq, LANES), lambda b, j: (b, 0, 0))],
        out_specs=spec,
        out_shape=jax.ShapeDtypeStruct((batch, seq, HEAD_W), BF16),
        scratch_shapes=[pair, pair, pair],
        compiler_params=_params(("parallel", "parallel")),
        name="fox_attn",
    )(fq, fk, fv, ck)


def _lane_max(v):
    return jnp.max(v, axis=1, keepdims=True)


def _first_lane(hit, lane_f):
    return jnp.min(jnp.where(hit, lane_f, float(LANES)), axis=1, keepdims=True)


def _out_proj_kernel(om_ref, of_ref, x_ref, gm_ref, gf_ref, wo_ref, gn_ref, wr_ref, br_ref, tri_ref,
                     x1_ref, h2_ref, rt_ref, rtt_ref, cnt_ref, carry_ref):
    @pl.when(pl.program_id(0) == 0)
    def _():
        carry_ref[...] = jnp.zeros_like(carry_ref)

    subs = _sub_slices(x_ref.shape[0])
    logits = [_project_out(sl, om_ref, of_ref, x_ref, gm_ref, gf_ref, wo_ref, gn_ref, wr_ref, br_ref, x1_ref, h2_ref)
              for sl in subs]
    carry = carry_ref[...]
    for sl, lg in zip(subs, logits):
        rt, carry = _route_rows(lg, tri_ref[...], carry)
        rt_ref[sl, :] = rt
        rtt_ref[:, sl] = rt.T[0:ROUTE_ROWS, :]
    carry_ref[...] = carry
    cnt_ref[...] = carry


def _project_out(sl, om_ref, of_ref, x_ref, gm_ref, gf_ref, wo_ref, gn_ref, wr_ref, br_ref, x1_ref, h2_ref):
    a = _rms(om_ref[sl, :].astype(F32), gm_ref[...]).astype(BF16)
    b = _rms(of_ref[sl, :].astype(F32), gf_ref[...]).astype(BF16)
    x1 = x_ref[sl, :] + _dot(a, wo_ref[0:HEAD_W, :]) + _dot(b, wo_ref[HEAD_W:2 * HEAD_W, :])
    x1_ref[sl, :] = x1
    h2 = _rms(x1, gn_ref[...])
    h_hi = h2.astype(BF16)
    h2_ref[sl, :] = _pack_rows(h_hi.astype(F32))
    h_lo = (h2 - h_hi.astype(F32)).astype(BF16)
    t = _dot(h_hi, wr_ref[...])
    return (t[:, 0:LANES] + t[:, LANES:2 * LANES]) + _dot(h_lo, wr_ref[:, 0:LANES]) + br_ref[...]


def _route_rows(lg, tri, carry):
    lane = lax.broadcasted_iota(jnp.int32, lg.shape, 1)
    lane_f = lane.astype(F32)
    neg_inf = float("-inf")
    is_group = lane < N_GROUPS
    gl = jnp.where(is_group, lg, neg_inf)
    mg = _lane_max(gl)
    gi = _first_lane(gl == mg, lane_f)
    g_val = 1.0 / jnp.sum(jnp.where(is_group, jnp.exp(lg - mg), 0.0), axis=1, keepdims=True)
    group_of_lane = ((lane - N_GROUPS) >> 3).astype(F32)
    is_expert = (lane >= N_GROUPS) & (lane < N_GROUPS + N_EXPERTS) & (group_of_lane == gi)
    el = jnp.where(is_expert, lg, neg_inf)
    m1 = _lane_max(el)
    i1 = _first_lane(el == m1, lane_f)
    el2 = jnp.where(lane_f == i1, neg_inf, el)
    m2 = _lane_max(el2)
    i2 = _first_lane(el2 == m2, lane_f)
    r = jnp.exp(m2 - m1)
    g0 = g_val / (1.0 + r)
    g1 = g0 * r

    hit1 = lane_f == i1
    hit2 = lane_f == i2
    onehot = jnp.where(hit1 | hit2, 1.0, 0.0)
    before = _dot(tri, onehot.astype(BF16)) + carry
    r0 = jnp.sum(jnp.where(hit1, before, 0.0), axis=1, keepdims=True)
    r1 = jnp.sum(jnp.where(hit2, before, 0.0), axis=1, keepdims=True)

    vals = (i1 - N_GROUPS, i2 - N_GROUPS, g0, g1, r0, r1)
    rt = jnp.zeros_like(lg)
    for k, v in enumerate(vals):
        rt = jnp.where(lane == k, v, rt)
    return rt, carry + jnp.sum(onehot, axis=0, keepdims=True)


def _out_proj(om, of, x2d, gm, gf, wo, gn, wr, br, tri):
    n = x2d.shape[0]
    rows = tri.shape[0] * SUB_TILES
    row = lambda c: pl.BlockSpec((rows, c), lambda i: (i, 0))
    full = lambda a: pl.BlockSpec(a.shape, lambda i: (0,) * a.ndim)
    return pl.pallas_call(
        _out_proj_kernel,
        grid=(n // rows,),
        in_specs=[row(HEAD_W), row(HEAD_W), row(D_MODEL), full(gm), full(gf), full(wo), full(gn), full(wr),
                  full(br), full(tri)],
        out_specs=(row(D_MODEL), row(D_MODEL // 2), row(LANES), pl.BlockSpec((ROUTE_ROWS, rows), lambda i: (0, i)),
                   pl.BlockSpec((1, LANES), lambda i: (0, 0))),
        out_shape=(jax.ShapeDtypeStruct((n, D_MODEL), F32), jax.ShapeDtypeStruct((n, D_MODEL // 2), jnp.uint32),
                   jax.ShapeDtypeStruct((n, LANES), F32), jax.ShapeDtypeStruct((ROUTE_ROWS, n), F32),
                   jax.ShapeDtypeStruct((1, LANES), F32)),
        scratch_shapes=[pltpu.VMEM((1, LANES), F32)],
        compiler_params=_params(("arbitrary",)),
        name="out_proj",
    )(om, of, x2d, gm, gf, wo, gn, wr, br, tri)


def _gather_rows(table, idx):
    m, w = idx.shape[0], table.shape[1]
    workers = SC_CORES * SC_SUBCORES
    per_worker = m // workers
    n_win = per_worker // GATHER_WINDOW
    assert per_worker * workers == m and n_win * GATHER_WINDOW == per_worker and n_win % 2 == 0
    mesh = plsc.VectorSubcoreMesh(core_axis_name="core", subcore_axis_name="subcore")

    @functools.partial(
        pl.kernel, mesh=mesh, out_type=jax.ShapeDtypeStruct((m, w), table.dtype),
        scratch_types=[pltpu.VMEM((per_worker,), jnp.int32), pltpu.VMEM((2, GATHER_WINDOW, w), table.dtype),
                       pltpu.SemaphoreType.DMA((2,)), pltpu.SemaphoreType.DMA((2,))],
        name="gather_rows")
    def gather(table_hbm, idx_hbm, out_hbm, idx_v, rows_v, gather_sems, write_sems):
        base = (lax.axis_index("subcore") * SC_CORES + lax.axis_index("core")) * per_worker
        pltpu.sync_copy(idx_hbm.at[pl.ds(base, per_worker)], idx_v)

        def gather_copy(win, buf):
            rows = idx_v.at[pl.ds(win * GATHER_WINDOW, GATHER_WINDOW)]
            return pltpu.make_async_copy(table_hbm.at[rows], rows_v.at[buf], gather_sems.at[buf])

        def write_copy(win, buf):
            dst = out_hbm.at[pl.ds(base + win * GATHER_WINDOW, GATHER_WINDOW)]
            return pltpu.make_async_copy(rows_v.at[buf], dst, write_sems.at[buf])

        gather_copy(0, 0).start()

        @pl.loop(0, n_win, step=2)
        def _(j):
            for buf in range(2):
                win = j + buf
                gather_copy(win, buf).wait()
                write_copy(win, buf).start()

                @pl.when(win + 1 < n_win)
                def _():
                    @pl.when(win >= 1)
                    def _():
                        write_copy(win - 1, 1 - buf).wait()

                    gather_copy(win + 1, 1 - buf).start()

        write_copy(n_win - 2, 0).wait()
        write_copy(n_win - 1, 1).wait()

    return gather(table, idx)


def _scatter_rows(rows, dests, n_out):
    n, w = rows.shape
    workers = SC_CORES * SC_SUBCORES
    per_worker = n // workers
    n_win = per_worker // GATHER_WINDOW
    n_lists = len(dests)
    assert per_worker * workers == n and n_win * GATHER_WINDOW == per_worker and n_win % 2 == 0
    mesh = plsc.VectorSubcoreMesh(core_axis_name="core", subcore_axis_name="subcore")

    @functools.partial(
        pl.kernel, mesh=mesh, out_type=jax.ShapeDtypeStruct((n_out, w), rows.dtype),
        scratch_types=[pltpu.VMEM((per_worker,), jnp.int32)] * n_lists + [
            pltpu.VMEM((2, GATHER_WINDOW, w), rows.dtype), pltpu.SemaphoreType.DMA((2,)),
            pltpu.SemaphoreType.DMA((2,))],
        name="scatter_rows")
    def scatter(rows_hbm, *refs):
        dest_hbm, out_hbm = refs[:n_lists], refs[n_lists]
        dest_v = refs[n_lists + 1:2 * n_lists + 1]
        rows_v, read_sems, write_sems = refs[2 * n_lists + 1:]
        base = (lax.axis_index("subcore") * SC_CORES + lax.axis_index("core")) * per_worker
        for d_hbm, d_v in zip(dest_hbm, dest_v):
            pltpu.sync_copy(d_hbm.at[pl.ds(base, per_worker)], d_v)

        def read_copy(win, buf):
            src = rows_hbm.at[pl.ds(base + win * GATHER_WINDOW, GATHER_WINDOW)]
            return pltpu.make_async_copy(src, rows_v.at[buf], read_sems.at[buf])

        def write_copies(win, buf):
            return [pltpu.make_async_copy(rows_v.at[buf],
                                          out_hbm.at[d_v.at[pl.ds(win * GATHER_WINDOW, GATHER_WINDOW)]],
                                          write_sems.at[buf]) for d_v in dest_v]

        read_copy(0, 0).start()

        @pl.loop(0, n_win, step=2)
        def _(j):
            for buf in range(2):
                win = j + buf
                read_copy(win, buf).wait()
                for c in write_copies(win, buf):
                    c.start()

                @pl.when(win + 1 < n_win)
                def _():
                    @pl.when(win >= 1)
                    def _():
                        for c in write_copies(win - 1, 1 - buf):
                            c.wait()

                    read_copy(win + 1, 1 - buf).start()

        for c in write_copies(n_win - 2, 0) + write_copies(n_win - 1, 1):
            c.wait()

    return scatter(rows, *dests)


def _experts_kernel(be_ref, bv_ref, xs_ref, wg_ref, wu_ref, wd_ref, o_ref, wg_s, wu_s, wd_s):
    i = pl.program_id(0)
    valid = bv_ref[i] != 0

    @pl.when(jnp.logical_and(valid, jnp.logical_or(i == 0, be_ref[i] != be_ref[jnp.maximum(i - 1, 0)])))
    def _():
        wg_s[...] = wg_ref[0, 0].astype(BF16)
        wu_s[...] = wu_ref[0, 0].astype(BF16)
        wd_s[...] = wd_ref[0, 0].astype(BF16)

    @pl.when(valid)
    def _():
        row = lax.broadcasted_iota(jnp.int32, xs_ref.shape, 0)
        x = _unpack_rows(jnp.where(row < bv_ref[i], xs_ref[...], jnp.uint32(0))).astype(BF16)
        g = _dot(x, wg_s[...])
        u = _dot(x, wu_s[...])
        act = ((g * jax.nn.sigmoid(g)) * u).astype(BF16)
        o_ref[...] = _pack_rows(_dot(act, wd_s[...]).astype(BF16).astype(F32))

    @pl.when(jnp.logical_not(valid))
    def _():
        o_ref[...] = jnp.zeros_like(o_ref)


def _experts(layer, blk_expert, blk_rows, xs, w_gate, w_up, w_down):
    n_slots = xs.shape[0]
    n_blocks = n_slots // EXPERT_TILE
    grid_spec = pltpu.PrefetchScalarGridSpec(
        num_scalar_prefetch=2,
        grid=(n_blocks,),
        in_specs=[pl.BlockSpec((EXPERT_TILE, D_MODEL // 2), lambda i, be, bv: (i, 0)),
                  pl.BlockSpec((1, 1, D_MODEL, D_EXPERT), lambda i, be, bv: (layer, be[i], 0, 0)),
                  pl.BlockSpec((1, 1, D_MODEL, D_EXPERT), lambda i, be, bv: (layer, be[i], 0, 0)),
                  pl.BlockSpec((1, 1, D_EXPERT, D_MODEL), lambda i, be, bv: (layer, be[i], 0, 0))],
        out_specs=pl.BlockSpec((EXPERT_TILE, D_MODEL // 2), lambda i, be, bv: (i, 0)),
        scratch_shapes=[pltpu.VMEM((D_MODEL, D_EXPERT), BF16), pltpu.VMEM((D_MODEL, D_EXPERT), BF16),
                        pltpu.VMEM((D_EXPERT, D_MODEL), BF16)],
    )
    return pl.pallas_call(
        _experts_kernel,
        grid_spec=grid_spec,
        out_shape=jax.ShapeDtypeStruct((n_slots, D_MODEL // 2), jnp.uint32),
        compiler_params=_params(("arbitrary",)),
        name="experts",
    )(blk_expert, blk_rows, xs, w_gate, w_up, w_down)


def _combine_kernel(x_ref, y0_ref, y1_ref, rt_ref, g_ref, o_ref, *, final):
    g0 = rt_ref[:, TOP_K:TOP_K + 1]
    g1 = rt_ref[:, TOP_K + 1:TOP_K + 2]
    x = x_ref[...] + (_unpack_rows(y0_ref[...]) * g0 + _unpack_rows(y1_ref[...]) * g1)
    o_ref[...] = _rms(x, g_ref[...]) if final else x


def _combine(x1, y01, route, g, final):
    n = x1.shape[0]
    rows = min(ROW_TILE, n)
    steps = n // rows
    row = pl.BlockSpec((rows, D_MODEL), lambda i: (i, 0))
    return pl.pallas_call(
        functools.partial(_combine_kernel, final=final),
        grid=(steps,),
        in_specs=[row, pl.BlockSpec((rows, D_MODEL // 2), lambda i: (i, 0)),
                  pl.BlockSpec((rows, D_MODEL // 2), lambda i: (i + steps, 0)),
                  pl.BlockSpec((rows, LANES), lambda i: (i, 0)),
                  pl.BlockSpec((1, D_MODEL), lambda i: (0, 0))],
        out_specs=row,
        out_shape=jax.ShapeDtypeStruct((n, D_MODEL), F32),
        compiler_params=_params(("parallel",)),
        name="combine",
    )(x1, y01, y01, route, g)


def _swap_halves(w):
    half = w.shape[-1] // 2
    return jnp.concatenate([w[..., half:], w[..., :half]], axis=-1)


def _prep_in_weights(w_in):
    d = w_in.shape[0]
    w_in = w_in.astype(BF16)
    q_lat, kv_lat, kr = w_in[:, 0:256], w_in[:, 256:384], w_in[:, 384:416]
    fq, fk, fv, fl = w_in[:, 416:928], w_in[:, 928:1440], w_in[:, 1440:1952], w_in[:, 1952:1960]
    slab_a = jnp.concatenate([kr, fl, jnp.zeros((d, LANES - MLA_ROPE - FOX_HEADS), BF16)], axis=1)
    slab_b = jnp.concatenate([_swap_halves(kr), jnp.zeros((d, LANES - MLA_ROPE), BF16)], axis=1)
    return jnp.concatenate([fq * (FOX_DIM ** -0.5), fk, fv, q_lat, kv_lat, slab_a, slab_b], axis=1)


def _prep_uq(w_uq):
    w = w_uq.reshape(MLA_Q_RANK, MLA_HEADS, MLA_NOPE + MLA_ROPE)
    nope, pe = w[:, :, :MLA_NOPE], w[:, :, MLA_NOPE:]
    pairs = nope.reshape(MLA_Q_RANK, MLA_HEADS // 2, 2, MLA_NOPE)[:, :, ::-1, :].reshape(MLA_Q_RANK, -1)
    return jnp.concatenate([pairs, pe.reshape(MLA_Q_RANK, -1), _swap_halves(pe).reshape(MLA_Q_RANK, -1)],
                           axis=1).astype(BF16)


def _prep_ukv(w_ukv):
    w = w_ukv.reshape(MLA_KV_RANK, MLA_HEADS // 2, 2, 2, MLA_NOPE)
    even = w[:, :, 0, ::-1, :]
    odd = w[:, :, 1, :, :]
    return jnp.stack([even, odd], axis=2).reshape(MLA_KV_RANK, -1).astype(BF16)


def _prep_router(w_rg, b_rg, w_re, b_re):
    d = w_rg.shape[0]
    w = jnp.concatenate([w_rg, w_re, jnp.zeros((d, LANES - N_GROUPS - N_EXPERTS), F32)], axis=1)
    w_hi = w.astype(BF16)
    w_lo = (w - w_hi.astype(F32)).astype(BF16)
    b = jnp.concatenate([b_rg, b_re, jnp.zeros((LANES - N_GROUPS - N_EXPERTS,), F32)])[None, :]
    return jnp.concatenate([w_hi, w_lo], axis=1), b


def _rope_slabs(positions):
    half = MLA_ROPE // 2
    inv_freq = ROPE_THETA ** (-jnp.arange(half, dtype=F32) / half)
    ang = positions.astype(F32).reshape(-1)[:, None] * inv_freq
    cos, sin = jnp.cos(ang), jnp.sin(ang)
    reps = LANES // MLA_ROPE
    return jnp.tile(jnp.concatenate([cos, cos], axis=1), (1, reps)), jnp.tile(
        jnp.concatenate([-sin, sin], axis=1), (1, reps))


def _slot_layout(route_t, counts, n_blocks):
    eid = route_t[0:TOP_K].astype(jnp.int32)
    rank = route_t[2 * TOP_K:3 * TOP_K].astype(jnp.int32)
    counts = counts[0, N_GROUPS:N_GROUPS + N_EXPERTS].astype(jnp.int32)
    padded = (counts + EXPERT_TILE - 1) // EXPERT_TILE * EXPERT_TILE
    pad_end = jnp.cumsum(padded)
    pad_start = pad_end - padded
    experts = jnp.arange(N_EXPERTS, dtype=jnp.int32)
    dest = jnp.sum(jnp.where(eid[None] == experts[:, None, None], pad_start[:, None, None], 0), axis=0) + rank
    blk_start = jnp.arange(n_blocks, dtype=jnp.int32) * EXPERT_TILE
    blk_expert = jnp.minimum(jnp.sum((blk_start[:, None] >= pad_end[None, :]).astype(jnp.int32), axis=1),
                             N_EXPERTS - 1)
    in_expert = jnp.sum(jnp.where(blk_expert[:, None] == experts, counts + pad_start, 0), axis=-1) - blk_start
    blk_rows = jnp.where(blk_start < pad_end[-1], jnp.clip(in_expert, 0, EXPERT_TILE), 0).astype(jnp.int32)
    return dest, blk_expert, blk_rows


def kernel(x, positions, attn_norm, w_in, b_forget, q_norm, w_uq, kv_norm, w_ukv, mla_out_norm, fox_out_norm,
           w_out, ffn_norm, w_router_group, b_router_group, w_router_expert, b_router_expert, w_gate, w_up,
           w_down, final_norm):
    batch, seq, d = x.shape
    n = batch * seq
    depth = w_in.shape[0]
    n_blocks = -(-(n * TOP_K) // EXPERT_TILE) + N_EXPERTS
    cc, ss = _rope_slabs(positions)
    tri = (jnp.arange(CUM_TILE)[:, None] <= jnp.arange(CUM_TILE)[None, :]).astype(BF16)
    sub = min(ROW_TILE, n) // SUB_TILES
    tri_rows = (jnp.arange(sub)[None, :] < jnp.arange(sub)[:, None]).astype(BF16)
    xf = x.reshape(n, d)
    for l in range(depth):
        fq, fk, fv, qn, qpe, kv, kpe, flt = _in_proj(
            xf, attn_norm[l][None, :], _prep_in_weights(w_in[l]), q_norm[l][None, :], _prep_uq(w_uq[l]),
            kv_norm[l][None, :], _prep_ukv(w_ukv[l]), cc, ss)
        ck = _fox_decay(flt, b_forget[l][:, None], tri, batch, seq)
        o_mla = _mla_attn(qn, qpe, kv, kpe, batch, seq).reshape(n, HEAD_W)
        o_fox = _fox_attn(fq, fk, fv, ck, batch, seq).reshape(n, HEAD_W)
        wr, br = _prep_router(w_router_group[l], b_router_group[l], w_router_expert[l], b_router_expert[l])
        x1, h2, route, route_t, counts = _out_proj(
            o_mla, o_fox, xf, mla_out_norm[l][None, :], fox_out_norm[l][None, :], w_out[l].astype(BF16),
            ffn_norm[l][None, :], wr, br, tri_rows)
        dest, blk_expert, blk_rows = _slot_layout(route_t, counts, n_blocks)
        xs = _scatter_rows(h2, [dest[k] for k in range(TOP_K)], n_blocks * EXPERT_TILE)
        ys = _experts(l, blk_expert, blk_rows, xs, w_gate, w_up, w_down)
        y01 = _gather_rows(ys, dest.reshape(-1))
        final = l == depth - 1
        xf = _combine(x1, y01, route, final_norm[None, :] if final else ffn_norm[l][None, :], final)
    return xf.reshape(batch, seq, d)
```

```python
import functools

import jax
import jax.numpy as jnp
from jax import lax
from jax.experimental import pallas as pl
from jax.experimental.pallas import tpu as pltpu
from jax.experimental.pallas import tpu_sc as plsc

D_MODEL = 1024
CHUNK = 64
MLA_HEADS = 8
MLA_NOPE = 64
MLA_ROPE = 32
MLA_V = 64
MLA_Q_RANK = 256
MLA_KV_RANK = 128
ROPE_THETA = 10000.0
FOX_HEADS = 8
FOX_DIM = 64
HEAD_W = 512
N_GROUPS = 4
EXPERTS_PER_GROUP = 8
N_EXPERTS = 32
TOP_K = 2
D_EXPERT = 512
NORM_EPS = 1e-6

LANES = 128
IN_COLS = 2176
ROW_TILE = 512
SUB_TILES = 2
Q_TILE = 256
EXPERT_TILE = 512
CUM_TILE = 256
NEG_BIG = -1e30
LOG2E = 1.4426950408889634
VMEM_LIMIT = 48 * 1024 * 1024
SC_CORES = 2
SC_SUBCORES = 16
GATHER_WINDOW = 64
ROUTE_ROWS = 8

F32 = jnp.float32
BF16 = jnp.bfloat16


def _rms(x, g):
    return (x * lax.rsqrt(jnp.mean(x * x, axis=-1, keepdims=True) + NORM_EPS)) * g


def _dot(a, b):
    return jnp.dot(a, b, preferred_element_type=F32)


def _dot_nt(a, b):
    return lax.dot_general(a, b, (((1,), (1,)), ((), ())), preferred_element_type=F32)


def _params(sem):
    return pltpu.CompilerParams(dimension_semantics=sem, vmem_limit_bytes=VMEM_LIMIT)


def _sub_slices(rows):
    sub = rows // SUB_TILES
    return [slice(t * sub, (t + 1) * sub) for t in range(SUB_TILES)]


def _pack_rows(v):
    w = v.shape[1] // 2
    lo = lax.bitcast_convert_type(v[:, :w], jnp.uint32) >> 16
    hi = lax.bitcast_convert_type(v[:, w:], jnp.uint32) & jnp.uint32(0xFFFF0000)
    return hi | lo


def _unpack_rows(p):
    lo = lax.bitcast_convert_type(p << 16, F32)
    hi = lax.bitcast_convert_type(p & jnp.uint32(0xFFFF0000), F32)
    return jnp.concatenate([lo, hi], axis=1)


def _in_proj_kernel(x_ref, g_ref, w_ref, gq_ref, wuq_ref, gkv_ref, wukv_ref, cc_ref, ss_ref,
                    fq_ref, fk_ref, fv_ref, qn_ref, qpe_ref, kv_ref, kpe_ref, flt_ref):
    subs = _sub_slices(x_ref.shape[0])
    ps = [_dot(_rms(x_ref[sl, :], g_ref[...]).astype(BF16), w_ref[...]) for sl in subs]
    scale = (MLA_NOPE + MLA_ROPE) ** -0.5 * LOG2E
    for sl, p in zip(subs, ps):
        fq_ref[sl, :] = (p[:, 0:512] * LOG2E).astype(BF16)
        fk_ref[sl, :] = p[:, 512:1024].astype(BF16)
        fv_ref[sl, :] = p[:, 1024:1536].astype(BF16)

        cc = cc_ref[sl, :]
        ss = ss_ref[sl, :]
        q = _dot(_rms(p[:, 1536:1792], gq_ref[...]).astype(BF16), wuq_ref[...])
        qn_ref[sl, :] = (q[:, 0:512] * scale).astype(BF16)
        cc2 = jnp.concatenate([cc, cc], axis=1)
        ss2 = jnp.concatenate([ss, ss], axis=1)
        qpe_ref[sl, :] = ((q[:, 512:768] * cc2 + q[:, 768:1024] * ss2) * scale).astype(BF16)

        kv_ref[sl, :] = _dot(_rms(p[:, 1792:1920], gkv_ref[...]).astype(BF16), wukv_ref[...]).astype(BF16)

        slab_a = p[:, 1920:2048]
        slab_b = p[:, 2048:2176]
        lane = lax.broadcasted_iota(jnp.int32, slab_a.shape, 1)
        roped = jnp.where(lane < MLA_ROPE, slab_a * cc + slab_b * ss, 0.0)
        tiled = roped + pltpu.roll(roped, 32, 1) + pltpu.roll(roped, 64, 1) + pltpu.roll(roped, 96, 1)
        kpe_ref[sl, :] = tiled.astype(BF16)
        flt_ref[:, sl] = slab_a.T[MLA_ROPE:MLA_ROPE + FOX_HEADS, :]


def _in_proj(x2d, g, w, gq, wuq, gkv, wukv, cc, ss):
    n = x2d.shape[0]
    rows = min(ROW_TILE, n)
    row = lambda c: pl.BlockSpec((rows, c), lambda i: (i, 0))
    full = lambda a: pl.BlockSpec(a.shape, lambda i: (0,) * a.ndim)
    out_shape = (
        jax.ShapeDtypeStruct((n, HEAD_W), BF16), jax.ShapeDtypeStruct((n, HEAD_W), BF16),
        jax.ShapeDtypeStruct((n, HEAD_W), BF16), jax.ShapeDtypeStruct((n, HEAD_W), BF16),
        jax.ShapeDtypeStruct((n, 256), BF16), jax.ShapeDtypeStruct((n, 1024), BF16),
        jax.ShapeDtypeStruct((n, LANES), BF16), jax.ShapeDtypeStruct((FOX_HEADS, n), F32),
    )
    return pl.pallas_call(
        _in_proj_kernel,
        grid=(n // rows,),
        in_specs=[row(D_MODEL), full(g), full(w), full(gq), full(wuq), full(gkv), full(wukv),
                  row(LANES), row(LANES)],
        out_specs=(row(HEAD_W), row(HEAD_W), row(HEAD_W), row(HEAD_W), row(256), row(1024), row(LANES),
                   pl.BlockSpec((FOX_HEADS, rows), lambda i: (0, i))),
        out_shape=out_shape,
        compiler_params=_params(("parallel",)),
        name="in_proj",
    )(x2d, g, w, gq, wuq, gkv, wukv, cc, ss)


def _fox_decay_kernel(fl_ref, b_ref, tri_ref, ck_ref):
    z = fl_ref[...] + b_ref[...]
    lf = jnp.minimum(z, 0.0) - jnp.log1p(jnp.exp(-jnp.abs(z)))
    seq = lf.shape[1]
    tri = tri_ref[...]
    carry = jnp.zeros((FOX_HEADS, 1), F32)
    zeros = jnp.zeros((FOX_HEADS, CUM_TILE), F32)
    for j in range(seq // CUM_TILE):
        v = lf[:, j * CUM_TILE:(j + 1) * CUM_TILE]
        hi = v.astype(BF16).astype(F32)
        r1 = v - hi
        mid = r1.astype(BF16).astype(F32)
        lo = r1 - mid
        parts = _dot(jnp.concatenate([hi, mid, lo, zeros], axis=0).astype(BF16), tri)
        cs = (parts[0:8] + parts[8:16]) + parts[16:24] + carry
        carry = cs[:, CUM_TILE - 1:CUM_TILE]
        d = cs * (-LOG2E)
        d_hi = d.astype(BF16).astype(F32)
        d_r = d - d_hi
        d_mid = d_r.astype(BF16).astype(F32)
        d_lo = d_r - d_mid
        rows = jnp.concatenate([d_hi, d_mid, d_lo, jnp.zeros((LANES - 3 * FOX_HEADS, CUM_TILE), F32)], axis=0)
        ck_ref[0, j * CUM_TILE:(j + 1) * CUM_TILE, :] = rows.T.astype(BF16)


def _fox_decay(flt, b_col, tri, batch, seq):
    return pl.pallas_call(
        _fox_decay_kernel,
        grid=(batch,),
        in_specs=[pl.BlockSpec((FOX_HEADS, seq), lambda b: (0, b)),
                  pl.BlockSpec((FOX_HEADS, 1), lambda b: (0, 0)),
                  pl.BlockSpec((CUM_TILE, CUM_TILE), lambda b: (0, 0))],
        out_specs=pl.BlockSpec((1, seq, LANES), lambda b: (b, 0, 0)),
        out_shape=jax.ShapeDtypeStruct((batch, seq, LANES), BF16),
        compiler_params=_params(("parallel",)),
        name="fox_decay",
    )(flt, b_col, tri)


def _probs(s_off, s_diag):
    m = jnp.max(s_diag, axis=-1, keepdims=True)
    if s_off is not None:
        m = jnp.maximum(m, jnp.max(s_off, axis=-1, keepdims=True))
    p_off = None if s_off is None else jnp.exp2(s_off - m).astype(BF16)
    return p_off, jnp.exp2(s_diag - m).astype(BF16)


def _attend(q_ref, k_ref, v_ref, o_ref, allowed):
    seq = q_ref.shape[1]
    out_lane = lax.broadcasted_iota(jnp.int32, (Q_TILE, LANES), 1)
    units = [(i, hh) for i in range(seq // Q_TILE) for hh in range(2)]

    def scores(i, hh):
        qs, qe = i * Q_TILE, (i + 1) * Q_TILE
        q = q_ref[hh, qs:qe, :]
        kh = hh % k_ref.shape[0]
        s_diag = jnp.where(allowed, _dot_nt(q, k_ref[kh, qs:qe, :]), NEG_BIG)
        s_off = _dot_nt(q, k_ref[kh, 0:qs, :]) if i > 0 else None
        return s_off, s_diag

    def values(i, hh, p_off, p_diag):
        qs, qe = i * Q_TILE, (i + 1) * Q_TILE
        vh = hh % v_ref.shape[0]
        acc = _dot(p_diag, v_ref[vh, qs:qe, :])
        if p_off is not None:
            acc = acc + _dot(p_off, v_ref[vh, 0:qs, :])
        return acc[:, 0:LANES] / acc[:, LANES:2 * LANES]

    n_units = len(units)
    s = {0: scores(*units[0])}
    if n_units > 1:
        s[1] = scores(*units[1])
    p = {0: _probs(*s.pop(0))}
    outs = []
    for n, (i, hh) in enumerate(units):
        if n + 2 < n_units:
            s[n + 2] = scores(*units[n + 2])
        if n + 1 < n_units:
            p[n + 1] = _probs(*s.pop(n + 1))
        outs.append(values(i, hh, *p.pop(n)))
        if hh == 1:
            o_ref[0, i * Q_TILE:(i + 1) * Q_TILE, :] = jnp.where(out_lane < 64, outs[0], outs[1]).astype(BF16)
            outs = []


def _tile_iota():
    row = lax.broadcasted_iota(jnp.int32, (Q_TILE, Q_TILE), 0)
    col = lax.broadcasted_iota(jnp.int32, (Q_TILE, Q_TILE), 1)
    return row, col


def _mla_attn_kernel(qn_ref, qpe_ref, kv_ref, kpe_ref, o_ref, qs_ref, ks_ref, vs_ref):
    j = pl.program_id(1)
    seq = qn_ref.shape[1]
    lane = lax.broadcasted_iota(jnp.int32, (seq, LANES), 1)
    qn = qn_ref[0]
    qpe = qpe_ref[0]
    kpe = kpe_ref[0]
    ones = jnp.ones((seq, LANES), BF16)
    for hh in range(2):
        nope_mask = (lane >= 64) if hh == 0 else (lane < 64)
        pe_slot = 2 * (j % 2) + hh
        pe_mask = (lane // MLA_ROPE) == pe_slot
        qs_ref[hh, :, 0:LANES] = jnp.where(nope_mask, qn, jnp.zeros_like(qn))
        qs_ref[hh, :, LANES:2 * LANES] = jnp.where(pe_mask, qpe, jnp.zeros_like(qpe))
        kvh = kv_ref[0, :, hh * LANES:(hh + 1) * LANES]
        ks_ref[hh, :, 0:LANES] = kvh
        ks_ref[hh, :, LANES:2 * LANES] = kpe
        vs_ref[hh, :, 0:LANES] = kvh
        vs_ref[hh, :, LANES:2 * LANES] = ones
    row, col = _tile_iota()
    _attend(qs_ref, ks_ref, vs_ref, o_ref, (col // CHUNK) <= (row // CHUNK))


def _mla_attn(qn, qpe, kv, kpe, batch, seq):
    qn, qpe, kv, kpe = (a.reshape(batch, seq, a.shape[-1]) for a in (qn, qpe, kv, kpe))
    pair = pltpu.VMEM((2, seq, 2 * LANES), BF16)
    return pl.pallas_call(
        _mla_attn_kernel,
        grid=(batch, MLA_HEADS // 2),
        in_specs=[pl.BlockSpec((1, seq, LANES), lambda b, j: (b, 0, j)),
                  pl.BlockSpec((1, seq, LANES), lambda b, j: (b, 0, j // 2)),
                  pl.BlockSpec((1, seq, 2 * LANES), lambda b, j: (b, 0, j)),
                  pl.BlockSpec((1, seq, LANES), lambda b, j: (b, 0, 0))],
        out_specs=pl.BlockSpec((1, seq, LANES), lambda b, j: (b, 0, j)),
        out_shape=jax.ShapeDtypeStruct((batch, seq, HEAD_W), BF16),
        scratch_shapes=[pair, pair, pair],
        compiler_params=_params(("parallel", "parallel")),
        name="mla_attn",
    )(qn, qpe, kv, kpe)


def _fox_attn_kernel(q_ref, k_ref, v_ref, ck_ref, o_ref, qs_ref, ks_ref, vs_ref):
    j = pl.program_id(1)
    seq = q_ref.shape[1]
    lane = lax.broadcasted_iota(jnp.int32, (seq, LANES), 1)
    q = q_ref[0]
    ks_ref[0, :, 0:LANES] = k_ref[0]
    ks_ref[0, :, LANES:2 * LANES] = ck_ref[0]
    vs_ref[0, :, 0:LANES] = v_ref[0]
    vs_ref[0, :, LANES:2 * LANES] = jnp.ones((seq, LANES), BF16)
    for hh in range(2):
        head = 2 * j + hh
        head_mask = (lane < 64) if hh == 0 else (lane >= 64)
        piece = jnp.where(lane < 3 * FOX_HEADS, lane % FOX_HEADS, -1) == head
        qs_ref[hh, :, 0:LANES] = jnp.where(head_mask, q, jnp.zeros_like(q))
        qs_ref[hh, :, LANES:2 * LANES] = jnp.where(piece, 1.0, 0.0).astype(BF16)
    row, col = _tile_iota()
    _attend(qs_ref, ks_ref, vs_ref, o_ref, col <= row)


def _fox_attn(fq, fk, fv, ck, batch, seq):
    fq, fk, fv = (a.reshape(batch, seq, HEAD_W) for a in (fq, fk, fv))
    spec = pl.BlockSpec((1, seq, LANES), lambda b, j: (b, 0, j))
    pair = pltpu.VMEM((2, seq, 2 * LANES), BF16)
    shared = pltpu.VMEM((1, seq, 2 * LANES), BF16)
    return pl.pallas_call(
        _fox_attn_kernel,
        grid=(batch, FOX_HEADS // 2),
        in_specs=[spec, spec, spec, pl.BlockSpec((1, seq, LANES), lambda b, j: (b, 0, 0))],
        out_specs=spec,
        out_shape=jax.ShapeDtypeStruct((batch, seq, HEAD_W), BF16),
        scratch_shapes=[pair, shared, shared],
        compiler_params=_params(("parallel", "parallel")),
        name="fox_attn",
    )(fq, fk, fv, ck)


def _lane_max(v):
    return jnp.max(v, axis=1, keepdims=True)


def _first_lane(hit, lane_f):
    return jnp.min(jnp.where(hit, lane_f, float(LANES)), axis=1, keepdims=True)


def _out_proj_kernel(om_ref, of_ref, x_ref, gm_ref, gf_ref, wo_ref, gn_ref, wr_ref, br_ref, tri_ref,
                     x1_ref, h2_ref, rt_ref, rtt_ref, cnt_ref, carry_ref):
    @pl.when(pl.program_id(0) == 0)
    def _():
        carry_ref[...] = jnp.zeros_like(carry_ref)

    subs = _sub_slices(x_ref.shape[0])
    mixed = [_mix_heads(sl, om_ref, of_ref, gm_ref, gf_ref, wo_ref) for sl in subs]
    logits = [_residual_and_logits(sl, m, x_ref, gn_ref, wr_ref, br_ref, x1_ref, h2_ref) for sl, m in zip(subs, mixed)]
    carry = carry_ref[...]
    for sl, lg in zip(subs, logits):
        rt, carry = _route_rows(lg, tri_ref[...], carry)
        rt_ref[sl, :] = rt
        rtt_ref[:, sl] = rt.T[0:ROUTE_ROWS, :]
    carry_ref[...] = carry
    cnt_ref[...] = carry


def _mix_heads(sl, om_ref, of_ref, gm_ref, gf_ref, wo_ref):
    a = _rms(om_ref[sl, :].astype(F32), gm_ref[...]).astype(BF16)
    b = _rms(of_ref[sl, :].astype(F32), gf_ref[...]).astype(BF16)
    return _dot(a, wo_ref[0:HEAD_W, :]) + _dot(b, wo_ref[HEAD_W:2 * HEAD_W, :])


def _residual_and_logits(sl, mixed, x_ref, gn_ref, wr_ref, br_ref, x1_ref, h2_ref):
    x1 = x_ref[sl, :] + mixed
    x1_ref[sl, :] = x1
    h2 = _rms(x1, gn_ref[...])
    h_hi = h2.astype(BF16)
    h2_ref[sl, :] = _pack_rows(h_hi.astype(F32))
    h_lo = (h2 - h_hi.astype(F32)).astype(BF16)
    t = _dot(h_hi, wr_ref[...])
    return (t[:, 0:LANES] + t[:, LANES:2 * LANES]) + _dot(h_lo, wr_ref[:, 0:LANES]) + br_ref[...]


def _route_rows(lg, tri, carry):
    lane = lax.broadcasted_iota(jnp.int32, lg.shape, 1)
    lane_f = lane.astype(F32)
    neg_inf = float("-inf")
    is_group = lane < N_GROUPS
    gl = jnp.where(is_group, lg, neg_inf)
    mg = _lane_max(gl)
    gi = _first_lane(gl == mg, lane_f)
    g_val = 1.0 / jnp.sum(jnp.where(is_group, jnp.exp(lg - mg), 0.0), axis=1, keepdims=True)
    group_of_lane = ((lane - N_GROUPS) >> 3).astype(F32)
    is_expert = (lane >= N_GROUPS) & (lane < N_GROUPS + N_EXPERTS) & (group_of_lane == gi)
    el = jnp.where(is_expert, lg, neg_inf)
    m1 = _lane_max(el)
    i1 = _first_lane(el == m1, lane_f)
    el2 = jnp.where(lane_f == i1, neg_inf, el)
    m2 = _lane_max(el2)
    i2 = _first_lane(el2 == m2, lane_f)
    r = jnp.exp(m2 - m1)
    g0 = g_val / (1.0 + r)
    g1 = g0 * r

    hit1 = lane_f == i1
    hit2 = lane_f == i2
    onehot = jnp.where(hit1 | hit2, 1.0, 0.0)
    before = _dot(tri, onehot.astype(BF16)) + carry
    r0 = jnp.sum(jnp.where(hit1, before, 0.0), axis=1, keepdims=True)
    r1 = jnp.sum(jnp.where(hit2, before, 0.0), axis=1, keepdims=True)

    vals = (i1 - N_GROUPS, i2 - N_GROUPS, g0, g1, r0, r1)
    rt = jnp.zeros_like(lg)
    for k, v in enumerate(vals):
        rt = jnp.where(lane == k, v, rt)
    return rt, carry + jnp.sum(onehot, axis=0, keepdims=True)


def _out_proj(om, of, x2d, gm, gf, wo, gn, wr, br, tri):
    n = x2d.shape[0]
    rows = tri.shape[0] * SUB_TILES
    row = lambda c: pl.BlockSpec((rows, c), lambda i: (i, 0))
    full = lambda a: pl.BlockSpec(a.shape, lambda i: (0,) * a.ndim)
    return pl.pallas_call(
        _out_proj_kernel,
        grid=(n // rows,),
        in_specs=[row(HEAD_W), row(HEAD_W), row(D_MODEL), full(gm), full(gf), full(wo), full(gn), full(wr),
                  full(br), full(tri)],
        out_specs=(row(D_MODEL), row(D_MODEL // 2), row(LANES), pl.BlockSpec((ROUTE_ROWS, rows), lambda i: (0, i)),
                   pl.BlockSpec((1, LANES), lambda i: (0, 0))),
        out_shape=(jax.ShapeDtypeStruct((n, D_MODEL), F32), jax.ShapeDtypeStruct((n, D_MODEL // 2), jnp.uint32),
                   jax.ShapeDtypeStruct((n, LANES), F32), jax.ShapeDtypeStruct((ROUTE_ROWS, n), F32),
                   jax.ShapeDtypeStruct((1, LANES), F32)),
        scratch_shapes=[pltpu.VMEM((1, LANES), F32)],
        compiler_params=_params(("arbitrary",)),
        name="out_proj",
    )(om, of, x2d, gm, gf, wo, gn, wr, br, tri)


def _gather_rows(table, idx):
    m, w = idx.shape[0], table.shape[1]
    workers = SC_CORES * SC_SUBCORES
    per_worker = m // workers
    n_win = per_worker // GATHER_WINDOW
    assert per_worker * workers == m and n_win * GATHER_WINDOW == per_worker and n_win % 2 == 0
    mesh = plsc.VectorSubcoreMesh(core_axis_name="core", subcore_axis_name="subcore")

    @functools.partial(
        pl.kernel, mesh=mesh, out_type=jax.ShapeDtypeStruct((m, w), table.dtype),
        scratch_types=[pltpu.VMEM((per_worker,), jnp.int32), pltpu.VMEM((2, GATHER_WINDOW, w), table.dtype),
                       pltpu.SemaphoreType.DMA((2,)), pltpu.SemaphoreType.DMA((2,))],
        name="gather_rows")
    def gather(table_hbm, idx_hbm, out_hbm, idx_v, rows_v, gather_sems, write_sems):
        base = (lax.axis_index("subcore") * SC_CORES + lax.axis_index("core")) * per_worker
        pltpu.sync_copy(idx_hbm.at[pl.ds(base, per_worker)], idx_v)

        def gather_copy(win, buf):
            rows = idx_v.at[pl.ds(win * GATHER_WINDOW, GATHER_WINDOW)]
            return pltpu.make_async_copy(table_hbm.at[rows], rows_v.at[buf], gather_sems.at[buf])

        def write_copy(win, buf):
            dst = out_hbm.at[pl.ds(base + win * GATHER_WINDOW, GATHER_WINDOW)]
            return pltpu.make_async_copy(rows_v.at[buf], dst, write_sems.at[buf])

        gather_copy(0, 0).start()

        @pl.loop(0, n_win, step=2)
        def _(j):
            for buf in range(2):
                win = j + buf
                gather_copy(win, buf).wait()
                write_copy(win, buf).start()

                @pl.when(win + 1 < n_win)
                def _():
                    @pl.when(win >= 1)
                    def _():
                        write_copy(win - 1, 1 - buf).wait()

                    gather_copy(win + 1, 1 - buf).start()

        write_copy(n_win - 2, 0).wait()
        write_copy(n_win - 1, 1).wait()

    return gather(table, idx)


def _scatter_rows(rows, dests, n_out):
    n, w = rows.shape
    workers = SC_CORES * SC_SUBCORES
    per_worker = n // workers
    n_win = per_worker // GATHER_WINDOW
    n_lists = len(dests)
    assert per_worker * workers == n and n_win * GATHER_WINDOW == per_worker and n_win % 2 == 0
    mesh = plsc.VectorSubcoreMesh(core_axis_name="core", subcore_axis_name="subcore")

    @functools.partial(
        pl.kernel, mesh=mesh, out_type=jax.ShapeDtypeStruct((n_out, w), rows.dtype),
        scratch_types=[pltpu.VMEM((per_worker,), jnp.int32)] * n_lists + [
            pltpu.VMEM((2, GATHER_WINDOW, w), rows.dtype), pltpu.SemaphoreType.DMA((2,)),
            pltpu.SemaphoreType.DMA((2,))],
        name="scatter_rows")
    def scatter(rows_hbm, *refs):
        dest_hbm, out_hbm = refs[:n_lists], refs[n_lists]
        dest_v = refs[n_lists + 1:2 * n_lists + 1]
        rows_v, read_sems, write_sems = refs[2 * n_lists + 1:]
        base = (lax.axis_index("subcore") * SC_CORES + lax.axis_index("core")) * per_worker
        for d_hbm, d_v in zip(dest_hbm, dest_v):
            pltpu.sync_copy(d_hbm.at[pl.ds(base, per_worker)], d_v)

        def read_copy(win, buf):
            src = rows_hbm.at[pl.ds(base + win * GATHER_WINDOW, GATHER_WINDOW)]
            return pltpu.make_async_copy(src, rows_v.at[buf], read_sems.at[buf])

        def write_copies(win, buf):
            return [pltpu.make_async_copy(rows_v.at[buf],
                                          out_hbm.at[d_v.at[pl.ds(win * GATHER_WINDOW, GATHER_WINDOW)]],
                                          write_sems.at[buf]) for d_v in dest_v]

        read_copy(0, 0).start()

        @pl.loop(0, n_win, step=2)
        def _(j):
            for buf in range(2):
                win = j + buf
                read_copy(win, buf).wait()
                for c in write_copies(win, buf):
                    c.start()

                @pl.when(win + 1 < n_win)
                def _():
                    @pl.when(win >= 1)
                    def _():
                        for c in write_copies(win - 1, 1 - buf):
                            c.wait()

                    read_copy(win + 1, 1 - buf).start()

        for c in write_copies(n_win - 2, 0) + write_copies(n_win - 1, 1):
            c.wait()

    return scatter(rows, *dests)


def _experts_kernel(be_ref, bv_ref, xs_ref, wg_ref, wu_ref, wd_ref, o_ref, wg_s, wu_s, wd_s):
    i = pl.program_id(0)
    valid = bv_ref[i] != 0

    @pl.when(jnp.logical_and(valid, jnp.logical_or(i == 0, be_ref[i] != be_ref[jnp.maximum(i - 1, 0)])))
    def _():
        wg_s[...] = wg_ref[0, 0].astype(BF16)
        wu_s[...] = wu_ref[0, 0].astype(BF16)
        wd_s[...] = wd_ref[0, 0].astype(BF16)

    @pl.when(valid)
    def _():
        row = lax.broadcasted_iota(jnp.int32, xs_ref.shape, 0)
        packed = jnp.where(row < bv_ref[i], xs_ref[...], jnp.uint32(0))
        subs = _sub_slices(EXPERT_TILE)
        gate_up = []
        for sl in subs:
            x = _unpack_rows(packed[sl, :]).astype(BF16)
            gate_up.append((_dot(x, wg_s[...]), _dot(x, wu_s[...])))
        for sl, (g, u) in zip(subs, gate_up):
            act = ((g * jax.nn.sigmoid(g)) * u).astype(BF16)
            o_ref[sl, :] = _pack_rows(_dot(act, wd_s[...]).astype(BF16).astype(F32))

    @pl.when(jnp.logical_not(valid))
    def _():
        o_ref[...] = jnp.zeros_like(o_ref)


def _experts(layer, blk_expert, blk_rows, xs, w_gate, w_up, w_down):
    n_slots = xs.shape[0]
    n_blocks = n_slots // EXPERT_TILE
    grid_spec = pltpu.PrefetchScalarGridSpec(
        num_scalar_prefetch=2,
        grid=(n_blocks,),
        in_specs=[pl.BlockSpec((EXPERT_TILE, D_MODEL // 2), lambda i, be, bv: (i, 0)),
                  pl.BlockSpec((1, 1, D_MODEL, D_EXPERT), lambda i, be, bv: (layer, be[i], 0, 0)),
                  pl.BlockSpec((1, 1, D_MODEL, D_EXPERT), lambda i, be, bv: (layer, be[i], 0, 0)),
                  pl.BlockSpec((1, 1, D_EXPERT, D_MODEL), lambda i, be, bv: (layer, be[i], 0, 0))],
        out_specs=pl.BlockSpec((EXPERT_TILE, D_MODEL // 2), lambda i, be, bv: (i, 0)),
        scratch_shapes=[pltpu.VMEM((D_MODEL, D_EXPERT), BF16), pltpu.VMEM((D_MODEL, D_EXPERT), BF16),
                        pltpu.VMEM((D_EXPERT, D_MODEL), BF16)],
    )
    return pl.pallas_call(
        _experts_kernel,
        grid_spec=grid_spec,
        out_shape=jax.ShapeDtypeStruct((n_slots, D_MODEL // 2), jnp.uint32),
        compiler_params=_params(("arbitrary",)),
        name="experts",
    )(blk_expert, blk_rows, xs, w_gate, w_up, w_down)


def _combine_kernel(x_ref, y0_ref, y1_ref, rt_ref, g_ref, o_ref, *, final):
    g0 = rt_ref[:, TOP_K:TOP_K + 1]
    g1 = rt_ref[:, TOP_K + 1:TOP_K + 2]
    x = x_ref[...] + (_unpack_rows(y0_ref[...]) * g0 + _unpack_rows(y1_ref[...]) * g1)
    o_ref[...] = _rms(x, g_ref[...]) if final else x


def _combine(x1, y01, route, g, final):
    n = x1.shape[0]
    rows = min(ROW_TILE, n)
    steps = n // rows
    row = pl.BlockSpec((rows, D_MODEL), lambda i: (i, 0))
    return pl.pallas_call(
        functools.partial(_combine_kernel, final=final),
        grid=(steps,),
        in_specs=[row, pl.BlockSpec((rows, D_MODEL // 2), lambda i: (i, 0)),
                  pl.BlockSpec((rows, D_MODEL // 2), lambda i: (i + steps, 0)),
                  pl.BlockSpec((rows, LANES), lambda i: (i, 0)),
                  pl.BlockSpec((1, D_MODEL), lambda i: (0, 0))],
        out_specs=row,
        out_shape=jax.ShapeDtypeStruct((n, D_MODEL), F32),
        compiler_params=_params(("parallel",)),
        name="combine",
    )(x1, y01, y01, route, g)


def _swap_halves(w):
    half = w.shape[-1] // 2
    return jnp.concatenate([w[..., half:], w[..., :half]], axis=-1)


def _prep_in_weights(w_in):
    d = w_in.shape[0]
    w_in = w_in.astype(BF16)
    q_lat, kv_lat, kr = w_in[:, 0:256], w_in[:, 256:384], w_in[:, 384:416]
    fq, fk, fv, fl = w_in[:, 416:928], w_in[:, 928:1440], w_in[:, 1440:1952], w_in[:, 1952:1960]
    slab_a = jnp.concatenate([kr, fl, jnp.zeros((d, LANES - MLA_ROPE - FOX_HEADS), BF16)], axis=1)
    slab_b = jnp.concatenate([_swap_halves(kr), jnp.zeros((d, LANES - MLA_ROPE), BF16)], axis=1)
    return jnp.concatenate([fq * (FOX_DIM ** -0.5), fk, fv, q_lat, kv_lat, slab_a, slab_b], axis=1)


def _prep_uq(w_uq):
    w = w_uq.reshape(MLA_Q_RANK, MLA_HEADS, MLA_NOPE + MLA_ROPE)
    nope, pe = w[:, :, :MLA_NOPE], w[:, :, MLA_NOPE:]
    pairs = nope.reshape(MLA_Q_RANK, MLA_HEADS // 2, 2, MLA_NOPE)[:, :, ::-1, :].reshape(MLA_Q_RANK, -1)
    return jnp.concatenate([pairs, pe.reshape(MLA_Q_RANK, -1), _swap_halves(pe).reshape(MLA_Q_RANK, -1)],
                           axis=1).astype(BF16)


def _prep_ukv(w_ukv):
    w = w_ukv.reshape(MLA_KV_RANK, MLA_HEADS // 2, 2, 2, MLA_NOPE)
    even = w[:, :, 0, ::-1, :]
    odd = w[:, :, 1, :, :]
    return jnp.stack([even, odd], axis=2).reshape(MLA_KV_RANK, -1).astype(BF16)


def _prep_router(w_rg, b_rg, w_re, b_re):
    d = w_rg.shape[0]
    w = jnp.concatenate([w_rg, w_re, jnp.zeros((d, LANES - N_GROUPS - N_EXPERTS), F32)], axis=1)
    w_hi = w.astype(BF16)
    w_lo = (w - w_hi.astype(F32)).astype(BF16)
    b = jnp.concatenate([b_rg, b_re, jnp.zeros((LANES - N_GROUPS - N_EXPERTS,), F32)])[None, :]
    return jnp.concatenate([w_hi, w_lo], axis=1), b


def _rope_slabs(positions):
    half = MLA_ROPE // 2
    inv_freq = ROPE_THETA ** (-jnp.arange(half, dtype=F32) / half)
    ang = positions.astype(F32).reshape(-1)[:, None] * inv_freq
    cos, sin = jnp.cos(ang), jnp.sin(ang)
    reps = LANES // MLA_ROPE
    return jnp.tile(jnp.concatenate([cos, cos], axis=1), (1, reps)), jnp.tile(
        jnp.concatenate([-sin, sin], axis=1), (1, reps))


def _slot_layout(route_t, counts, n_blocks):
    eid = route_t[0:TOP_K].astype(jnp.int32)
    rank = route_t[2 * TOP_K:3 * TOP_K].astype(jnp.int32)
    counts = counts[0, N_GROUPS:N_GROUPS + N_EXPERTS].astype(jnp.int32)
    padded = (counts + EXPERT_TILE - 1) // EXPERT_TILE * EXPERT_TILE
    pad_end = jnp.cumsum(padded)
    pad_start = pad_end - padded
    experts = jnp.arange(N_EXPERTS, dtype=jnp.int32)
    dest = jnp.sum(jnp.where(eid[None] == experts[:, None, None], pad_start[:, None, None], 0), axis=0) + rank
    blk_start = jnp.arange(n_blocks, dtype=jnp.int32) * EXPERT_TILE
    blk_expert = jnp.minimum(jnp.sum((blk_start[:, None] >= pad_end[None, :]).astype(jnp.int32), axis=1),
                             N_EXPERTS - 1)
    in_expert = jnp.sum(jnp.where(blk_expert[:, None] == experts, counts + pad_start, 0), axis=-1) - blk_start
    blk_rows = jnp.where(blk_start < pad_end[-1], jnp.clip(in_expert, 0, EXPERT_TILE), 0).astype(jnp.int32)
    return dest, blk_expert, blk_rows


def kernel(x, positions, attn_norm, w_in, b_forget, q_norm, w_uq, kv_norm, w_ukv, mla_out_norm, fox_out_norm,
           w_out, ffn_norm, w_router_group, b_router_group, w_router_expert, b_router_expert, w_gate, w_up,
           w_down, final_norm):
    batch, seq, d = x.shape
    n = batch * seq
    depth = w_in.shape[0]
    n_blocks = -(-(n * TOP_K) // EXPERT_TILE) + N_EXPERTS
    cc, ss = _rope_slabs(positions)
    tri = (jnp.arange(CUM_TILE)[:, None] <= jnp.arange(CUM_TILE)[None, :]).astype(BF16)
    sub = min(ROW_TILE, n) // SUB_TILES
    tri_rows = (jnp.arange(sub)[None, :] < jnp.arange(sub)[:, None]).astype(BF16)
    xf = x.reshape(n, d)
    for l in range(depth):
        fq, fk, fv, qn, qpe, kv, kpe, flt = _in_proj(
            xf, attn_norm[l][None, :], _prep_in_weights(w_in[l]), q_norm[l][None, :], _prep_uq(w_uq[l]),
            kv_norm[l][None, :], _prep_ukv(w_ukv[l]), cc, ss)
        ck = _fox_decay(flt, b_forget[l][:, None], tri, batch, seq)
        o_mla = _mla_attn(qn, qpe, kv, kpe, batch, seq).reshape(n, HEAD_W)
        o_fox = _fox_attn(fq, fk, fv, ck, batch, seq).reshape(n, HEAD_W)
        wr, br = _prep_router(w_router_group[l], b_router_group[l], w_router_expert[l], b_router_expert[l])
        x1, h2, route, route_t, counts = _out_proj(
            o_mla, o_fox, xf, mla_out_norm[l][None, :], fox_out_norm[l][None, :], w_out[l].astype(BF16),
            ffn_norm[l][None, :], wr, br, tri_rows)
        dest, blk_expert, blk_rows = _slot_layout(route_t, counts, n_blocks)
        xs = _scatter_rows(h2, [dest[k] for k in range(TOP_K)], n_blocks * EXPERT_TILE)
        ys = _experts(l, blk_expert, blk_rows, xs, w_gate, w_up, w_down)
        y01 = _gather_rows(ys, dest.reshape(-1))
        final = l == depth - 1
        xf = _combine(x1, y01, route, final_norm[None, :] if final else ffn_norm[l][None, :], final)
    return xf.reshape(batch, seq, d)
```

```python
import functools

import jax
import numpy as np
import jax.numpy as jnp
from jax import lax
from jax.experimental import pallas as pl
from jax.experimental.pallas import tpu as pltpu
from jax.experimental.pallas import tpu_sc as plsc

D_MODEL = 1024
CHUNK = 64
MLA_HEADS = 8
MLA_NOPE = 64
MLA_ROPE = 32
MLA_V = 64
MLA_Q_RANK = 256
MLA_KV_RANK = 128
ROPE_THETA = 10000.0
FOX_HEADS = 8
FOX_DIM = 64
HEAD_W = 512
N_GROUPS = 4
EXPERTS_PER_GROUP = 8
N_EXPERTS = 32
TOP_K = 2
D_EXPERT = 512
NORM_EPS = 1e-6

LANES = 128
IN_COLS = 2048
ROW_TILE = 512
SUB_TILES = 2
Q_TILE = 256
EXPERT_TILE = 512
CUM_TILE = 256
NEG_BIG = -1e30
LOG2E = 1.4426950408889634
VMEM_LIMIT = 48 * 1024 * 1024
SC_CORES = 2
SC_SUBCORES = 16
GATHER_WINDOW = 64
ROUTE_ROWS = 8

F32 = jnp.float32
BF16 = jnp.bfloat16


def _rms(x, g):
    return (x * lax.rsqrt(jnp.mean(x * x, axis=-1, keepdims=True) + NORM_EPS)) * g


def _dot(a, b):
    return jnp.dot(a, b, preferred_element_type=F32)


def _dot_nt(a, b):
    return lax.dot_general(a, b, (((1,), (1,)), ((), ())), preferred_element_type=F32)


def _params(sem):
    return pltpu.CompilerParams(dimension_semantics=sem, vmem_limit_bytes=VMEM_LIMIT)


def _sub_slices(rows):
    sub = rows // SUB_TILES
    return [slice(t * sub, (t + 1) * sub) for t in range(SUB_TILES)]


def _pack_rows(v):
    w = v.shape[1] // 2
    lo = lax.bitcast_convert_type(v[:, :w], jnp.uint32) >> 16
    hi = lax.bitcast_convert_type(v[:, w:], jnp.uint32) & jnp.uint32(0xFFFF0000)
    return hi | lo


def _unpack_rows(p):
    lo = lax.bitcast_convert_type(p << 16, F32)
    hi = lax.bitcast_convert_type(p & jnp.uint32(0xFFFF0000), F32)
    return jnp.concatenate([lo, hi], axis=1)


def _in_proj_kernel(x_ref, g_ref, w_ref, gq_ref, wuq_ref, gkv_ref, wukv_ref, cc_ref, ss_ref,
                    fq_ref, fk_ref, fv_ref, qn_ref, qpe_ref, kv_ref, kpe_ref, flt_ref):
    subs = _sub_slices(x_ref.shape[0])
    ps = [_dot(_rms(x_ref[sl, :], g_ref[...]).astype(BF16), w_ref[...]) for sl in subs]
    scale = (MLA_NOPE + MLA_ROPE) ** -0.5 * LOG2E
    for sl, p in zip(subs, ps):
        fq_ref[sl, :] = (p[:, 0:512] * LOG2E).astype(BF16)
        fk_ref[sl, :] = p[:, 512:1024].astype(BF16)
        fv_ref[sl, :] = p[:, 1024:1536].astype(BF16)

        cc = cc_ref[sl, :]
        ss = ss_ref[sl, :]
        q = _dot(_rms(p[:, 1536:1792], gq_ref[...]).astype(BF16), wuq_ref[...])
        qn_ref[sl, :] = (q[:, 0:512] * scale).astype(BF16)
        cc2 = jnp.concatenate([cc, cc], axis=1)
        ss2 = jnp.concatenate([ss, ss], axis=1)
        qpe_ref[sl, :] = ((q[:, 512:768] * cc2 + q[:, 768:1024] * ss2) * scale).astype(BF16)

        kv_ref[sl, :] = _dot(_rms(p[:, 1792:1920], gkv_ref[...]).astype(BF16), wukv_ref[...]).astype(BF16)

        slab_a = p[:, 1920:2048]
        lane = lax.broadcasted_iota(jnp.int32, slab_a.shape, 1)
        half = MLA_ROPE // 2
        slab_b = jnp.where(lane < half, pltpu.roll(slab_a, LANES - half, 1), pltpu.roll(slab_a, half, 1))
        roped = jnp.where(lane < MLA_ROPE, slab_a * cc + slab_b * ss, 0.0)
        tiled = roped + pltpu.roll(roped, 32, 1) + pltpu.roll(roped, 64, 1) + pltpu.roll(roped, 96, 1)
        kpe_ref[sl, :] = tiled.astype(BF16)
        flt_ref[:, sl] = slab_a.T[MLA_ROPE:MLA_ROPE + FOX_HEADS, :]


def _in_proj(x2d, g, w, gq, wuq, gkv, wukv, cc, ss):
    n = x2d.shape[0]
    rows = min(ROW_TILE, n)
    row = lambda c: pl.BlockSpec((rows, c), lambda i: (i, 0))
    full = lambda a: pl.BlockSpec(a.shape, lambda i: (0,) * a.ndim)
    out_shape = (
        jax.ShapeDtypeStruct((n, HEAD_W), BF16), jax.ShapeDtypeStruct((n, HEAD_W), BF16),
        jax.ShapeDtypeStruct((n, HEAD_W), BF16), jax.ShapeDtypeStruct((n, HEAD_W), BF16),
        jax.ShapeDtypeStruct((n, 256), BF16), jax.ShapeDtypeStruct((n, 1024), BF16),
        jax.ShapeDtypeStruct((n, LANES), BF16), jax.ShapeDtypeStruct((FOX_HEADS, n), F32),
    )
    return pl.pallas_call(
        _in_proj_kernel,
        grid=(n // rows,),
        in_specs=[row(D_MODEL), full(g), full(w), full(gq), full(wuq), full(gkv), full(wukv),
                  row(LANES), row(LANES)],
        out_specs=(row(HEAD_W), row(HEAD_W), row(HEAD_W), row(HEAD_W), row(256), row(1024), row(LANES),
                   pl.BlockSpec((FOX_HEADS, rows), lambda i: (0, i))),
        out_shape=out_shape,
        compiler_params=_params(("parallel",)),
        name="in_proj",
    )(x2d, g, w, gq, wuq, gkv, wukv, cc, ss)


def _fox_decay_kernel(fl_ref, b_ref, tri_ref, ck_ref):
    z = fl_ref[...] + b_ref[...]
    lf = jnp.minimum(z, 0.0) - jnp.log1p(jnp.exp(-jnp.abs(z)))
    seq = lf.shape[1]
    tri = tri_ref[...]
    carry = jnp.zeros((FOX_HEADS, 1), F32)
    zeros = jnp.zeros((FOX_HEADS, CUM_TILE), F32)
    for j in range(seq // CUM_TILE):
        v = lf[:, j * CUM_TILE:(j + 1) * CUM_TILE]
        hi = v.astype(BF16).astype(F32)
        r1 = v - hi
        mid = r1.astype(BF16).astype(F32)
        lo = r1 - mid
        parts = _dot(jnp.concatenate([hi, mid, lo, zeros], axis=0).astype(BF16), tri)
        cs = (parts[0:8] + parts[8:16]) + parts[16:24] + carry
        carry = cs[:, CUM_TILE - 1:CUM_TILE]
        d = cs * (-LOG2E)
        d_hi = d.astype(BF16).astype(F32)
        d_r = d - d_hi
        d_mid = d_r.astype(BF16).astype(F32)
        d_lo = d_r - d_mid
        rows = jnp.concatenate([d_hi, d_mid, d_lo, jnp.zeros((LANES - 3 * FOX_HEADS, CUM_TILE), F32)], axis=0)
        ck_ref[0, j * CUM_TILE:(j + 1) * CUM_TILE, :] = rows.T.astype(BF16)


def _fox_decay(flt, b_col, tri, batch, seq):
    return pl.pallas_call(
        _fox_decay_kernel,
        grid=(batch,),
        in_specs=[pl.BlockSpec((FOX_HEADS, seq), lambda b: (0, b)),
                  pl.BlockSpec((FOX_HEADS, 1), lambda b: (0, 0)),
                  pl.BlockSpec((CUM_TILE, CUM_TILE), lambda b: (0, 0))],
        out_specs=pl.BlockSpec((1, seq, LANES), lambda b: (b, 0, 0)),
        out_shape=jax.ShapeDtypeStruct((batch, seq, LANES), BF16),
        compiler_params=_params(("parallel",)),
        name="fox_decay",
    )(flt, b_col, tri)


def _probs(s_off, s_diag):
    m = jnp.max(s_diag, axis=-1, keepdims=True)
    if s_off is not None:
        m = jnp.maximum(m, jnp.max(s_off, axis=-1, keepdims=True))
    p_off = None if s_off is None else jnp.exp2(s_off - m).astype(BF16)
    return p_off, jnp.exp2(s_diag - m).astype(BF16)


def _attend(q_ref, k_ref, v_ref, o_ref, allowed):
    seq = q_ref.shape[1]
    out_lane = lax.broadcasted_iota(jnp.int32, (Q_TILE, LANES), 1)
    units = [(i, hh) for i in range(seq // Q_TILE) for hh in range(2)]

    def scores(i, hh):
        qs, qe = i * Q_TILE, (i + 1) * Q_TILE
        q = q_ref[hh, qs:qe, :]
        kh = hh % k_ref.shape[0]
        s_diag = jnp.where(allowed, _dot_nt(q, k_ref[kh, qs:qe, :]), NEG_BIG)
        s_off = _dot_nt(q, k_ref[kh, 0:qs, :]) if i > 0 else None
        return s_off, s_diag

    def values(i, hh, p_off, p_diag):
        qs, qe = i * Q_TILE, (i + 1) * Q_TILE
        vh = hh % v_ref.shape[0]
        acc = _dot(p_diag, v_ref[vh, qs:qe, :])
        if p_off is not None:
            acc = acc + _dot(p_off, v_ref[vh, 0:qs, :])
        return acc[:, 0:LANES] / acc[:, LANES:2 * LANES]

    n_units = len(units)
    s = {0: scores(*units[0])}
    if n_units > 1:
        s[1] = scores(*units[1])
    p = {0: _probs(*s.pop(0))}
    outs = []
    for n, (i, hh) in enumerate(units):
        if n + 2 < n_units:
            s[n + 2] = scores(*units[n + 2])
        if n + 1 < n_units:
            p[n + 1] = _probs(*s.pop(n + 1))
        outs.append(values(i, hh, *p.pop(n)))
        if hh == 1:
            o_ref[0, i * Q_TILE:(i + 1) * Q_TILE, :] = jnp.where(out_lane < 64, outs[0], outs[1]).astype(BF16)
            outs = []


def _tile_iota():
    row = lax.broadcasted_iota(jnp.int32, (Q_TILE, Q_TILE), 0)
    col = lax.broadcasted_iota(jnp.int32, (Q_TILE, Q_TILE), 1)
    return row, col


def _mla_attn_kernel(qn_ref, qpe_ref, kv_ref, kpe_ref, o_ref, qs_ref, ks_ref, vs_ref):
    j = pl.program_id(1)
    seq = qn_ref.shape[1]
    lane = lax.broadcasted_iota(jnp.int32, (seq, LANES), 1)
    qn = qn_ref[0]
    qpe = qpe_ref[0]
    kpe = kpe_ref[0]
    ones = jnp.ones((seq, LANES), BF16)
    for hh in range(2):
        nope_mask = (lane >= 64) if hh == 0 else (lane < 64)
        pe_slot = 2 * (j % 2) + hh
        pe_mask = (lane // MLA_ROPE) == pe_slot
        qs_ref[hh, :, 0:LANES] = jnp.where(nope_mask, qn, jnp.zeros_like(qn))
        qs_ref[hh, :, LANES:2 * LANES] = jnp.where(pe_mask, qpe, jnp.zeros_like(qpe))
        kvh = kv_ref[0, :, hh * LANES:(hh + 1) * LANES]
        ks_ref[hh, :, 0:LANES] = kvh
        ks_ref[hh, :, LANES:2 * LANES] = kpe
        vs_ref[hh, :, 0:LANES] = kvh
        vs_ref[hh, :, LANES:2 * LANES] = ones
    row, col = _tile_iota()
    _attend(qs_ref, ks_ref, vs_ref, o_ref, (col // CHUNK) <= (row // CHUNK))


def _mla_attn(qn, qpe, kv, kpe, batch, seq):
    qn, qpe, kv, kpe = (a.reshape(batch, seq, a.shape[-1]) for a in (qn, qpe, kv, kpe))
    pair = pltpu.VMEM((2, seq, 2 * LANES), BF16)
    return pl.pallas_call(
        _mla_attn_kernel,
        grid=(batch, MLA_HEADS // 2),
        in_specs=[pl.BlockSpec((1, seq, LANES), lambda b, j: (b, 0, j)),
                  pl.BlockSpec((1, seq, LANES), lambda b, j: (b, 0, j // 2)),
                  pl.BlockSpec((1, seq, 2 * LANES), lambda b, j: (b, 0, j)),
                  pl.BlockSpec((1, seq, LANES), lambda b, j: (b, 0, 0))],
        out_specs=pl.BlockSpec((1, seq, LANES), lambda b, j: (b, 0, j)),
        out_shape=jax.ShapeDtypeStruct((batch, seq, HEAD_W), BF16),
        scratch_shapes=[pair, pair, pair],
        compiler_params=_params(("parallel", "parallel")),
        name="mla_attn",
    )(qn, qpe, kv, kpe)


def _fox_attn_kernel(q_ref, k_ref, v_ref, ck_ref, o_ref, qs_ref, ks_ref, vs_ref):
    j = pl.program_id(1)
    seq = q_ref.shape[1]
    lane = lax.broadcasted_iota(jnp.int32, (seq, LANES), 1)
    q = q_ref[0]
    ks_ref[0, :, 0:LANES] = k_ref[0]
    ks_ref[0, :, LANES:2 * LANES] = ck_ref[0]
    vs_ref[0, :, 0:LANES] = v_ref[0]
    vs_ref[0, :, LANES:2 * LANES] = jnp.ones((seq, LANES), BF16)
    for hh in range(2):
        head = 2 * j + hh
        head_mask = (lane < 64) if hh == 0 else (lane >= 64)
        piece = jnp.where(lane < 3 * FOX_HEADS, lane % FOX_HEADS, -1) == head
        qs_ref[hh, :, 0:LANES] = jnp.where(head_mask, q, jnp.zeros_like(q))
        qs_ref[hh, :, LANES:2 * LANES] = jnp.where(piece, 1.0, 0.0).astype(BF16)
    row, col = _tile_iota()
    _attend(qs_ref, ks_ref, vs_ref, o_ref, col <= row)


def _fox_attn(fq, fk, fv, ck, batch, seq):
    fq, fk, fv = (a.reshape(batch, seq, HEAD_W) for a in (fq, fk, fv))
    spec = pl.BlockSpec((1, seq, LANES), lambda b, j: (b, 0, j))
    pair = pltpu.VMEM((2, seq, 2 * LANES), BF16)
    shared = pltpu.VMEM((1, seq, 2 * LANES), BF16)
    return pl.pallas_call(
        _fox_attn_kernel,
        grid=(batch, FOX_HEADS // 2),
        in_specs=[spec, spec, spec, pl.BlockSpec((1, seq, LANES), lambda b, j: (b, 0, 0))],
        out_specs=spec,
        out_shape=jax.ShapeDtypeStruct((batch, seq, HEAD_W), BF16),
        scratch_shapes=[pair, shared, shared],
        compiler_params=_params(("parallel", "parallel")),
        name="fox_attn",
    )(fq, fk, fv, ck)


def _lane_max(v):
    return jnp.max(v, axis=1, keepdims=True)


def _first_lane(hit, lane_f):
    return jnp.min(jnp.where(hit, lane_f, float(LANES)), axis=1, keepdims=True)


def _out_proj_kernel(om_ref, of_ref, x_ref, gm_ref, gf_ref, wo_ref, gn_ref, wr_ref, br_ref, tri_ref,
                     x1_ref, h2_ref, rt_ref, rtt_ref, cnt_ref, carry_ref):
    @pl.when(pl.program_id(0) == 0)
    def _():
        carry_ref[...] = jnp.zeros_like(carry_ref)

    subs = _sub_slices(x_ref.shape[0])
    mixed = [_mix_heads(sl, om_ref, of_ref, gm_ref, gf_ref, wo_ref) for sl in subs]
    logits = [_residual_and_logits(sl, m, x_ref, gn_ref, wr_ref, br_ref, x1_ref, h2_ref) for sl, m in zip(subs, mixed)]
    carry = carry_ref[...]
    for sl, lg in zip(subs, logits):
        rt, carry = _route_rows(lg, tri_ref[...], carry)
        rt_ref[sl, :] = rt
        rtt_ref[:, sl] = rt.T[0:ROUTE_ROWS, :]
    carry_ref[...] = carry
    cnt_ref[...] = carry


def _mix_heads(sl, om_ref, of_ref, gm_ref, gf_ref, wo_ref):
    a = _rms(om_ref[sl, :].astype(F32), gm_ref[...]).astype(BF16)
    b = _rms(of_ref[sl, :].astype(F32), gf_ref[...]).astype(BF16)
    return _dot(a, wo_ref[0:HEAD_W, :]) + _dot(b, wo_ref[HEAD_W:2 * HEAD_W, :])


def _residual_and_logits(sl, mixed, x_ref, gn_ref, wr_ref, br_ref, x1_ref, h2_ref):
    x1 = x_ref[sl, :] + mixed
    x1_ref[sl, :] = x1
    h2 = _rms(x1, gn_ref[...])
    h_hi = h2.astype(BF16)
    h2_ref[sl, :] = _pack_rows(h_hi.astype(F32))
    h_lo = (h2 - h_hi.astype(F32)).astype(BF16)
    t = _dot(h_hi, wr_ref[...])
    return (t[:, 0:LANES] + t[:, LANES:2 * LANES]) + _dot(h_lo, wr_ref[:, 0:LANES]) + br_ref[...]


def _route_rows(lg, tri, carry):
    lane = lax.broadcasted_iota(jnp.int32, lg.shape, 1)
    lane_f = lane.astype(F32)
    neg_inf = float("-inf")
    is_group = lane < N_GROUPS
    gl = jnp.where(is_group, lg, neg_inf)
    mg = _lane_max(gl)
    gi = _first_lane(gl == mg, lane_f)
    g_val = 1.0 / jnp.sum(jnp.where(is_group, jnp.exp(lg - mg), 0.0), axis=1, keepdims=True)
    group_of_lane = ((lane - N_GROUPS) >> 3).astype(F32)
    is_expert = (lane >= N_GROUPS) & (lane < N_GROUPS + N_EXPERTS) & (group_of_lane == gi)
    el = jnp.where(is_expert, lg, neg_inf)
    m1 = _lane_max(el)
    i1 = _first_lane(el == m1, lane_f)
    el2 = jnp.where(lane_f == i1, neg_inf, el)
    m2 = _lane_max(el2)
    i2 = _first_lane(el2 == m2, lane_f)
    r = jnp.exp(m2 - m1)
    g0 = g_val / (1.0 + r)
    g1 = g0 * r

    hit1 = lane_f == i1
    hit2 = lane_f == i2
    onehot = jnp.where(hit1 | hit2, 1.0, 0.0)
    before = _dot(tri, onehot.astype(BF16)) + carry
    r0 = jnp.sum(jnp.where(hit1, before, 0.0), axis=1, keepdims=True)
    r1 = jnp.sum(jnp.where(hit2, before, 0.0), axis=1, keepdims=True)

    vals = (i1 - N_GROUPS, i2 - N_GROUPS, g0, g1, r0, r1)
    rt = jnp.zeros_like(lg)
    for k, v in enumerate(vals):
        rt = jnp.where(lane == k, v, rt)
    return rt, carry + jnp.sum(onehot, axis=0, keepdims=True)


def _out_proj(om, of, x2d, gm, gf, wo, gn, wr, br, tri):
    n = x2d.shape[0]
    rows = tri.shape[0] * SUB_TILES
    row = lambda c: pl.BlockSpec((rows, c), lambda i: (i, 0))
    full = lambda a: pl.BlockSpec(a.shape, lambda i: (0,) * a.ndim)
    return pl.pallas_call(
        _out_proj_kernel,
        grid=(n // rows,),
        in_specs=[row(HEAD_W), row(HEAD_W), row(D_MODEL), full(gm), full(gf), full(wo), full(gn), full(wr),
                  full(br), full(tri)],
        out_specs=(row(D_MODEL), row(D_MODEL // 2), row(LANES), pl.BlockSpec((ROUTE_ROWS, rows), lambda i: (0, i)),
                   pl.BlockSpec((1, LANES), lambda i: (0, 0))),
        out_shape=(jax.ShapeDtypeStruct((n, D_MODEL), F32), jax.ShapeDtypeStruct((n, D_MODEL // 2), jnp.uint32),
                   jax.ShapeDtypeStruct((n, LANES), F32), jax.ShapeDtypeStruct((ROUTE_ROWS, n), F32),
                   jax.ShapeDtypeStruct((1, LANES), F32)),
        scratch_shapes=[pltpu.VMEM((1, LANES), F32)],
        compiler_params=_params(("arbitrary",)),
        name="out_proj",
    )(om, of, x2d, gm, gf, wo, gn, wr, br, tri)


def _gather_rows(table, idx):
    m, w = idx.shape[0], table.shape[1]
    workers = SC_CORES * SC_SUBCORES
    per_worker = m // workers
    n_win = per_worker // GATHER_WINDOW
    assert per_worker * workers == m and n_win * GATHER_WINDOW == per_worker and n_win % 2 == 0
    mesh = plsc.VectorSubcoreMesh(core_axis_name="core", subcore_axis_name="subcore")

    @functools.partial(
        pl.kernel, mesh=mesh, out_type=jax.ShapeDtypeStruct((m, w), table.dtype),
        scratch_types=[pltpu.VMEM((per_worker,), jnp.int32), pltpu.VMEM((2, GATHER_WINDOW, w), table.dtype),
                       pltpu.SemaphoreType.DMA((2,)), pltpu.SemaphoreType.DMA((2,))],
        name="gather_rows")
    def gather(table_hbm, idx_hbm, out_hbm, idx_v, rows_v, gather_sems, write_sems):
        base = (lax.axis_index("subcore") * SC_CORES + lax.axis_index("core")) * per_worker
        pltpu.sync_copy(idx_hbm.at[pl.ds(base, per_worker)], idx_v)

        def gather_copy(win, buf):
            rows = idx_v.at[pl.ds(win * GATHER_WINDOW, GATHER_WINDOW)]
            return pltpu.make_async_copy(table_hbm.at[rows], rows_v.at[buf], gather_sems.at[buf])

        def write_copy(win, buf):
            dst = out_hbm.at[pl.ds(base + win * GATHER_WINDOW, GATHER_WINDOW)]
            return pltpu.make_async_copy(rows_v.at[buf], dst, write_sems.at[buf])

        gather_copy(0, 0).start()

        @pl.loop(0, n_win, step=2)
        def _(j):
            for buf in range(2):
                win = j + buf
                gather_copy(win, buf).wait()
                write_copy(win, buf).start()

                @pl.when(win + 1 < n_win)
                def _():
                    @pl.when(win >= 1)
                    def _():
                        write_copy(win - 1, 1 - buf).wait()

                    gather_copy(win + 1, 1 - buf).start()

        write_copy(n_win - 2, 0).wait()
        write_copy(n_win - 1, 1).wait()

    return gather(table, idx)


def _scatter_rows(rows, dests, n_out):
    n, w = rows.shape
    workers = SC_CORES * SC_SUBCORES
    per_worker = n // workers
    n_win = per_worker // GATHER_WINDOW
    n_lists = len(dests)
    assert per_worker * workers == n and n_win * GATHER_WINDOW == per_worker and n_win % 2 == 0
    mesh = plsc.VectorSubcoreMesh(core_axis_name="core", subcore_axis_name="subcore")

    @functools.partial(
        pl.kernel, mesh=mesh, out_type=jax.ShapeDtypeStruct((n_out, w), rows.dtype),
        scratch_types=[pltpu.VMEM((per_worker,), jnp.int32)] * n_lists + [
            pltpu.VMEM((2, GATHER_WINDOW, w), rows.dtype), pltpu.SemaphoreType.DMA((2,)),
            pltpu.SemaphoreType.DMA((2,))],
        name="scatter_rows")
    def scatter(rows_hbm, *refs):
        dest_hbm, out_hbm = refs[:n_lists], refs[n_lists]
        dest_v = refs[n_lists + 1:2 * n_lists + 1]
        rows_v, read_sems, write_sems = refs[2 * n_lists + 1:]
        base = (lax.axis_index("subcore") * SC_CORES + lax.axis_index("core")) * per_worker
        for d_hbm, d_v in zip(dest_hbm, dest_v):
            pltpu.sync_copy(d_hbm.at[pl.ds(base, per_worker)], d_v)

        def read_copy(win, buf):
            src = rows_hbm.at[pl.ds(base + win * GATHER_WINDOW, GATHER_WINDOW)]
            return pltpu.make_async_copy(src, rows_v.at[buf], read_sems.at[buf])

        def write_copies(win, buf):
            return [pltpu.make_async_copy(rows_v.at[buf],
                                          out_hbm.at[d_v.at[pl.ds(win * GATHER_WINDOW, GATHER_WINDOW)]],
                                          write_sems.at[buf]) for d_v in dest_v]

        read_copy(0, 0).start()

        @pl.loop(0, n_win, step=2)
        def _(j):
            for buf in range(2):
                win = j + buf
                read_copy(win, buf).wait()
                for c in write_copies(win, buf):
                    c.start()

                @pl.when(win + 1 < n_win)
                def _():
                    @pl.when(win >= 1)
                    def _():
                        for c in write_copies(win - 1, 1 - buf):
                            c.wait()

                    read_copy(win + 1, 1 - buf).start()

        for c in write_copies(n_win - 2, 0) + write_copies(n_win - 1, 1):
            c.wait()

    return scatter(rows, *dests)


def _experts_kernel(be_ref, bv_ref, xs_ref, wg_ref, wu_ref, wd_ref, o_ref, wg_s, wu_s, wd_s):
    i = pl.program_id(0)
    valid = bv_ref[i] != 0

    @pl.when(jnp.logical_and(valid, jnp.logical_or(i == 0, be_ref[i] != be_ref[jnp.maximum(i - 1, 0)])))
    def _():
        wg_s[...] = wg_ref[0, 0].astype(BF16)
        wu_s[...] = wu_ref[0, 0].astype(BF16)
        wd_s[...] = wd_ref[0, 0].astype(BF16)

    @pl.when(valid)
    def _():
        row = lax.broadcasted_iota(jnp.int32, xs_ref.shape, 0)
        packed = jnp.where(row < bv_ref[i], xs_ref[...], jnp.uint32(0))
        subs = _sub_slices(EXPERT_TILE)
        gate_up = []
        for sl in subs:
            x = _unpack_rows(packed[sl, :]).astype(BF16)
            gate_up.append((_dot(x, wg_s[...]), _dot(x, wu_s[...])))
        for sl, (g, u) in zip(subs, gate_up):
            act = ((g * jax.nn.sigmoid(g)) * u).astype(BF16)
            o_ref[sl, :] = _pack_rows(_dot(act, wd_s[...]).astype(BF16).astype(F32))

    @pl.when(jnp.logical_not(valid))
    def _():
        o_ref[...] = jnp.zeros_like(o_ref)


def _experts(layer, blk_expert, blk_rows, xs, w_gate, w_up, w_down):
    n_slots = xs.shape[0]
    n_blocks = n_slots // EXPERT_TILE
    grid_spec = pltpu.PrefetchScalarGridSpec(
        num_scalar_prefetch=2,
        grid=(n_blocks,),
        in_specs=[pl.BlockSpec((EXPERT_TILE, D_MODEL // 2), lambda i, be, bv: (i, 0)),
                  pl.BlockSpec((1, 1, D_MODEL, D_EXPERT), lambda i, be, bv: (layer, be[i], 0, 0)),
                  pl.BlockSpec((1, 1, D_MODEL, D_EXPERT), lambda i, be, bv: (layer, be[i], 0, 0)),
                  pl.BlockSpec((1, 1, D_EXPERT, D_MODEL), lambda i, be, bv: (layer, be[i], 0, 0))],
        out_specs=pl.BlockSpec((EXPERT_TILE, D_MODEL // 2), lambda i, be, bv: (i, 0)),
        scratch_shapes=[pltpu.VMEM((D_MODEL, D_EXPERT), BF16), pltpu.VMEM((D_MODEL, D_EXPERT), BF16),
                        pltpu.VMEM((D_EXPERT, D_MODEL), BF16)],
    )
    return pl.pallas_call(
        _experts_kernel,
        grid_spec=grid_spec,
        out_shape=jax.ShapeDtypeStruct((n_slots, D_MODEL // 2), jnp.uint32),
        compiler_params=_params(("arbitrary",)),
        name="experts",
    )(blk_expert, blk_rows, xs, w_gate, w_up, w_down)


def _combine_kernel(x_ref, y0_ref, y1_ref, rt_ref, g_ref, o_ref, *, final):
    g0 = rt_ref[:, TOP_K:TOP_K + 1]
    g1 = rt_ref[:, TOP_K + 1:TOP_K + 2]
    x = x_ref[...] + (_unpack_rows(y0_ref[...]) * g0 + _unpack_rows(y1_ref[...]) * g1)
    o_ref[...] = _rms(x, g_ref[...]) if final else x


def _combine(x1, y01, route, g, final):
    n = x1.shape[0]
    rows = min(ROW_TILE, n)
    steps = n // rows
    row = pl.BlockSpec((rows, D_MODEL), lambda i: (i, 0))
    return pl.pallas_call(
        functools.partial(_combine_kernel, final=final),
        grid=(steps,),
        in_specs=[row, pl.BlockSpec((rows, D_MODEL // 2), lambda i: (i, 0)),
                  pl.BlockSpec((rows, D_MODEL // 2), lambda i: (i + steps, 0)),
                  pl.BlockSpec((rows, LANES), lambda i: (i, 0)),
                  pl.BlockSpec((1, D_MODEL), lambda i: (0, 0))],
        out_specs=row,
        out_shape=jax.ShapeDtypeStruct((n, D_MODEL), F32),
        compiler_params=_params(("parallel",)),
        name="combine",
    )(x1, y01, y01, route, g)


def _swap_halves(w):
    half = w.shape[-1] // 2
    return jnp.concatenate([w[..., half:], w[..., :half]], axis=-1)


def _prep_in_weights(w_in):
    src = np.full((IN_COLS,), -1, np.int32)
    src[0:1536] = np.arange(416, 1952)
    src[1536:1920] = np.arange(0, 384)
    src[1920:1920 + MLA_ROPE] = np.arange(384, 416)
    src[1920 + MLA_ROPE:1920 + MLA_ROPE + FOX_HEADS] = np.arange(1952, 1960)
    scale = np.where(np.arange(IN_COLS) < HEAD_W, FOX_DIM ** -0.5, 1.0).astype(np.float32)
    rows = lax.broadcasted_iota(jnp.int32, (w_in.shape[1], IN_COLS), 0)
    select = jnp.where(rows == jnp.asarray(src)[None, :], jnp.asarray(scale)[None, :], 0.0).astype(BF16)
    return jnp.dot(w_in.astype(BF16), select, preferred_element_type=F32).astype(BF16)


def _prep_uq(w_uq):
    w = w_uq.reshape(MLA_Q_RANK, MLA_HEADS, MLA_NOPE + MLA_ROPE)
    nope, pe = w[:, :, :MLA_NOPE], w[:, :, MLA_NOPE:]
    pairs = nope.reshape(MLA_Q_RANK, MLA_HEADS // 2, 2, MLA_NOPE)[:, :, ::-1, :].reshape(MLA_Q_RANK, -1)
    return jnp.concatenate([pairs, pe.reshape(MLA_Q_RANK, -1), _swap_halves(pe).reshape(MLA_Q_RANK, -1)],
                           axis=1).astype(BF16)


def _prep_ukv(w_ukv):
    w = w_ukv.reshape(MLA_KV_RANK, MLA_HEADS // 2, 2, 2, MLA_NOPE)
    even = w[:, :, 0, ::-1, :]
    odd = w[:, :, 1, :, :]
    return jnp.stack([even, odd], axis=2).reshape(MLA_KV_RANK, -1).astype(BF16)


def _prep_router(w_rg, b_rg, w_re, b_re):
    d = w_rg.shape[0]
    w = jnp.concatenate([w_rg, w_re, jnp.zeros((d, LANES - N_GROUPS - N_EXPERTS), F32)], axis=1)
    w_hi = w.astype(BF16)
    w_lo = (w - w_hi.astype(F32)).astype(BF16)
    b = jnp.concatenate([b_rg, b_re, jnp.zeros((LANES - N_GROUPS - N_EXPERTS,), F32)])[None, :]
    return jnp.concatenate([w_hi, w_lo], axis=1), b


def _rope_slabs(positions):
    half = MLA_ROPE // 2
    inv_freq = ROPE_THETA ** (-jnp.arange(half, dtype=F32) / half)
    ang = positions.astype(F32).reshape(-1)[:, None] * inv_freq
    cos, sin = jnp.cos(ang), jnp.sin(ang)
    reps = LANES // MLA_ROPE
    return jnp.tile(jnp.concatenate([cos, cos], axis=1), (1, reps)), jnp.tile(
        jnp.concatenate([-sin, sin], axis=1), (1, reps))


def _slot_layout(route_t, counts, n_blocks):
    eid = route_t[0:TOP_K].astype(jnp.int32)
    rank = route_t[2 * TOP_K:3 * TOP_K].astype(jnp.int32)
    counts = counts[0, N_GROUPS:N_GROUPS + N_EXPERTS].astype(jnp.int32)
    padded = (counts + EXPERT_TILE - 1) // EXPERT_TILE * EXPERT_TILE
    pad_end = jnp.cumsum(padded)
    pad_start = pad_end - padded
    experts = jnp.arange(N_EXPERTS, dtype=jnp.int32)
    dest = jnp.sum(jnp.where(eid[None] == experts[:, None, None], pad_start[:, None, None], 0), axis=0) + rank
    blk_start = jnp.arange(n_blocks, dtype=jnp.int32) * EXPERT_TILE
    blk_expert = jnp.minimum(jnp.sum((blk_start[:, None] >= pad_end[None, :]).astype(jnp.int32), axis=1),
                             N_EXPERTS - 1)
    in_expert = jnp.sum(jnp.where(blk_expert[:, None] == experts, counts + pad_start, 0), axis=-1) - blk_start
    blk_rows = jnp.where(blk_start < pad_end[-1], jnp.clip(in_expert, 0, EXPERT_TILE), 0).astype(jnp.int32)
    return dest, blk_expert, blk_rows


def kernel(x, positions, attn_norm, w_in, b_forget, q_norm, w_uq, kv_norm, w_ukv, mla_out_norm, fox_out_norm,
           w_out, ffn_norm, w_router_group, b_router_group, w_router_expert, b_router_expert, w_gate, w_up,
           w_down, final_norm):
    batch, seq, d = x.shape
    n = batch * seq
    depth = w_in.shape[0]
    n_blocks = -(-(n * TOP_K) // EXPERT_TILE) + N_EXPERTS
    cc, ss = _rope_slabs(positions)
    tri = (jnp.arange(CUM_TILE)[:, None] <= jnp.arange(CUM_TILE)[None, :]).astype(BF16)
    sub = min(ROW_TILE, n) // SUB_TILES
    tri_rows = (jnp.arange(sub)[None, :] < jnp.arange(sub)[:, None]).astype(BF16)
    xf = x.reshape(n, d)
    for l in range(depth):
        fq, fk, fv, qn, qpe, kv, kpe, flt = _in_proj(
            xf, attn_norm[l][None, :], _prep_in_weights(w_in[l]), q_norm[l][None, :], _prep_uq(w_uq[l]),
            kv_norm[l][None, :], _prep_ukv(w_ukv[l]), cc, ss)
        ck = _fox_decay(flt, b_forget[l][:, None], tri, batch, seq)
        o_mla = _mla_attn(qn, qpe, kv, kpe, batch, seq).reshape(n, HEAD_W)
        o_fox = _fox_attn(fq, fk, fv, ck, batch, seq).reshape(n, HEAD_W)
        wr, br = _prep_router(w_router_group[l], b_router_group[l], w_router_expert[l], b_router_expert[l])
        x1, h2, route, route_t, counts = _out_proj(
            o_mla, o_fox, xf, mla_out_norm[l][None, :], fox_out_norm[l][None, :], w_out[l].astype(BF16),
            ffn_norm[l][None, :], wr, br, tri_rows)
        dest, blk_expert, blk_rows = _slot_layout(route_t, counts, n_blocks)
        xs = _scatter_rows(h2, [dest[k] for k in range(TOP_K)], n_blocks * EXPERT_TILE)
        ys = _experts(l, blk_expert, blk_rows, xs, w_gate, w_up, w_down)
        y01 = _gather_rows(ys, dest.reshape(-1))
        final = l == depth - 1
        xf = _combine(x1, y01, route, final_norm[None, :] if final else ffn_norm[l][None, :], final)
    return xf.reshape(batch, seq, d)
```

```python
import functools

import jax
import numpy as np
import jax.numpy as jnp
from jax import lax
from jax.experimental import pallas as pl
from jax.experimental.pallas import tpu as pltpu
from jax.experimental.pallas import tpu_sc as plsc

D_MODEL = 1024
CHUNK = 64
MLA_HEADS = 8
MLA_NOPE = 64
MLA_ROPE = 32
MLA_V = 64
MLA_Q_RANK = 256
MLA_KV_RANK = 128
ROPE_THETA = 10000.0
FOX_HEADS = 8
FOX_DIM = 64
HEAD_W = 512
N_GROUPS = 4
EXPERTS_PER_GROUP = 8
N_EXPERTS = 32
TOP_K = 2
D_EXPERT = 512
NORM_EPS = 1e-6

LANES = 128
IN_COLS = 2048
ROW_TILE = 512
SUB_TILES = 2
Q_TILE = 256
PAIRS_PER_STEP = 2
EXPERT_TILE = 512
CUM_TILE = 256
NEG_BIG = -1e30
LOG2E = 1.4426950408889634
VMEM_LIMIT = 48 * 1024 * 1024
SC_CORES = 2
SC_SUBCORES = 16
GATHER_WINDOW = 64
ROUTE_ROWS = 8

F32 = jnp.float32
BF16 = jnp.bfloat16


def _rms(x, g):
    return (x * lax.rsqrt(jnp.mean(x * x, axis=-1, keepdims=True) + NORM_EPS)) * g


def _dot(a, b):
    return jnp.dot(a, b, preferred_element_type=F32)


def _dot_nt(a, b):
    return lax.dot_general(a, b, (((1,), (1,)), ((), ())), preferred_element_type=F32)


def _params(sem):
    return pltpu.CompilerParams(dimension_semantics=sem, vmem_limit_bytes=VMEM_LIMIT)


def _sub_slices(rows):
    sub = rows // SUB_TILES
    return [slice(t * sub, (t + 1) * sub) for t in range(SUB_TILES)]


def _pack_rows(v):
    w = v.shape[1] // 2
    lo = lax.bitcast_convert_type(v[:, :w], jnp.uint32) >> 16
    hi = lax.bitcast_convert_type(v[:, w:], jnp.uint32) & jnp.uint32(0xFFFF0000)
    return hi | lo


def _unpack_rows(p):
    lo = lax.bitcast_convert_type(p << 16, F32)
    hi = lax.bitcast_convert_type(p & jnp.uint32(0xFFFF0000), F32)
    return jnp.concatenate([lo, hi], axis=1)


def _swap_rope_halves(x):
    half = MLA_ROPE // 2
    tiles = []
    for c in range(x.shape[1] // LANES):
        t = x[:, c * LANES:(c + 1) * LANES]
        lane = lax.broadcasted_iota(jnp.int32, t.shape, 1)
        tiles.append(jnp.where(lane % MLA_ROPE < half, pltpu.roll(t, LANES - half, 1), pltpu.roll(t, half, 1)))
    return tiles[0] if len(tiles) == 1 else jnp.concatenate(tiles, axis=1)


def _in_proj_kernel(x_ref, g_ref, w_ref, gq_ref, wuq_ref, gkv_ref, wukv_ref, cc_ref, ss_ref,
                    fq_ref, fk_ref, fv_ref, qn_ref, qpe_ref, kv_ref, kpe_ref, flt_ref):
    subs = _sub_slices(x_ref.shape[0])
    ps = [_dot(_rms(x_ref[sl, :], g_ref[...]).astype(BF16), w_ref[...]) for sl in subs]
    scale = (MLA_NOPE + MLA_ROPE) ** -0.5 * LOG2E
    for sl, p in zip(subs, ps):
        fq_ref[sl, :] = (p[:, 0:512] * LOG2E).astype(BF16)
        fk_ref[sl, :] = p[:, 512:1024].astype(BF16)
        fv_ref[sl, :] = p[:, 1024:1536].astype(BF16)

        cc = cc_ref[sl, :]
        ss = ss_ref[sl, :]
        q = _dot(_rms(p[:, 1536:1792], gq_ref[...]).astype(BF16), wuq_ref[...])
        qn_ref[sl, :] = (q[:, 0:512] * scale).astype(BF16)
        cc2 = jnp.concatenate([cc, cc], axis=1)
        ss2 = jnp.concatenate([ss, ss], axis=1)
        qpe_ref[sl, :] = ((q[:, 512:768] * cc2 + q[:, 768:1024] * ss2) * scale).astype(BF16)

        kv_ref[sl, :] = _dot(_rms(p[:, 1792:1920], gkv_ref[...]).astype(BF16), wukv_ref[...]).astype(BF16)

        slab_a = p[:, 1920:2048]
        lane = lax.broadcasted_iota(jnp.int32, slab_a.shape, 1)
        roped = jnp.where(lane < MLA_ROPE, slab_a * cc + _swap_rope_halves(slab_a) * ss, 0.0)
        tiled = roped + pltpu.roll(roped, 32, 1) + pltpu.roll(roped, 64, 1) + pltpu.roll(roped, 96, 1)
        kpe_ref[sl, :] = tiled.astype(BF16)
        flt_ref[:, sl] = slab_a.T[MLA_ROPE:MLA_ROPE + FOX_HEADS, :]


def _in_proj(x2d, g, w, gq, wuq, gkv, wukv, cc, ss):
    n = x2d.shape[0]
    rows = min(ROW_TILE, n)
    row = lambda c: pl.BlockSpec((rows, c), lambda i: (i, 0))
    full = lambda a: pl.BlockSpec(a.shape, lambda i: (0,) * a.ndim)
    out_shape = (
        jax.ShapeDtypeStruct((n, HEAD_W), BF16), jax.ShapeDtypeStruct((n, HEAD_W), BF16),
        jax.ShapeDtypeStruct((n, HEAD_W), BF16), jax.ShapeDtypeStruct((n, HEAD_W), BF16),
        jax.ShapeDtypeStruct((n, 256), BF16), jax.ShapeDtypeStruct((n, 1024), BF16),
        jax.ShapeDtypeStruct((n, LANES), BF16), jax.ShapeDtypeStruct((FOX_HEADS, n), F32),
    )
    return pl.pallas_call(
        _in_proj_kernel,
        grid=(n // rows,),
        in_specs=[row(D_MODEL), full(g), full(w), full(gq), full(wuq), full(gkv), full(wukv),
                  row(LANES), row(LANES)],
        out_specs=(row(HEAD_W), row(HEAD_W), row(HEAD_W), row(HEAD_W), row(256), row(1024), row(LANES),
                   pl.BlockSpec((FOX_HEADS, rows), lambda i: (0, i))),
        out_shape=out_shape,
        compiler_params=_params(("parallel",)),
        name="in_proj",
    )(x2d, g, w, gq, wuq, gkv, wukv, cc, ss)


def _fox_decay_kernel(fl_ref, b_ref, tri_ref, ck_ref):
    z = fl_ref[...] + b_ref[...]
    lf = jnp.minimum(z, 0.0) - jnp.log1p(jnp.exp(-jnp.abs(z)))
    seq = lf.shape[1]
    tri = tri_ref[...]
    carry = jnp.zeros((FOX_HEADS, 1), F32)
    zeros = jnp.zeros((FOX_HEADS, CUM_TILE), F32)
    for j in range(seq // CUM_TILE):
        v = lf[:, j * CUM_TILE:(j + 1) * CUM_TILE]
        hi = v.astype(BF16).astype(F32)
        r1 = v - hi
        mid = r1.astype(BF16).astype(F32)
        lo = r1 - mid
        parts = _dot(jnp.concatenate([hi, mid, lo, zeros], axis=0).astype(BF16), tri)
        cs = (parts[0:8] + parts[8:16]) + parts[16:24] + carry
        carry = cs[:, CUM_TILE - 1:CUM_TILE]
        d = cs * (-LOG2E)
        d_hi = d.astype(BF16).astype(F32)
        d_r = d - d_hi
        d_mid = d_r.astype(BF16).astype(F32)
        d_lo = d_r - d_mid
        rows = jnp.concatenate([d_hi, d_mid, d_lo, jnp.zeros((LANES - 3 * FOX_HEADS, CUM_TILE), F32)], axis=0)
        ck_ref[0, j * CUM_TILE:(j + 1) * CUM_TILE, :] = rows.T.astype(BF16)


def _fox_decay(flt, b_col, tri, batch, seq):
    return pl.pallas_call(
        _fox_decay_kernel,
        grid=(batch,),
        in_specs=[pl.BlockSpec((FOX_HEADS, seq), lambda b: (0, b)),
                  pl.BlockSpec((FOX_HEADS, 1), lambda b: (0, 0)),
                  pl.BlockSpec((CUM_TILE, CUM_TILE), lambda b: (0, 0))],
        out_specs=pl.BlockSpec((1, seq, LANES), lambda b: (b, 0, 0)),
        out_shape=jax.ShapeDtypeStruct((batch, seq, LANES), BF16),
        compiler_params=_params(("parallel",)),
        name="fox_decay",
    )(flt, b_col, tri)


def _probs(s_off, s_diag):
    m = jnp.max(s_diag, axis=-1, keepdims=True)
    if s_off is not None:
        m = jnp.maximum(m, jnp.max(s_off, axis=-1, keepdims=True))
    p_off = None if s_off is None else jnp.exp2(s_off - m).astype(BF16)
    return p_off, jnp.exp2(s_diag - m).astype(BF16)


def _attend(q_ref, k_ref, v_ref, o_ref, out_lane0, allowed):
    seq = q_ref.shape[1]
    out_lane = lax.broadcasted_iota(jnp.int32, (Q_TILE, LANES), 1)
    units = [(i, hh) for i in range(seq // Q_TILE) for hh in range(2)]

    def scores(i, hh):
        qs, qe = i * Q_TILE, (i + 1) * Q_TILE
        q = q_ref[hh, qs:qe, :]
        kh = hh % k_ref.shape[0]
        s_diag = jnp.where(allowed, _dot_nt(q, k_ref[kh, qs:qe, :]), NEG_BIG)
        s_off = _dot_nt(q, k_ref[kh, 0:qs, :]) if i > 0 else None
        return s_off, s_diag

    def values(i, hh, p_off, p_diag):
        qs, qe = i * Q_TILE, (i + 1) * Q_TILE
        vh = hh % v_ref.shape[0]
        acc = _dot(p_diag, v_ref[vh, qs:qe, :])
        if p_off is not None:
            acc = acc + _dot(p_off, v_ref[vh, 0:qs, :])
        return acc[:, 0:LANES] / acc[:, LANES:2 * LANES]

    n_units = len(units)
    s = {0: scores(*units[0])}
    if n_units > 1:
        s[1] = scores(*units[1])
    p = {0: _probs(*s.pop(0))}
    outs = []
    for n, (i, hh) in enumerate(units):
        if n + 2 < n_units:
            s[n + 2] = scores(*units[n + 2])
        if n + 1 < n_units:
            p[n + 1] = _probs(*s.pop(n + 1))
        outs.append(values(i, hh, *p.pop(n)))
        if hh == 1:
            o_ref[0, i * Q_TILE:(i + 1) * Q_TILE, out_lane0:out_lane0 + LANES] = jnp.where(
                out_lane < 64, outs[0], outs[1]).astype(BF16)
            outs = []


def _tile_iota():
    row = lax.broadcasted_iota(jnp.int32, (Q_TILE, Q_TILE), 0)
    col = lax.broadcasted_iota(jnp.int32, (Q_TILE, Q_TILE), 1)
    return row, col


def _mla_attn_kernel(qn_ref, qpe_ref, kv_ref, kpe_ref, o_ref, qs_all, ks_all, vs_all):
    seq = qn_ref.shape[1]
    lane = lax.broadcasted_iota(jnp.int32, (seq, LANES), 1)
    qpe = qpe_ref[0]
    kpe = kpe_ref[0]
    ones = jnp.ones((seq, LANES), BF16)
    row, col = _tile_iota()
    for pp in range(PAIRS_PER_STEP):
        qs_ref, ks_ref, vs_ref = qs_all.at[pp], ks_all.at[pp], vs_all.at[pp]
        qn = qn_ref[0, :, pp * LANES:(pp + 1) * LANES]
        for hh in range(2):
            nope_mask = (lane >= 64) if hh == 0 else (lane < 64)
            pe_mask = (lane // MLA_ROPE) == 2 * pp + hh
            qs_ref[hh, :, 0:LANES] = jnp.where(nope_mask, qn, jnp.zeros_like(qn))
            qs_ref[hh, :, LANES:2 * LANES] = jnp.where(pe_mask, qpe, jnp.zeros_like(qpe))
            kvh = kv_ref[0, :, (2 * pp + hh) * LANES:(2 * pp + hh + 1) * LANES]
            ks_ref[hh, :, 0:LANES] = kvh
            ks_ref[hh, :, LANES:2 * LANES] = kpe
            vs_ref[hh, :, 0:LANES] = kvh
            vs_ref[hh, :, LANES:2 * LANES] = ones
        _attend(qs_ref, ks_ref, vs_ref, o_ref, pp * LANES, (col // CHUNK) <= (row // CHUNK))


def _mla_attn(qn, qpe, kv, kpe, batch, seq):
    qn, qpe, kv, kpe = (a.reshape(batch, seq, a.shape[-1]) for a in (qn, qpe, kv, kpe))
    pair = pltpu.VMEM((PAIRS_PER_STEP, 2, seq, 2 * LANES), BF16)
    width = PAIRS_PER_STEP * LANES
    return pl.pallas_call(
        _mla_attn_kernel,
        grid=(batch, MLA_HEADS // (2 * PAIRS_PER_STEP)),
        in_specs=[pl.BlockSpec((1, seq, width), lambda b, j: (b, 0, j)),
                  pl.BlockSpec((1, seq, LANES), lambda b, j: (b, 0, j)),
                  pl.BlockSpec((1, seq, 2 * width), lambda b, j: (b, 0, j)),
                  pl.BlockSpec((1, seq, LANES), lambda b, j: (b, 0, 0))],
        out_specs=pl.BlockSpec((1, seq, width), lambda b, j: (b, 0, j)),
        out_shape=jax.ShapeDtypeStruct((batch, seq, HEAD_W), BF16),
        scratch_shapes=[pair, pair, pair],
        compiler_params=_params(("parallel", "parallel")),
        name="mla_attn",
    )(qn, qpe, kv, kpe)


def _fox_attn_kernel(q_ref, k_ref, v_ref, ck_ref, o_ref, qs_all, ks_all, vs_all):
    j = pl.program_id(1)
    seq = q_ref.shape[1]
    lane = lax.broadcasted_iota(jnp.int32, (seq, LANES), 1)
    row, col = _tile_iota()
    for pp in range(PAIRS_PER_STEP):
        qs_ref, ks_ref, vs_ref = qs_all.at[pp], ks_all.at[pp], vs_all.at[pp]
        chunk = slice(pp * LANES, (pp + 1) * LANES)
        q = q_ref[0, :, chunk]
        ks_ref[0, :, 0:LANES] = k_ref[0, :, chunk]
        ks_ref[0, :, LANES:2 * LANES] = ck_ref[0]
        vs_ref[0, :, 0:LANES] = v_ref[0, :, chunk]
        vs_ref[0, :, LANES:2 * LANES] = jnp.ones((seq, LANES), BF16)
        for hh in range(2):
            head = 2 * (PAIRS_PER_STEP * j + pp) + hh
            head_mask = (lane < 64) if hh == 0 else (lane >= 64)
            piece = jnp.where(lane < 3 * FOX_HEADS, lane % FOX_HEADS, -1) == head
            qs_ref[hh, :, 0:LANES] = jnp.where(head_mask, q, jnp.zeros_like(q))
            qs_ref[hh, :, LANES:2 * LANES] = jnp.where(piece, 1.0, 0.0).astype(BF16)
        _attend(qs_ref, ks_ref, vs_ref, o_ref, pp * LANES, col <= row)


def _fox_attn(fq, fk, fv, ck, batch, seq):
    fq, fk, fv = (a.reshape(batch, seq, HEAD_W) for a in (fq, fk, fv))
    spec = pl.BlockSpec((1, seq, PAIRS_PER_STEP * LANES), lambda b, j: (b, 0, j))
    pair = pltpu.VMEM((PAIRS_PER_STEP, 2, seq, 2 * LANES), BF16)
    shared = pltpu.VMEM((PAIRS_PER_STEP, 1, seq, 2 * LANES), BF16)
    return pl.pallas_call(
        _fox_attn_kernel,
        grid=(batch, FOX_HEADS // (2 * PAIRS_PER_STEP)),
        in_specs=[spec, spec, spec, pl.BlockSpec((1, seq, LANES), lambda b, j: (b, 0, 0))],
        out_specs=spec,
        out_shape=jax.ShapeDtypeStruct((batch, seq, HEAD_W), BF16),
        scratch_shapes=[pair, shared, shared],
        compiler_params=_params(("parallel", "parallel")),
        name="fox_attn",
    )(fq, fk, fv, ck)


def _lane_max(v):
    return jnp.max(v, axis=1, keepdims=True)


def _first_lane(hit, lane_f):
    return jnp.min(jnp.where(hit, lane_f, float(LANES)), axis=1, keepdims=True)


def _out_proj_kernel(om_ref, of_ref, x_ref, gm_ref, gf_ref, wo_ref, gn_ref, wr_ref, br_ref, tri_ref,
                     x1_ref, h2_ref, rt_ref, rtt_ref, cnt_ref, carry_ref):
    @pl.when(pl.program_id(0) == 0)
    def _():
        carry_ref[...] = jnp.zeros_like(carry_ref)

    subs = _sub_slices(x_ref.shape[0])
    mixed = [_mix_heads(sl, om_ref, of_ref, gm_ref, gf_ref, wo_ref) for sl in subs]
    logits = [_residual_and_logits(sl, m, x_ref, gn_ref, wr_ref, br_ref, x1_ref, h2_ref) for sl, m in zip(subs, mixed)]
    carry = carry_ref[...]
    for sl, lg in zip(subs, logits):
        rt, carry = _route_rows(lg, tri_ref[...], carry)
        rt_ref[sl, :] = rt
        rtt_ref[:, sl] = rt.T[0:ROUTE_ROWS, :]
    carry_ref[...] = carry
    cnt_ref[...] = carry


def _mix_heads(sl, om_ref, of_ref, gm_ref, gf_ref, wo_ref):
    a = _rms(om_ref[sl, :].astype(F32), gm_ref[...]).astype(BF16)
    b = _rms(of_ref[sl, :].astype(F32), gf_ref[...]).astype(BF16)
    return _dot(a, wo_ref[0:HEAD_W, :]) + _dot(b, wo_ref[HEAD_W:2 * HEAD_W, :])


def _residual_and_logits(sl, mixed, x_ref, gn_ref, wr_ref, br_ref, x1_ref, h2_ref):
    x1 = x_ref[sl, :] + mixed
    x1_ref[sl, :] = x1
    h2 = _rms(x1, gn_ref[...])
    h_hi = h2.astype(BF16)
    h2_ref[sl, :] = _pack_rows(h_hi.astype(F32))
    h_lo = (h2 - h_hi.astype(F32)).astype(BF16)
    t = _dot(h_hi, wr_ref[...])
    return (t[:, 0:LANES] + t[:, LANES:2 * LANES]) + _dot(h_lo, wr_ref[:, 0:LANES]) + br_ref[...]


def _route_rows(lg, tri, carry):
    lane = lax.broadcasted_iota(jnp.int32, lg.shape, 1)
    lane_f = lane.astype(F32)
    neg_inf = float("-inf")
    is_group = lane < N_GROUPS
    gl = jnp.where(is_group, lg, neg_inf)
    mg = _lane_max(gl)
    gi = _first_lane(gl == mg, lane_f)
    g_val = 1.0 / jnp.sum(jnp.where(is_group, jnp.exp(lg - mg), 0.0), axis=1, keepdims=True)
    group_of_lane = ((lane - N_GROUPS) >> 3).astype(F32)
    is_expert = (lane >= N_GROUPS) & (lane < N_GROUPS + N_EXPERTS) & (group_of_lane == gi)
    el = jnp.where(is_expert, lg, neg_inf)
    m1 = _lane_max(el)
    i1 = _first_lane(el == m1, lane_f)
    el2 = jnp.where(lane_f == i1, neg_inf, el)
    m2 = _lane_max(el2)
    i2 = _first_lane(el2 == m2, lane_f)
    r = jnp.exp(m2 - m1)
    g0 = g_val / (1.0 + r)
    g1 = g0 * r

    hit1 = lane_f == i1
    hit2 = lane_f == i2
    onehot = jnp.where(hit1 | hit2, 1.0, 0.0)
    before = _dot(tri, onehot.astype(BF16)) + carry
    r0 = jnp.sum(jnp.where(hit1, before, 0.0), axis=1, keepdims=True)
    r1 = jnp.sum(jnp.where(hit2, before, 0.0), axis=1, keepdims=True)

    vals = (i1 - N_GROUPS, i2 - N_GROUPS, g0, g1, r0, r1)
    rt = jnp.zeros_like(lg)
    for k, v in enumerate(vals):
        rt = jnp.where(lane == k, v, rt)
    return rt, carry + jnp.sum(onehot, axis=0, keepdims=True)


def _out_proj(om, of, x2d, gm, gf, wo, gn, wr, br, tri):
    n = x2d.shape[0]
    rows = tri.shape[0] * SUB_TILES
    row = lambda c: pl.BlockSpec((rows, c), lambda i: (i, 0))
    full = lambda a: pl.BlockSpec(a.shape, lambda i: (0,) * a.ndim)
    return pl.pallas_call(
        _out_proj_kernel,
        grid=(n // rows,),
        in_specs=[row(HEAD_W), row(HEAD_W), row(D_MODEL), full(gm), full(gf), full(wo), full(gn), full(wr),
                  full(br), full(tri)],
        out_specs=(row(D_MODEL), row(D_MODEL // 2), row(LANES), pl.BlockSpec((ROUTE_ROWS, rows), lambda i: (0, i)),
                   pl.BlockSpec((1, LANES), lambda i: (0, 0))),
        out_shape=(jax.ShapeDtypeStruct((n, D_MODEL), F32), jax.ShapeDtypeStruct((n, D_MODEL // 2), jnp.uint32),
                   jax.ShapeDtypeStruct((n, LANES), F32), jax.ShapeDtypeStruct((ROUTE_ROWS, n), F32),
                   jax.ShapeDtypeStruct((1, LANES), F32)),
        scratch_shapes=[pltpu.VMEM((1, LANES), F32)],
        compiler_params=_params(("arbitrary",)),
        name="out_proj",
    )(om, of, x2d, gm, gf, wo, gn, wr, br, tri)


def _gather_rows(table, idx):
    m, w = idx.shape[0], table.shape[1]
    workers = SC_CORES * SC_SUBCORES
    per_worker = m // workers
    n_win = per_worker // GATHER_WINDOW
    assert per_worker * workers == m and n_win * GATHER_WINDOW == per_worker and n_win % 2 == 0
    mesh = plsc.VectorSubcoreMesh(core_axis_name="core", subcore_axis_name="subcore")

    @functools.partial(
        pl.kernel, mesh=mesh, out_type=jax.ShapeDtypeStruct((m, w), table.dtype),
        scratch_types=[pltpu.VMEM((per_worker,), jnp.int32), pltpu.VMEM((2, GATHER_WINDOW, w), table.dtype),
                       pltpu.SemaphoreType.DMA((2,)), pltpu.SemaphoreType.DMA((2,))],
        name="gather_rows")
    def gather(table_hbm, idx_hbm, out_hbm, idx_v, rows_v, gather_sems, write_sems):
        base = (lax.axis_index("subcore") * SC_CORES + lax.axis_index("core")) * per_worker
        pltpu.sync_copy(idx_hbm.at[pl.ds(base, per_worker)], idx_v)

        def gather_copy(win, buf):
            rows = idx_v.at[pl.ds(win * GATHER_WINDOW, GATHER_WINDOW)]
            return pltpu.make_async_copy(table_hbm.at[rows], rows_v.at[buf], gather_sems.at[buf])

        def write_copy(win, buf):
            dst = out_hbm.at[pl.ds(base + win * GATHER_WINDOW, GATHER_WINDOW)]
            return pltpu.make_async_copy(rows_v.at[buf], dst, write_sems.at[buf])

        gather_copy(0, 0).start()

        @pl.loop(0, n_win, step=2)
        def _(j):
            for buf in range(2):
                win = j + buf
                gather_copy(win, buf).wait()
                write_copy(win, buf).start()

                @pl.when(win + 1 < n_win)
                def _():
                    @pl.when(win >= 1)
                    def _():
                        write_copy(win - 1, 1 - buf).wait()

                    gather_copy(win + 1, 1 - buf).start()

        write_copy(n_win - 2, 0).wait()
        write_copy(n_win - 1, 1).wait()

    return gather(table, idx)


def _scatter_rows(rows, dests, n_out):
    n, w = rows.shape
    workers = SC_CORES * SC_SUBCORES
    per_worker = n // workers
    n_win = per_worker // GATHER_WINDOW
    n_lists = len(dests)
    assert per_worker * workers == n and n_win * GATHER_WINDOW == per_worker and n_win % 2 == 0
    mesh = plsc.VectorSubcoreMesh(core_axis_name="core", subcore_axis_name="subcore")

    @functools.partial(
        pl.kernel, mesh=mesh, out_type=jax.ShapeDtypeStruct((n_out, w), rows.dtype),
        scratch_types=[pltpu.VMEM((per_worker,), jnp.int32)] * n_lists + [
            pltpu.VMEM((2, GATHER_WINDOW, w), rows.dtype), pltpu.SemaphoreType.DMA((2,)),
            pltpu.SemaphoreType.DMA((2,))],
        name="scatter_rows")
    def scatter(rows_hbm, *refs):
        dest_hbm, out_hbm = refs[:n_lists], refs[n_lists]
        dest_v = refs[n_lists + 1:2 * n_lists + 1]
        rows_v, read_sems, write_sems = refs[2 * n_lists + 1:]
        base = (lax.axis_index("subcore") * SC_CORES + lax.axis_index("core")) * per_worker
        for d_hbm, d_v in zip(dest_hbm, dest_v):
            pltpu.sync_copy(d_hbm.at[pl.ds(base, per_worker)], d_v)

        def read_copy(win, buf):
            src = rows_hbm.at[pl.ds(base + win * GATHER_WINDOW, GATHER_WINDOW)]
            return pltpu.make_async_copy(src, rows_v.at[buf], read_sems.at[buf])

        def write_copies(win, buf):
            return [pltpu.make_async_copy(rows_v.at[buf],
                                          out_hbm.at[d_v.at[pl.ds(win * GATHER_WINDOW, GATHER_WINDOW)]],
                                          write_sems.at[buf]) for d_v in dest_v]

        read_copy(0, 0).start()

        @pl.loop(0, n_win, step=2)
        def _(j):
            for buf in range(2):
                win = j + buf
                read_copy(win, buf).wait()
                for c in write_copies(win, buf):
                    c.start()

                @pl.when(win + 1 < n_win)
                def _():
                    @pl.when(win >= 1)
                    def _():
                        for c in write_copies(win - 1, 1 - buf):
                            c.wait()

                    read_copy(win + 1, 1 - buf).start()

        for c in write_copies(n_win - 2, 0) + write_copies(n_win - 1, 1):
            c.wait()

    return scatter(rows, *dests)


def _experts_kernel(be_ref, bv_ref, xs_ref, wg_ref, wu_ref, wd_ref, o_ref, wg_s, wu_s, wd_s):
    i = pl.program_id(0)
    valid = bv_ref[i] != 0

    @pl.when(jnp.logical_and(valid, jnp.logical_or(i == 0, be_ref[i] != be_ref[jnp.maximum(i - 1, 0)])))
    def _():
        wg_s[...] = wg_ref[0, 0].astype(BF16)
        wu_s[...] = wu_ref[0, 0].astype(BF16)
        wd_s[...] = wd_ref[0, 0].astype(BF16)

    @pl.when(valid)
    def _():
        row = lax.broadcasted_iota(jnp.int32, xs_ref.shape, 0)
        packed = jnp.where(row < bv_ref[i], xs_ref[...], jnp.uint32(0))
        subs = _sub_slices(EXPERT_TILE)
        gate_up = []
        for sl in subs:
            x = _unpack_rows(packed[sl, :]).astype(BF16)
            gate_up.append((_dot(x, wg_s[...]), _dot(x, wu_s[...])))
        for sl, (g, u) in zip(subs, gate_up):
            act = ((g * jax.nn.sigmoid(g)) * u).astype(BF16)
            o_ref[sl, :] = _pack_rows(_dot(act, wd_s[...]).astype(BF16).astype(F32))

    @pl.when(jnp.logical_not(valid))
    def _():
        o_ref[...] = jnp.zeros_like(o_ref)


def _experts(layer, blk_expert, blk_rows, xs, w_gate, w_up, w_down):
    n_slots = xs.shape[0]
    n_blocks = n_slots // EXPERT_TILE
    grid_spec = pltpu.PrefetchScalarGridSpec(
        num_scalar_prefetch=2,
        grid=(n_blocks,),
        in_specs=[pl.BlockSpec((EXPERT_TILE, D_MODEL // 2), lambda i, be, bv: (i, 0)),
                  pl.BlockSpec((1, 1, D_MODEL, D_EXPERT), lambda i, be, bv: (layer, be[i], 0, 0)),
                  pl.BlockSpec((1, 1, D_MODEL, D_EXPERT), lambda i, be, bv: (layer, be[i], 0, 0)),
                  pl.BlockSpec((1, 1, D_EXPERT, D_MODEL), lambda i, be, bv: (layer, be[i], 0, 0))],
        out_specs=pl.BlockSpec((EXPERT_TILE, D_MODEL // 2), lambda i, be, bv: (i, 0)),
        scratch_shapes=[pltpu.VMEM((D_MODEL, D_EXPERT), BF16), pltpu.VMEM((D_MODEL, D_EXPERT), BF16),
                        pltpu.VMEM((D_EXPERT, D_MODEL), BF16)],
    )
    return pl.pallas_call(
        _experts_kernel,
        grid_spec=grid_spec,
        out_shape=jax.ShapeDtypeStruct((n_slots, D_MODEL // 2), jnp.uint32),
        compiler_params=_params(("arbitrary",)),
        name="experts",
    )(blk_expert, blk_rows, xs, w_gate, w_up, w_down)


def _combine_kernel(x_ref, y0_ref, y1_ref, rt_ref, g_ref, o_ref, *, final):
    g0 = rt_ref[:, TOP_K:TOP_K + 1]
    g1 = rt_ref[:, TOP_K + 1:TOP_K + 2]
    x = x_ref[...] + (_unpack_rows(y0_ref[...]) * g0 + _unpack_rows(y1_ref[...]) * g1)
    o_ref[...] = _rms(x, g_ref[...]) if final else x


def _combine(x1, y01, route, g, final):
    n = x1.shape[0]
    rows = min(ROW_TILE, n)
    steps = n // rows
    row = pl.BlockSpec((rows, D_MODEL), lambda i: (i, 0))
    return pl.pallas_call(
        functools.partial(_combine_kernel, final=final),
        grid=(steps,),
        in_specs=[row, pl.BlockSpec((rows, D_MODEL // 2), lambda i: (i, 0)),
                  pl.BlockSpec((rows, D_MODEL // 2), lambda i: (i + steps, 0)),
                  pl.BlockSpec((rows, LANES), lambda i: (i, 0)),
                  pl.BlockSpec((1, D_MODEL), lambda i: (0, 0))],
        out_specs=row,
        out_shape=jax.ShapeDtypeStruct((n, D_MODEL), F32),
        compiler_params=_params(("parallel",)),
        name="combine",
    )(x1, y01, y01, route, g)


def _prep_in_weights(w_in):
    src = np.full((IN_COLS,), -1, np.int32)
    src[0:1536] = np.arange(416, 1952)
    src[1536:1920] = np.arange(0, 384)
    src[1920:1920 + MLA_ROPE] = np.arange(384, 416)
    src[1920 + MLA_ROPE:1920 + MLA_ROPE + FOX_HEADS] = np.arange(1952, 1960)
    scale = np.where(np.arange(IN_COLS) < HEAD_W, FOX_DIM ** -0.5, 1.0).astype(np.float32)
    rows = lax.broadcasted_iota(jnp.int32, (w_in.shape[1], IN_COLS), 0)
    select = jnp.where(rows == jnp.asarray(src)[None, :], jnp.asarray(scale)[None, :], 0.0).astype(BF16)
    return jnp.dot(w_in.astype(BF16), select, preferred_element_type=F32).astype(BF16)


def _prep_uq(w_uq):
    w = w_uq.reshape(MLA_Q_RANK, MLA_HEADS, MLA_NOPE + MLA_ROPE)
    nope, pe = w[:, :, :MLA_NOPE], w[:, :, MLA_NOPE:]
    pairs = nope.reshape(MLA_Q_RANK, MLA_HEADS // 2, 2, MLA_NOPE)[:, :, ::-1, :].reshape(MLA_Q_RANK, -1)
    half = MLA_ROPE // 2
    pe_swapped = jnp.concatenate([pe[..., half:], pe[..., :half]], axis=-1)
    return jnp.concatenate([pairs, pe.reshape(MLA_Q_RANK, -1), pe_swapped.reshape(MLA_Q_RANK, -1)],
                           axis=1).astype(BF16)


def _prep_ukv(w_ukv):
    w = w_ukv.reshape(MLA_KV_RANK, MLA_HEADS // 2, 2, 2, MLA_NOPE)
    even = w[:, :, 0, ::-1, :]
    odd = w[:, :, 1, :, :]
    return jnp.stack([even, odd], axis=2).reshape(MLA_KV_RANK, -1).astype(BF16)


def _prep_router(w_rg, b_rg, w_re, b_re):
    d = w_rg.shape[0]
    w = jnp.concatenate([w_rg, w_re, jnp.zeros((d, LANES - N_GROUPS - N_EXPERTS), F32)], axis=1)
    w_hi = w.astype(BF16)
    w_lo = (w - w_hi.astype(F32)).astype(BF16)
    b = jnp.concatenate([b_rg, b_re, jnp.zeros((LANES - N_GROUPS - N_EXPERTS,), F32)])[None, :]
    return jnp.concatenate([w_hi, w_lo], axis=1), b


def _rope_slabs(positions):
    half = MLA_ROPE // 2
    inv_freq = ROPE_THETA ** (-jnp.arange(half, dtype=F32) / half)
    ang = positions.astype(F32).reshape(-1)[:, None] * inv_freq
    cos, sin = jnp.cos(ang), jnp.sin(ang)
    reps = LANES // MLA_ROPE
    return jnp.tile(jnp.concatenate([cos, cos], axis=1), (1, reps)), jnp.tile(
        jnp.concatenate([-sin, sin], axis=1), (1, reps))


def _slot_layout(route_t, counts, n_blocks):
    eid = route_t[0:TOP_K].astype(jnp.int32)
    rank = route_t[2 * TOP_K:3 * TOP_K].astype(jnp.int32)
    counts = counts[0, N_GROUPS:N_GROUPS + N_EXPERTS].astype(jnp.int32)
    padded = (counts + EXPERT_TILE - 1) // EXPERT_TILE * EXPERT_TILE
    pad_end = jnp.cumsum(padded)
    pad_start = pad_end - padded
    experts = jnp.arange(N_EXPERTS, dtype=jnp.int32)
    dest = jnp.sum(jnp.where(eid[None] == experts[:, None, None], pad_start[:, None, None], 0), axis=0) + rank
    blk_start = jnp.arange(n_blocks, dtype=jnp.int32) * EXPERT_TILE
    blk_expert = jnp.minimum(jnp.sum((blk_start[:, None] >= pad_end[None, :]).astype(jnp.int32), axis=1),
                             N_EXPERTS - 1)
    in_expert = jnp.sum(jnp.where(blk_expert[:, None] == experts, counts + pad_start, 0), axis=-1) - blk_start
    blk_rows = jnp.where(blk_start < pad_end[-1], jnp.clip(in_expert, 0, EXPERT_TILE), 0).astype(jnp.int32)
    return dest, blk_expert, blk_rows


def kernel(x, positions, attn_norm, w_in, b_forget, q_norm, w_uq, kv_norm, w_ukv, mla_out_norm, fox_out_norm,
           w_out, ffn_norm, w_router_group, b_router_group, w_router_expert, b_router_expert, w_gate, w_up,
           w_down, final_norm):
    batch, seq, d = x.shape
    n = batch * seq
    depth = w_in.shape[0]
    n_blocks = -(-(n * TOP_K) // EXPERT_TILE) + N_EXPERTS
    cc, ss = _rope_slabs(positions)
    tri = (jnp.arange(CUM_TILE)[:, None] <= jnp.arange(CUM_TILE)[None, :]).astype(BF16)
    sub = min(ROW_TILE, n) // SUB_TILES
    tri_rows = (jnp.arange(sub)[None, :] < jnp.arange(sub)[:, None]).astype(BF16)
    xf = x.reshape(n, d)
    for l in range(depth):
        fq, fk, fv, qn, qpe, kv, kpe, flt = _in_proj(
            xf, attn_norm[l][None, :], _prep_in_weights(w_in[l]), q_norm[l][None, :], _prep_uq(w_uq[l]),
            kv_norm[l][None, :], _prep_ukv(w_ukv[l]), cc, ss)
        ck = _fox_decay(flt, b_forget[l][:, None], tri, batch, seq)
        o_mla = _mla_attn(qn, qpe, kv, kpe, batch, seq).reshape(n, HEAD_W)
        o_fox = _fox_attn(fq, fk, fv, ck, batch, seq).reshape(n, HEAD_W)
        wr, br = _prep_router(w_router_group[l], b_router_group[l], w_router_expert[l], b_router_expert[l])
        x1, h2, route, route_t, counts = _out_proj(
            o_mla, o_fox, xf, mla_out_norm[l][None, :], fox_out_norm[l][None, :], w_out[l].astype(BF16),
            ffn_norm[l][None, :], wr, br, tri_rows)
        dest, blk_expert, blk_rows = _slot_layout(route_t, counts, n_blocks)
        xs = _scatter_rows(h2, [dest[k] for k in range(TOP_K)], n_blocks * EXPERT_TILE)
        ys = _experts(l, blk_expert, blk_rows, xs, w_gate, w_up, w_down)
        y01 = _gather_rows(ys, dest.reshape(-1))
        final = l == depth - 1
        xf = _combine(x1, y01, route, final_norm[None, :] if final else ffn_norm[l][None, :], final)
    return xf.reshape(batch, seq, d)
```

```python
import functools

import jax
import numpy as np
import jax.numpy as jnp
from jax import lax
from jax.experimental import pallas as pl
from jax.experimental.pallas import tpu as pltpu
from jax.experimental.pallas import tpu_sc as plsc

D_MODEL = 1024
CHUNK = 64
MLA_HEADS = 8
MLA_NOPE = 64
MLA_ROPE = 32
MLA_V = 64
MLA_Q_RANK = 256
MLA_KV_RANK = 128
ROPE_THETA = 10000.0
FOX_HEADS = 8
FOX_DIM = 64
HEAD_W = 512
N_GROUPS = 4
EXPERTS_PER_GROUP = 8
N_EXPERTS = 32
TOP_K = 2
D_EXPERT = 512
NORM_EPS = 1e-6

LANES = 128
IN_COLS = 2048
ROW_TILE = 512
SUB_TILES = 2
COMBINE_TILE = 1024
Q_TILE = 256
PAIRS_PER_STEP = 2
EXPERT_TILE = 512
CUM_TILE = 256
NEG_BIG = -1e30
LOG2E = 1.4426950408889634
VMEM_LIMIT = 48 * 1024 * 1024
SC_CORES = 2
SC_SUBCORES = 16
GATHER_WINDOW = 64
ROUTE_ROWS = 8

F32 = jnp.float32
BF16 = jnp.bfloat16


def _rms(x, g):
    return (x * lax.rsqrt(jnp.mean(x * x, axis=-1, keepdims=True) + NORM_EPS)) * g


def _dot(a, b):
    return jnp.dot(a, b, preferred_element_type=F32)


def _dot_nt(a, b):
    return lax.dot_general(a, b, (((1,), (1,)), ((), ())), preferred_element_type=F32)


def _params(sem):
    return pltpu.CompilerParams(dimension_semantics=sem, vmem_limit_bytes=VMEM_LIMIT)


def _sub_slices(rows):
    sub = rows // SUB_TILES
    return [slice(t * sub, (t + 1) * sub) for t in range(SUB_TILES)]


def _pack_rows(v):
    w = v.shape[1] // 2
    lo = lax.bitcast_convert_type(v[:, :w], jnp.uint32) >> 16
    hi = lax.bitcast_convert_type(v[:, w:], jnp.uint32) & jnp.uint32(0xFFFF0000)
    return hi | lo


def _unpack_rows(p):
    lo = lax.bitcast_convert_type(p << 16, F32)
    hi = lax.bitcast_convert_type(p & jnp.uint32(0xFFFF0000), F32)
    return jnp.concatenate([lo, hi], axis=1)


def _swap_rope_halves(x):
    half = MLA_ROPE // 2
    tiles = []
    for c in range(x.shape[1] // LANES):
        t = x[:, c * LANES:(c + 1) * LANES]
        lane = lax.broadcasted_iota(jnp.int32, t.shape, 1)
        tiles.append(jnp.where(lane % MLA_ROPE < half, pltpu.roll(t, LANES - half, 1), pltpu.roll(t, half, 1)))
    return tiles[0] if len(tiles) == 1 else jnp.concatenate(tiles, axis=1)


def _in_proj_kernel(x_ref, g_ref, w_ref, gq_ref, wuq_ref, gkv_ref, wukv_ref, cc_ref, ss_ref,
                    fq_ref, fk_ref, fv_ref, qn_ref, qpe_ref, kv_ref, kpe_ref, flt_ref):
    subs = _sub_slices(x_ref.shape[0])
    ps = [_dot(_rms(x_ref[sl, :], g_ref[...]).astype(BF16), w_ref[...]) for sl in subs]
    scale = (MLA_NOPE + MLA_ROPE) ** -0.5 * LOG2E
    for sl, p in zip(subs, ps):
        fq_ref[sl, :] = (p[:, 0:512] * LOG2E).astype(BF16)
        fk_ref[sl, :] = p[:, 512:1024].astype(BF16)
        fv_ref[sl, :] = p[:, 1024:1536].astype(BF16)

        cc = cc_ref[sl, :]
        ss = ss_ref[sl, :]
        q = _dot(_rms(p[:, 1536:1792], gq_ref[...]).astype(BF16), wuq_ref[...])
        qn_ref[sl, :] = (q[:, 0:512] * scale).astype(BF16)
        cc2 = jnp.concatenate([cc, cc], axis=1)
        ss2 = jnp.concatenate([ss, ss], axis=1)
        qpe_ref[sl, :] = ((q[:, 512:768] * cc2 + q[:, 768:1024] * ss2) * scale).astype(BF16)

        kv_ref[sl, :] = _dot(_rms(p[:, 1792:1920], gkv_ref[...]).astype(BF16), wukv_ref[...]).astype(BF16)

        slab_a = p[:, 1920:2048]
        lane = lax.broadcasted_iota(jnp.int32, slab_a.shape, 1)
        roped = jnp.where(lane < MLA_ROPE, slab_a * cc + _swap_rope_halves(slab_a) * ss, 0.0)
        tiled = roped + pltpu.roll(roped, 32, 1) + pltpu.roll(roped, 64, 1) + pltpu.roll(roped, 96, 1)
        kpe_ref[sl, :] = tiled.astype(BF16)
        flt_ref[:, sl] = slab_a.T[MLA_ROPE:MLA_ROPE + FOX_HEADS, :]


def _in_proj(x2d, g, w, gq, wuq, gkv, wukv, cc, ss):
    n = x2d.shape[0]
    rows = min(ROW_TILE, n)
    row = lambda c: pl.BlockSpec((rows, c), lambda i: (i, 0))
    full = lambda a: pl.BlockSpec(a.shape, lambda i: (0,) * a.ndim)
    out_shape = (
        jax.ShapeDtypeStruct((n, HEAD_W), BF16), jax.ShapeDtypeStruct((n, HEAD_W), BF16),
        jax.ShapeDtypeStruct((n, HEAD_W), BF16), jax.ShapeDtypeStruct((n, HEAD_W), BF16),
        jax.ShapeDtypeStruct((n, 256), BF16), jax.ShapeDtypeStruct((n, 1024), BF16),
        jax.ShapeDtypeStruct((n, LANES), BF16), jax.ShapeDtypeStruct((FOX_HEADS, n), F32),
    )
    return pl.pallas_call(
        _in_proj_kernel,
        grid=(n // rows,),
        in_specs=[row(D_MODEL), full(g), full(w), full(gq), full(wuq), full(gkv), full(wukv),
                  row(LANES), row(LANES)],
        out_specs=(row(HEAD_W), row(HEAD_W), row(HEAD_W), row(HEAD_W), row(256), row(1024), row(LANES),
                   pl.BlockSpec((FOX_HEADS, rows), lambda i: (0, i))),
        out_shape=out_shape,
        compiler_params=_params(("parallel",)),
        name="in_proj",
    )(x2d, g, w, gq, wuq, gkv, wukv, cc, ss)


def _fox_decay_kernel(fl_ref, b_ref, tri_ref, ck_ref):
    z = fl_ref[...] + b_ref[...]
    lf = jnp.minimum(z, 0.0) - jnp.log1p(jnp.exp(-jnp.abs(z)))
    seq = lf.shape[1]
    tri = tri_ref[...]
    carry = jnp.zeros((FOX_HEADS, 1), F32)
    zeros = jnp.zeros((FOX_HEADS, CUM_TILE), F32)
    for j in range(seq // CUM_TILE):
        v = lf[:, j * CUM_TILE:(j + 1) * CUM_TILE]
        hi = v.astype(BF16).astype(F32)
        r1 = v - hi
        mid = r1.astype(BF16).astype(F32)
        lo = r1 - mid
        parts = _dot(jnp.concatenate([hi, mid, lo, zeros], axis=0).astype(BF16), tri)
        cs = (parts[0:8] + parts[8:16]) + parts[16:24] + carry
        carry = cs[:, CUM_TILE - 1:CUM_TILE]
        d = cs * (-LOG2E)
        d_hi = d.astype(BF16).astype(F32)
        d_r = d - d_hi
        d_mid = d_r.astype(BF16).astype(F32)
        d_lo = d_r - d_mid
        rows = jnp.concatenate([d_hi, d_mid, d_lo, jnp.zeros((LANES - 3 * FOX_HEADS, CUM_TILE), F32)], axis=0)
        ck_ref[0, j * CUM_TILE:(j + 1) * CUM_TILE, :] = rows.T.astype(BF16)


def _fox_decay(flt, b_col, tri, batch, seq):
    return pl.pallas_call(
        _fox_decay_kernel,
        grid=(batch,),
        in_specs=[pl.BlockSpec((FOX_HEADS, seq), lambda b: (0, b)),
                  pl.BlockSpec((FOX_HEADS, 1), lambda b: (0, 0)),
                  pl.BlockSpec((CUM_TILE, CUM_TILE), lambda b: (0, 0))],
        out_specs=pl.BlockSpec((1, seq, LANES), lambda b: (b, 0, 0)),
        out_shape=jax.ShapeDtypeStruct((batch, seq, LANES), BF16),
        compiler_params=_params(("parallel",)),
        name="fox_decay",
    )(flt, b_col, tri)


def _probs(s_off, s_diag):
    m = jnp.max(s_diag, axis=-1, keepdims=True)
    if s_off is not None:
        m = jnp.maximum(m, jnp.max(s_off, axis=-1, keepdims=True))
    p_off = None if s_off is None else jnp.exp2(s_off - m).astype(BF16)
    return p_off, jnp.exp2(s_diag - m).astype(BF16)


def _attend(q_ref, k_ref, v_ref, o_ref, out_lane0, allowed):
    seq = q_ref.shape[1]
    out_lane = lax.broadcasted_iota(jnp.int32, (Q_TILE, LANES), 1)
    units = [(i, hh) for i in range(seq // Q_TILE) for hh in range(2)]

    def scores(i, hh):
        qs, qe = i * Q_TILE, (i + 1) * Q_TILE
        q = q_ref[hh, qs:qe, :]
        kh = hh % k_ref.shape[0]
        s_diag = jnp.where(allowed, _dot_nt(q, k_ref[kh, qs:qe, :]), NEG_BIG)
        s_off = _dot_nt(q, k_ref[kh, 0:qs, :]) if i > 0 else None
        return s_off, s_diag

    def values(i, hh, p_off, p_diag):
        qs, qe = i * Q_TILE, (i + 1) * Q_TILE
        vh = hh % v_ref.shape[0]
        acc = _dot(p_diag, v_ref[vh, qs:qe, :])
        if p_off is not None:
            acc = acc + _dot(p_off, v_ref[vh, 0:qs, :])
        return acc[:, 0:LANES] / acc[:, LANES:2 * LANES]

    n_units = len(units)
    s = {0: scores(*units[0])}
    if n_units > 1:
        s[1] = scores(*units[1])
    p = {0: _probs(*s.pop(0))}
    outs = []
    for n, (i, hh) in enumerate(units):
        if n + 2 < n_units:
            s[n + 2] = scores(*units[n + 2])
        if n + 1 < n_units:
            p[n + 1] = _probs(*s.pop(n + 1))
        outs.append(values(i, hh, *p.pop(n)))
        if hh == 1:
            o_ref[0, i * Q_TILE:(i + 1) * Q_TILE, out_lane0:out_lane0 + LANES] = jnp.where(
                out_lane < 64, outs[0], outs[1]).astype(BF16)
            outs = []


def _tile_iota():
    row = lax.broadcasted_iota(jnp.int32, (Q_TILE, Q_TILE), 0)
    col = lax.broadcasted_iota(jnp.int32, (Q_TILE, Q_TILE), 1)
    return row, col


def _mla_attn_kernel(qn_ref, qpe_ref, kv_ref, kpe_ref, o_ref, qs_all, ks_all, vs_all):
    seq = qn_ref.shape[1]
    lane = lax.broadcasted_iota(jnp.int32, (seq, LANES), 1)
    qpe = qpe_ref[0]
    kpe = kpe_ref[0]
    ones = jnp.ones((seq, LANES), BF16)
    row, col = _tile_iota()
    for pp in range(PAIRS_PER_STEP):
        qs_ref, ks_ref, vs_ref = qs_all.at[pp], ks_all.at[pp], vs_all.at[pp]
        qn = qn_ref[0, :, pp * LANES:(pp + 1) * LANES]
        for hh in range(2):
            nope_mask = (lane >= 64) if hh == 0 else (lane < 64)
            pe_mask = (lane // MLA_ROPE) == 2 * pp + hh
            qs_ref[hh, :, 0:LANES] = jnp.where(nope_mask, qn, jnp.zeros_like(qn))
            qs_ref[hh, :, LANES:2 * LANES] = jnp.where(pe_mask, qpe, jnp.zeros_like(qpe))
            kvh = kv_ref[0, :, (2 * pp + hh) * LANES:(2 * pp + hh + 1) * LANES]
            ks_ref[hh, :, 0:LANES] = kvh
            ks_ref[hh, :, LANES:2 * LANES] = kpe
            vs_ref[hh, :, 0:LANES] = kvh
            vs_ref[hh, :, LANES:2 * LANES] = ones
        _attend(qs_ref, ks_ref, vs_ref, o_ref, pp * LANES, (col // CHUNK) <= (row // CHUNK))


def _mla_attn(qn, qpe, kv, kpe, batch, seq):
    qn, qpe, kv, kpe = (a.reshape(batch, seq, a.shape[-1]) for a in (qn, qpe, kv, kpe))
    pair = pltpu.VMEM((PAIRS_PER_STEP, 2, seq, 2 * LANES), BF16)
    width = PAIRS_PER_STEP * LANES
    return pl.pallas_call(
        _mla_attn_kernel,
        grid=(batch, MLA_HEADS // (2 * PAIRS_PER_STEP)),
        in_specs=[pl.BlockSpec((1, seq, width), lambda b, j: (b, 0, j)),
                  pl.BlockSpec((1, seq, LANES), lambda b, j: (b, 0, j)),
                  pl.BlockSpec((1, seq, 2 * width), lambda b, j: (b, 0, j)),
                  pl.BlockSpec((1, seq, LANES), lambda b, j: (b, 0, 0))],
        out_specs=pl.BlockSpec((1, seq, width), lambda b, j: (b, 0, j)),
        out_shape=jax.ShapeDtypeStruct((batch, seq, HEAD_W), BF16),
        scratch_shapes=[pair, pair, pair],
        compiler_params=_params(("parallel", "parallel")),
        name="mla_attn",
    )(qn, qpe, kv, kpe)


def _fox_attn_kernel(q_ref, k_ref, v_ref, ck_ref, o_ref, qs_all, ks_all, vs_all):
    j = pl.program_id(1)
    seq = q_ref.shape[1]
    lane = lax.broadcasted_iota(jnp.int32, (seq, LANES), 1)
    row, col = _tile_iota()
    for pp in range(PAIRS_PER_STEP):
        qs_ref, ks_ref, vs_ref = qs_all.at[pp], ks_all.at[pp], vs_all.at[pp]
        chunk = slice(pp * LANES, (pp + 1) * LANES)
        q = q_ref[0, :, chunk]
        ks_ref[0, :, 0:LANES] = k_ref[0, :, chunk]
        ks_ref[0, :, LANES:2 * LANES] = ck_ref[0]
        vs_ref[0, :, 0:LANES] = v_ref[0, :, chunk]
        vs_ref[0, :, LANES:2 * LANES] = jnp.ones((seq, LANES), BF16)
        for hh in range(2):
            head = 2 * (PAIRS_PER_STEP * j + pp) + hh
            head_mask = (lane < 64) if hh == 0 else (lane >= 64)
            piece = jnp.where(lane < 3 * FOX_HEADS, lane % FOX_HEADS, -1) == head
            qs_ref[hh, :, 0:LANES] = jnp.where(head_mask, q, jnp.zeros_like(q))
            qs_ref[hh, :, LANES:2 * LANES] = jnp.where(piece, 1.0, 0.0).astype(BF16)
        _attend(qs_ref, ks_ref, vs_ref, o_ref, pp * LANES, col <= row)


def _fox_attn(fq, fk, fv, ck, batch, seq):
    fq, fk, fv = (a.reshape(batch, seq, HEAD_W) for a in (fq, fk, fv))
    spec = pl.BlockSpec((1, seq, PAIRS_PER_STEP * LANES), lambda b, j: (b, 0, j))
    pair = pltpu.VMEM((PAIRS_PER_STEP, 2, seq, 2 * LANES), BF16)
    shared = pltpu.VMEM((PAIRS_PER_STEP, 1, seq, 2 * LANES), BF16)
    return pl.pallas_call(
        _fox_attn_kernel,
        grid=(batch, FOX_HEADS // (2 * PAIRS_PER_STEP)),
        in_specs=[spec, spec, spec, pl.BlockSpec((1, seq, LANES), lambda b, j: (b, 0, 0))],
        out_specs=spec,
        out_shape=jax.ShapeDtypeStruct((batch, seq, HEAD_W), BF16),
        scratch_shapes=[pair, shared, shared],
        compiler_params=_params(("parallel", "parallel")),
        name="fox_attn",
    )(fq, fk, fv, ck)


def _lane_max(v):
    return jnp.max(v, axis=1, keepdims=True)


def _first_lane(hit, lane_f):
    return jnp.min(jnp.where(hit, lane_f, float(LANES)), axis=1, keepdims=True)


def _out_proj_kernel(om_ref, of_ref, x_ref, gm_ref, gf_ref, wo_ref, gn_ref, wr_ref, br_ref, tri_ref,
                     x1_ref, h2_ref, rt_ref, rtt_ref, cnt_ref, carry_ref):
    @pl.when(pl.program_id(0) == 0)
    def _():
        carry_ref[...] = jnp.zeros_like(carry_ref)

    subs = _sub_slices(x_ref.shape[0])
    mixed = [_mix_heads(sl, om_ref, of_ref, gm_ref, gf_ref, wo_ref) for sl in subs]
    logits = [_residual_and_logits(sl, m, x_ref, gn_ref, wr_ref, br_ref, x1_ref, h2_ref) for sl, m in zip(subs, mixed)]
    carry = carry_ref[...]
    for sl, lg in zip(subs, logits):
        rt, carry = _route_rows(lg, tri_ref[...], carry)
        rt_ref[sl, :] = rt
        rtt_ref[:, sl] = rt.T[0:ROUTE_ROWS, :]
    carry_ref[...] = carry
    cnt_ref[...] = carry


def _mix_heads(sl, om_ref, of_ref, gm_ref, gf_ref, wo_ref):
    a = _rms(om_ref[sl, :].astype(F32), gm_ref[...]).astype(BF16)
    b = _rms(of_ref[sl, :].astype(F32), gf_ref[...]).astype(BF16)
    return _dot(a, wo_ref[0:HEAD_W, :]) + _dot(b, wo_ref[HEAD_W:2 * HEAD_W, :])


def _residual_and_logits(sl, mixed, x_ref, gn_ref, wr_ref, br_ref, x1_ref, h2_ref):
    x1 = x_ref[sl, :] + mixed
    x1_ref[sl, :] = x1
    h2 = _rms(x1, gn_ref[...])
    h_hi = h2.astype(BF16)
    h2_ref[sl, :] = _pack_rows(h_hi.astype(F32))
    h_lo = (h2 - h_hi.astype(F32)).astype(BF16)
    t = _dot(h_hi, wr_ref[...])
    return (t[:, 0:LANES] + t[:, LANES:2 * LANES]) + _dot(h_lo, wr_ref[:, 0:LANES]) + br_ref[...]


def _route_rows(lg, tri, carry):
    lane = lax.broadcasted_iota(jnp.int32, lg.shape, 1)
    lane_f = lane.astype(F32)
    neg_inf = float("-inf")
    is_group = lane < N_GROUPS
    gl = jnp.where(is_group, lg, neg_inf)
    mg = _lane_max(gl)
    gi = _first_lane(gl == mg, lane_f)
    g_val = 1.0 / jnp.sum(jnp.where(is_group, jnp.exp(lg - mg), 0.0), axis=1, keepdims=True)
    group_of_lane = ((lane - N_GROUPS) >> 3).astype(F32)
    is_expert = (lane >= N_GROUPS) & (lane < N_GROUPS + N_EXPERTS) & (group_of_lane == gi)
    el = jnp.where(is_expert, lg, neg_inf)
    m1 = _lane_max(el)
    i1 = _first_lane(el == m1, lane_f)
    el2 = jnp.where(lane_f == i1, neg_inf, el)
    m2 = _lane_max(el2)
    i2 = _first_lane(el2 == m2, lane_f)
    r = jnp.exp(m2 - m1)
    g0 = g_val / (1.0 + r)
    g1 = g0 * r

    hit1 = lane_f == i1
    hit2 = lane_f == i2
    onehot = jnp.where(hit1 | hit2, 1.0, 0.0)
    before = _dot(tri, onehot.astype(BF16)) + carry
    r0 = jnp.sum(jnp.where(hit1, before, 0.0), axis=1, keepdims=True)
    r1 = jnp.sum(jnp.where(hit2, before, 0.0), axis=1, keepdims=True)

    vals = (i1 - N_GROUPS, i2 - N_GROUPS, g0, g1, r0, r1)
    rt = jnp.zeros_like(lg)
    for k, v in enumerate(vals):
        rt = jnp.where(lane == k, v, rt)
    return rt, carry + jnp.sum(onehot, axis=0, keepdims=True)


def _out_proj(om, of, x2d, gm, gf, wo, gn, wr, br, tri):
    n = x2d.shape[0]
    rows = tri.shape[0] * SUB_TILES
    row = lambda c: pl.BlockSpec((rows, c), lambda i: (i, 0))
    full = lambda a: pl.BlockSpec(a.shape, lambda i: (0,) * a.ndim)
    return pl.pallas_call(
        _out_proj_kernel,
        grid=(n // rows,),
        in_specs=[row(HEAD_W), row(HEAD_W), row(D_MODEL), full(gm), full(gf), full(wo), full(gn), full(wr),
                  full(br), full(tri)],
        out_specs=(row(D_MODEL), row(D_MODEL // 2), row(LANES), pl.BlockSpec((ROUTE_ROWS, rows), lambda i: (0, i)),
                   pl.BlockSpec((1, LANES), lambda i: (0, 0))),
        out_shape=(jax.ShapeDtypeStruct((n, D_MODEL), F32), jax.ShapeDtypeStruct((n, D_MODEL // 2), jnp.uint32),
                   jax.ShapeDtypeStruct((n, LANES), F32), jax.ShapeDtypeStruct((ROUTE_ROWS, n), F32),
                   jax.ShapeDtypeStruct((1, LANES), F32)),
        scratch_shapes=[pltpu.VMEM((1, LANES), F32)],
        compiler_params=_params(("arbitrary",)),
        name="out_proj",
    )(om, of, x2d, gm, gf, wo, gn, wr, br, tri)


def _gather_rows(table, idx):
    m, w = idx.shape[0], table.shape[1]
    workers = SC_CORES * SC_SUBCORES
    per_worker = m // workers
    n_win = per_worker // GATHER_WINDOW
    assert per_worker * workers == m and n_win * GATHER_WINDOW == per_worker and n_win % 2 == 0
    mesh = plsc.VectorSubcoreMesh(core_axis_name="core", subcore_axis_name="subcore")

    @functools.partial(
        pl.kernel, mesh=mesh, out_type=jax.ShapeDtypeStruct((m, w), table.dtype),
        scratch_types=[pltpu.VMEM((per_worker,), jnp.int32), pltpu.VMEM((2, GATHER_WINDOW, w), table.dtype),
                       pltpu.SemaphoreType.DMA((2,)), pltpu.SemaphoreType.DMA((2,))],
        name="gather_rows")
    def gather(table_hbm, idx_hbm, out_hbm, idx_v, rows_v, gather_sems, write_sems):
        base = (lax.axis_index("subcore") * SC_CORES + lax.axis_index("core")) * per_worker
        pltpu.sync_copy(idx_hbm.at[pl.ds(base, per_worker)], idx_v)

        def gather_copy(win, buf):
            rows = idx_v.at[pl.ds(win * GATHER_WINDOW, GATHER_WINDOW)]
            return pltpu.make_async_copy(table_hbm.at[rows], rows_v.at[buf], gather_sems.at[buf])

        def write_copy(win, buf):
            dst = out_hbm.at[pl.ds(base + win * GATHER_WINDOW, GATHER_WINDOW)]
            return pltpu.make_async_copy(rows_v.at[buf], dst, write_sems.at[buf])

        gather_copy(0, 0).start()

        @pl.loop(0, n_win, step=2)
        def _(j):
            for buf in range(2):
                win = j + buf
                gather_copy(win, buf).wait()
                write_copy(win, buf).start()

                @pl.when(win + 1 < n_win)
                def _():
                    @pl.when(win >= 1)
                    def _():
                        write_copy(win - 1, 1 - buf).wait()

                    gather_copy(win + 1, 1 - buf).start()

        write_copy(n_win - 2, 0).wait()
        write_copy(n_win - 1, 1).wait()

    return gather(table, idx)


def _scatter_rows(rows, dests, n_out):
    n, w = rows.shape
    workers = SC_CORES * SC_SUBCORES
    per_worker = n // workers
    n_win = per_worker // GATHER_WINDOW
    n_lists = len(dests)
    assert per_worker * workers == n and n_win * GATHER_WINDOW == per_worker and n_win % 2 == 0
    mesh = plsc.VectorSubcoreMesh(core_axis_name="core", subcore_axis_name="subcore")

    @functools.partial(
        pl.kernel, mesh=mesh, out_type=jax.ShapeDtypeStruct((n_out, w), rows.dtype),
        scratch_types=[pltpu.VMEM((per_worker,), jnp.int32)] * n_lists + [
            pltpu.VMEM((2, GATHER_WINDOW, w), rows.dtype), pltpu.SemaphoreType.DMA((2,)),
            pltpu.SemaphoreType.DMA((2,))],
        name="scatter_rows")
    def scatter(rows_hbm, *refs):
        dest_hbm, out_hbm = refs[:n_lists], refs[n_lists]
        dest_v = refs[n_lists + 1:2 * n_lists + 1]
        rows_v, read_sems, write_sems = refs[2 * n_lists + 1:]
        base = (lax.axis_index("subcore") * SC_CORES + lax.axis_index("core")) * per_worker
        for d_hbm, d_v in zip(dest_hbm, dest_v):
            pltpu.sync_copy(d_hbm.at[pl.ds(base, per_worker)], d_v)

        def read_copy(win, buf):
            src = rows_hbm.at[pl.ds(base + win * GATHER_WINDOW, GATHER_WINDOW)]
            return pltpu.make_async_copy(src, rows_v.at[buf], read_sems.at[buf])

        def write_copies(win, buf):
            return [pltpu.make_async_copy(rows_v.at[buf],
                                          out_hbm.at[d_v.at[pl.ds(win * GATHER_WINDOW, GATHER_WINDOW)]],
                                          write_sems.at[buf]) for d_v in dest_v]

        read_copy(0, 0).start()

        @pl.loop(0, n_win, step=2)
        def _(j):
            for buf in range(2):
                win = j + buf
                read_copy(win, buf).wait()
                for c in write_copies(win, buf):
                    c.start()

                @pl.when(win + 1 < n_win)
                def _():
                    @pl.when(win >= 1)
                    def _():
                        for c in write_copies(win - 1, 1 - buf):
                            c.wait()

                    read_copy(win + 1, 1 - buf).start()

        for c in write_copies(n_win - 2, 0) + write_copies(n_win - 1, 1):
            c.wait()

    return scatter(rows, *dests)


def _experts_kernel(be_ref, bv_ref, xs_ref, wg_ref, wu_ref, wd_ref, o_ref, wg_s, wu_s, wd_s):
    i = pl.program_id(0)
    valid = bv_ref[i] != 0

    @pl.when(jnp.logical_and(valid, jnp.logical_or(i == 0, be_ref[i] != be_ref[jnp.maximum(i - 1, 0)])))
    def _():
        wg_s[...] = wg_ref[0, 0].astype(BF16)
        wu_s[...] = wu_ref[0, 0].astype(BF16)
        wd_s[...] = wd_ref[0, 0].astype(BF16)

    @pl.when(valid)
    def _():
        row = lax.broadcasted_iota(jnp.int32, xs_ref.shape, 0)
        packed = jnp.where(row < bv_ref[i], xs_ref[...], jnp.uint32(0))
        subs = _sub_slices(EXPERT_TILE)
        gate_up = []
        for sl in subs:
            x = _unpack_rows(packed[sl, :]).astype(BF16)
            gate_up.append((_dot(x, wg_s[...]), _dot(x, wu_s[...])))
        for sl, (g, u) in zip(subs, gate_up):
            act = ((g * jax.nn.sigmoid(g)) * u).astype(BF16)
            o_ref[sl, :] = _pack_rows(_dot(act, wd_s[...]).astype(BF16).astype(F32))

    @pl.when(jnp.logical_not(valid))
    def _():
        o_ref[...] = jnp.zeros_like(o_ref)


def _experts(layer, blk_expert, blk_rows, xs, w_gate, w_up, w_down):
    n_slots = xs.shape[0]
    n_blocks = n_slots // EXPERT_TILE
    grid_spec = pltpu.PrefetchScalarGridSpec(
        num_scalar_prefetch=2,
        grid=(n_blocks,),
        in_specs=[pl.BlockSpec((EXPERT_TILE, D_MODEL // 2), lambda i, be, bv: (i, 0)),
                  pl.BlockSpec((1, 1, D_MODEL, D_EXPERT), lambda i, be, bv: (layer, be[i], 0, 0)),
                  pl.BlockSpec((1, 1, D_MODEL, D_EXPERT), lambda i, be, bv: (layer, be[i], 0, 0)),
                  pl.BlockSpec((1, 1, D_EXPERT, D_MODEL), lambda i, be, bv: (layer, be[i], 0, 0))],
        out_specs=pl.BlockSpec((EXPERT_TILE, D_MODEL // 2), lambda i, be, bv: (i, 0)),
        scratch_shapes=[pltpu.VMEM((D_MODEL, D_EXPERT), BF16), pltpu.VMEM((D_MODEL, D_EXPERT), BF16),
                        pltpu.VMEM((D_EXPERT, D_MODEL), BF16)],
    )
    return pl.pallas_call(
        _experts_kernel,
        grid_spec=grid_spec,
        out_shape=jax.ShapeDtypeStruct((n_slots, D_MODEL // 2), jnp.uint32),
        compiler_params=_params(("arbitrary",)),
        name="experts",
    )(blk_expert, blk_rows, xs, w_gate, w_up, w_down)


def _combine_kernel(x_ref, y0_ref, y1_ref, rt_ref, g_ref, o_ref, *, final):
    g0 = rt_ref[:, TOP_K:TOP_K + 1]
    g1 = rt_ref[:, TOP_K + 1:TOP_K + 2]
    x = x_ref[...] + (_unpack_rows(y0_ref[...]) * g0 + _unpack_rows(y1_ref[...]) * g1)
    o_ref[...] = _rms(x, g_ref[...]) if final else x


def _combine(x1, y01, route, g, final):
    n = x1.shape[0]
    rows = min(COMBINE_TILE, n)
    steps = n // rows
    row = pl.BlockSpec((rows, D_MODEL), lambda i: (i, 0))
    return pl.pallas_call(
        functools.partial(_combine_kernel, final=final),
        grid=(steps,),
        in_specs=[row, pl.BlockSpec((rows, D_MODEL // 2), lambda i: (i, 0)),
                  pl.BlockSpec((rows, D_MODEL // 2), lambda i: (i + steps, 0)),
                  pl.BlockSpec((rows, LANES), lambda i: (i, 0)),
                  pl.BlockSpec((1, D_MODEL), lambda i: (0, 0))],
        out_specs=row,
        out_shape=jax.ShapeDtypeStruct((n, D_MODEL), F32),
        compiler_params=_params(("parallel",)),
        name="combine",
    )(x1, y01, y01, route, g)


def _prep_in_weights(w_in):
    src = np.full((IN_COLS,), -1, np.int32)
    src[0:1536] = np.arange(416, 1952)
    src[1536:1920] = np.arange(0, 384)
    src[1920:1920 + MLA_ROPE] = np.arange(384, 416)
    src[1920 + MLA_ROPE:1920 + MLA_ROPE + FOX_HEADS] = np.arange(1952, 1960)
    scale = np.where(np.arange(IN_COLS) < HEAD_W, FOX_DIM ** -0.5, 1.0).astype(np.float32)
    rows = lax.broadcasted_iota(jnp.int32, (w_in.shape[1], IN_COLS), 0)
    select = jnp.where(rows == jnp.asarray(src)[None, :], jnp.asarray(scale)[None, :], 0.0).astype(BF16)
    return jnp.dot(w_in.astype(BF16), select, preferred_element_type=F32).astype(BF16)


def _prep_uq(w_uq):
    w = w_uq.reshape(MLA_Q_RANK, MLA_HEADS, MLA_NOPE + MLA_ROPE)
    nope, pe = w[:, :, :MLA_NOPE], w[:, :, MLA_NOPE:]
    pairs = nope.reshape(MLA_Q_RANK, MLA_HEADS // 2, 2, MLA_NOPE)[:, :, ::-1, :].reshape(MLA_Q_RANK, -1)
    half = MLA_ROPE // 2
    pe_swapped = jnp.concatenate([pe[..., half:], pe[..., :half]], axis=-1)
    return jnp.concatenate([pairs, pe.reshape(MLA_Q_RANK, -1), pe_swapped.reshape(MLA_Q_RANK, -1)],
                           axis=1).astype(BF16)


def _prep_ukv(w_ukv):
    w = w_ukv.reshape(MLA_KV_RANK, MLA_HEADS // 2, 2, 2, MLA_NOPE)
    even = w[:, :, 0, ::-1, :]
    odd = w[:, :, 1, :, :]
    return jnp.stack([even, odd], axis=2).reshape(MLA_KV_RANK, -1).astype(BF16)


def _prep_router(w_rg, b_rg, w_re, b_re):
    d = w_rg.shape[0]
    w = jnp.concatenate([w_rg, w_re, jnp.zeros((d, LANES - N_GROUPS - N_EXPERTS), F32)], axis=1)
    w_hi = w.astype(BF16)
    w_lo = (w - w_hi.astype(F32)).astype(BF16)
    b = jnp.concatenate([b_rg, b_re, jnp.zeros((LANES - N_GROUPS - N_EXPERTS,), F32)])[None, :]
    return jnp.concatenate([w_hi, w_lo], axis=1), b


def _rope_slabs(positions):
    half = MLA_ROPE // 2
    inv_freq = ROPE_THETA ** (-jnp.arange(half, dtype=F32) / half)
    ang = positions.astype(F32).reshape(-1)[:, None] * jnp.tile(inv_freq, LANES // half)
    sign = np.where(np.arange(LANES) % MLA_ROPE < half, -1.0, 1.0).astype(np.float32)
    return jnp.cos(ang), jnp.sin(ang) * sign


def _slot_layout(route_t, counts, n_blocks):
    eid = route_t[0:TOP_K].astype(jnp.int32)
    rank = route_t[2 * TOP_K:3 * TOP_K].astype(jnp.int32)
    counts = counts[0, N_GROUPS:N_GROUPS + N_EXPERTS].astype(jnp.int32)
    padded = (counts + EXPERT_TILE - 1) // EXPERT_TILE * EXPERT_TILE
    pad_end = jnp.cumsum(padded)
    pad_start = pad_end - padded
    experts = jnp.arange(N_EXPERTS, dtype=jnp.int32)
    dest = jnp.sum(jnp.where(eid[None] == experts[:, None, None], pad_start[:, None, None], 0), axis=0) + rank
    blk_start = jnp.arange(n_blocks, dtype=jnp.int32) * EXPERT_TILE
    blk_expert = jnp.minimum(jnp.sum((blk_start[:, None] >= pad_end[None, :]).astype(jnp.int32), axis=1),
                             N_EXPERTS - 1)
    in_expert = jnp.sum(jnp.where(blk_expert[:, None] == experts, counts + pad_start, 0), axis=-1) - blk_start
    blk_rows = jnp.where(blk_start < pad_end[-1], jnp.clip(in_expert, 0, EXPERT_TILE), 0).astype(jnp.int32)
    return dest, blk_expert, blk_rows


def kernel(x, positions, attn_norm, w_in, b_forget, q_norm, w_uq, kv_norm, w_ukv, mla_out_norm, fox_out_norm,
           w_out, ffn_norm, w_router_group, b_router_group, w_router_expert, b_router_expert, w_gate, w_up,
           w_down, final_norm):
    batch, seq, d = x.shape
    n = batch * seq
    depth = w_in.shape[0]
    n_blocks = -(-(n * TOP_K) // EXPERT_TILE) + N_EXPERTS
    cc, ss = _rope_slabs(positions)
    tri = (jnp.arange(CUM_TILE)[:, None] <= jnp.arange(CUM_TILE)[None, :]).astype(BF16)
    sub = min(ROW_TILE, n) // SUB_TILES
    tri_rows = (jnp.arange(sub)[None, :] < jnp.arange(sub)[:, None]).astype(BF16)
    xf = x.reshape(n, d)
    for l in range(depth):
        fq, fk, fv, qn, qpe, kv, kpe, flt = _in_proj(
            xf, attn_norm[l][None, :], _prep_in_weights(w_in[l]), q_norm[l][None, :], _prep_uq(w_uq[l]),
            kv_norm[l][None, :], _prep_ukv(w_ukv[l]), cc, ss)
        ck = _fox_decay(flt, b_forget[l][:, None], tri, batch, seq)
        o_mla = _mla_attn(qn, qpe, kv, kpe, batch, seq).reshape(n, HEAD_W)
        o_fox = _fox_attn(fq, fk, fv, ck, batch, seq).reshape(n, HEAD_W)
        wr, br = _prep_router(w_router_group[l], b_router_group[l], w_router_expert[l], b_router_expert[l])
        x1, h2, route, route_t, counts = _out_proj(
            o_mla, o_fox, xf, mla_out_norm[l][None, :], fox_out_norm[l][None, :], w_out[l].astype(BF16),
            ffn_norm[l][None, :], wr, br, tri_rows)
        dest, blk_expert, blk_rows = _slot_layout(route_t, counts, n_blocks)
        xs = _scatter_rows(h2, [dest[k] for k in range(TOP_K)], n_blocks * EXPERT_TILE)
        ys = _experts(l, blk_expert, blk_rows, xs, w_gate, w_up, w_down)
        y01 = _gather_rows(ys, dest.reshape(-1))
        final = l == depth - 1
        xf = _combine(x1, y01, route, final_norm[None, :] if final else ffn_norm[l][None, :], final)
    return xf.reshape(batch, seq, d)
```

```python
import functools

import jax
import numpy as np
import jax.numpy as jnp
from jax import lax
from jax.experimental import pallas as pl
from jax.experimental.pallas import tpu as pltpu
from jax.experimental.pallas import tpu_sc as plsc

D_MODEL = 1024
CHUNK = 64
MLA_HEADS = 8
MLA_NOPE = 64
MLA_ROPE = 32
MLA_V = 64
MLA_Q_RANK = 256
MLA_KV_RANK = 128
ROPE_THETA = 10000.0
FOX_HEADS = 8
FOX_DIM = 64
HEAD_W = 512
N_GROUPS = 4
EXPERTS_PER_GROUP = 8
N_EXPERTS = 32
TOP_K = 2
D_EXPERT = 512
NORM_EPS = 1e-6

LANES = 128
IN_COLS = 2048
ROW_TILE = 512
SUB_TILES = 2
COMBINE_TILE = 1024
Q_TILE = 256
PAIRS_PER_STEP = 2
EXPERT_TILE = 512
CUM_TILE = 256
NEG_BIG = -1e30
LOG2E = 1.4426950408889634
VMEM_LIMIT = 48 * 1024 * 1024
SC_CORES = 2
SC_SUBCORES = 16
GATHER_WINDOW = 64
ROUTE_ROWS = 8

F32 = jnp.float32
BF16 = jnp.bfloat16


def _rms(x, g):
    return (x * lax.rsqrt(jnp.mean(x * x, axis=-1, keepdims=True) + NORM_EPS)) * g


def _dot(a, b):
    return jnp.dot(a, b, preferred_element_type=F32)


def _dot_nt(a, b):
    return lax.dot_general(a, b, (((1,), (1,)), ((), ())), preferred_element_type=F32)


def _params(sem):
    return pltpu.CompilerParams(dimension_semantics=sem, vmem_limit_bytes=VMEM_LIMIT)


def _sub_slices(rows):
    sub = rows // SUB_TILES
    return [slice(t * sub, (t + 1) * sub) for t in range(SUB_TILES)]


def _pack_rows(v):
    w = v.shape[1] // 2
    lo = lax.bitcast_convert_type(v[:, :w], jnp.uint32) >> 16
    hi = lax.bitcast_convert_type(v[:, w:], jnp.uint32) & jnp.uint32(0xFFFF0000)
    return hi | lo


def _unpack_rows(p):
    lo = lax.bitcast_convert_type(p << 16, F32)
    hi = lax.bitcast_convert_type(p & jnp.uint32(0xFFFF0000), F32)
    return jnp.concatenate([lo, hi], axis=1)


def _swap_rope_halves(x):
    half = MLA_ROPE // 2
    tiles = []
    for c in range(x.shape[1] // LANES):
        t = x[:, c * LANES:(c + 1) * LANES]
        lane = lax.broadcasted_iota(jnp.int32, t.shape, 1)
        tiles.append(jnp.where(lane % MLA_ROPE < half, pltpu.roll(t, LANES - half, 1), pltpu.roll(t, half, 1)))
    return tiles[0] if len(tiles) == 1 else jnp.concatenate(tiles, axis=1)


def _in_proj_kernel(x_ref, g_ref, w_ref, gq_ref, wuq_ref, gkv_ref, wukv_ref, cc_ref, ss_ref,
                    fq_ref, fk_ref, fv_ref, qn_ref, qpe_ref, kv_ref, kpe_ref, flt_ref):
    subs = _sub_slices(x_ref.shape[0])
    ps = [_dot(_rms(x_ref[sl, :], g_ref[...]).astype(BF16), w_ref[...]) for sl in subs]
    scale = (MLA_NOPE + MLA_ROPE) ** -0.5 * LOG2E
    for sl, p in zip(subs, ps):
        fq_ref[sl, :] = (p[:, 0:512] * LOG2E).astype(BF16)
        fk_ref[sl, :] = p[:, 512:1024].astype(BF16)
        fv_ref[sl, :] = p[:, 1024:1536].astype(BF16)

        cc = cc_ref[sl, :]
        ss = ss_ref[sl, :]
        q = _dot(_rms(p[:, 1536:1792], gq_ref[...]).astype(BF16), wuq_ref[...])
        qn_ref[sl, :] = (q[:, 0:512] * scale).astype(BF16)
        cc2 = jnp.concatenate([cc, cc], axis=1)
        ss2 = jnp.concatenate([ss, ss], axis=1)
        qpe_ref[sl, :] = ((q[:, 512:768] * cc2 + q[:, 768:1024] * ss2) * scale).astype(BF16)

        kv_ref[sl, :] = _dot(_rms(p[:, 1792:1920], gkv_ref[...]).astype(BF16), wukv_ref[...]).astype(BF16)

        slab_a = p[:, 1920:2048]
        lane = lax.broadcasted_iota(jnp.int32, slab_a.shape, 1)
        roped = jnp.where(lane < MLA_ROPE, slab_a * cc + _swap_rope_halves(slab_a) * ss, 0.0)
        tiled = roped + pltpu.roll(roped, 32, 1) + pltpu.roll(roped, 64, 1) + pltpu.roll(roped, 96, 1)
        kpe_ref[sl, :] = tiled.astype(BF16)
        flt_ref[:, sl] = slab_a.T[MLA_ROPE:MLA_ROPE + FOX_HEADS, :]


def _in_proj(x2d, g, w, gq, wuq, gkv, wukv, cc, ss):
    n = x2d.shape[0]
    rows = min(ROW_TILE, n)
    row = lambda c: pl.BlockSpec((rows, c), lambda i: (i, 0))
    full = lambda a: pl.BlockSpec(a.shape, lambda i: (0,) * a.ndim)
    out_shape = (
        jax.ShapeDtypeStruct((n, HEAD_W), BF16), jax.ShapeDtypeStruct((n, HEAD_W), BF16),
        jax.ShapeDtypeStruct((n, HEAD_W), BF16), jax.ShapeDtypeStruct((n, HEAD_W), BF16),
        jax.ShapeDtypeStruct((n, 256), BF16), jax.ShapeDtypeStruct((n, 1024), BF16),
        jax.ShapeDtypeStruct((n, LANES), BF16), jax.ShapeDtypeStruct((FOX_HEADS, n), F32),
    )
    return pl.pallas_call(
        _in_proj_kernel,
        grid=(n // rows,),
        in_specs=[row(D_MODEL), full(g), full(w), full(gq), full(wuq), full(gkv), full(wukv),
                  row(LANES), row(LANES)],
        out_specs=(row(HEAD_W), row(HEAD_W), row(HEAD_W), row(HEAD_W), row(256), row(1024), row(LANES),
                   pl.BlockSpec((FOX_HEADS, rows), lambda i: (0, i))),
        out_shape=out_shape,
        compiler_params=_params(("parallel",)),
        name="in_proj",
    )(x2d, g, w, gq, wuq, gkv, wukv, cc, ss)


def _fox_decay_kernel(fl_ref, b_ref, tri_ref, ck_ref):
    z = fl_ref[...] + b_ref[...]
    lf = jnp.minimum(z, 0.0) - jnp.log1p(jnp.exp(-jnp.abs(z)))
    seq = lf.shape[1]
    tri = tri_ref[...]
    carry = jnp.zeros((FOX_HEADS, 1), F32)
    zeros = jnp.zeros((FOX_HEADS, CUM_TILE), F32)
    for j in range(seq // CUM_TILE):
        v = lf[:, j * CUM_TILE:(j + 1) * CUM_TILE]
        hi = v.astype(BF16).astype(F32)
        r1 = v - hi
        mid = r1.astype(BF16).astype(F32)
        lo = r1 - mid
        parts = _dot(jnp.concatenate([hi, mid, lo, zeros], axis=0).astype(BF16), tri)
        cs = (parts[0:8] + parts[8:16]) + parts[16:24] + carry
        carry = cs[:, CUM_TILE - 1:CUM_TILE]
        d = cs * (-LOG2E)
        d_hi = d.astype(BF16).astype(F32)
        d_r = d - d_hi
        d_mid = d_r.astype(BF16).astype(F32)
        d_lo = d_r - d_mid
        rows = jnp.concatenate([d_hi, d_mid, d_lo, jnp.zeros((LANES - 3 * FOX_HEADS, CUM_TILE), F32)], axis=0)
        ck_ref[0, j * CUM_TILE:(j + 1) * CUM_TILE, :] = rows.T.astype(BF16)


def _fox_decay(flt, b_col, tri, batch, seq):
    return pl.pallas_call(
        _fox_decay_kernel,
        grid=(batch,),
        in_specs=[pl.BlockSpec((FOX_HEADS, seq), lambda b: (0, b)),
                  pl.BlockSpec((FOX_HEADS, 1), lambda b: (0, 0)),
                  pl.BlockSpec((CUM_TILE, CUM_TILE), lambda b: (0, 0))],
        out_specs=pl.BlockSpec((1, seq, LANES), lambda b: (b, 0, 0)),
        out_shape=jax.ShapeDtypeStruct((batch, seq, LANES), BF16),
        compiler_params=_params(("parallel",)),
        name="fox_decay",
    )(flt, b_col, tri)


def _probs(s_off, s_diag):
    m = jnp.max(s_diag, axis=-1, keepdims=True)
    if s_off is not None:
        m = jnp.maximum(m, jnp.max(s_off, axis=-1, keepdims=True))
    p_off = None if s_off is None else jnp.exp2(s_off - m).astype(BF16)
    return p_off, jnp.exp2(s_diag - m).astype(BF16)


def _attend(q_ref, k_ref, v_ref, o_ref, out_lane0, allowed):
    seq = q_ref.shape[1]
    out_lane = lax.broadcasted_iota(jnp.int32, (Q_TILE, LANES), 1)
    units = [(i, hh) for i in range(seq // Q_TILE) for hh in range(2)]

    def scores(i, hh):
        qs, qe = i * Q_TILE, (i + 1) * Q_TILE
        q = q_ref[hh, qs:qe, :]
        kh = hh % k_ref.shape[0]
        s_diag = jnp.where(allowed, _dot_nt(q, k_ref[kh, qs:qe, :]), NEG_BIG)
        s_off = _dot_nt(q, k_ref[kh, 0:qs, :]) if i > 0 else None
        return s_off, s_diag

    def values(i, hh, p_off, p_diag):
        qs, qe = i * Q_TILE, (i + 1) * Q_TILE
        vh = hh % v_ref.shape[0]
        acc = _dot(p_diag, v_ref[vh, qs:qe, :])
        if p_off is not None:
            acc = acc + _dot(p_off, v_ref[vh, 0:qs, :])
        return acc[:, 0:LANES] / acc[:, LANES:2 * LANES]

    n_units = len(units)
    s = {0: scores(*units[0])}
    if n_units > 1:
        s[1] = scores(*units[1])
    p = {0: _probs(*s.pop(0))}
    outs = []
    for n, (i, hh) in enumerate(units):
        if n + 2 < n_units:
            s[n + 2] = scores(*units[n + 2])
        if n + 1 < n_units:
            p[n + 1] = _probs(*s.pop(n + 1))
        outs.append(values(i, hh, *p.pop(n)))
        if hh == 1:
            o_ref[0, i * Q_TILE:(i + 1) * Q_TILE, out_lane0:out_lane0 + LANES] = jnp.where(
                out_lane < 64, outs[0], outs[1]).astype(BF16)
            outs = []


def _tile_iota():
    row = lax.broadcasted_iota(jnp.int32, (Q_TILE, Q_TILE), 0)
    col = lax.broadcasted_iota(jnp.int32, (Q_TILE, Q_TILE), 1)
    return row, col


def _mla_attn_kernel(qn_ref, qpe_ref, kv_ref, kpe_ref, o_ref, qs_all, ks_all, vs_all):
    seq = qn_ref.shape[1]
    lane = lax.broadcasted_iota(jnp.int32, (seq, LANES), 1)
    qpe = qpe_ref[0]
    kpe = kpe_ref[0]
    ones = jnp.ones((seq, LANES), BF16)
    row, col = _tile_iota()
    for pp in range(PAIRS_PER_STEP):
        qs_ref, ks_ref, vs_ref = qs_all.at[pp], ks_all.at[pp], vs_all.at[pp]
        qn = qn_ref[0, :, pp * LANES:(pp + 1) * LANES]
        for hh in range(2):
            nope_mask = (lane >= 64) if hh == 0 else (lane < 64)
            pe_mask = (lane // MLA_ROPE) == 2 * pp + hh
            qs_ref[hh, :, 0:LANES] = jnp.where(nope_mask, qn, jnp.zeros_like(qn))
            qs_ref[hh, :, LANES:2 * LANES] = jnp.where(pe_mask, qpe, jnp.zeros_like(qpe))
            kvh = kv_ref[0, :, (2 * pp + hh) * LANES:(2 * pp + hh + 1) * LANES]
            ks_ref[hh, :, 0:LANES] = kvh
            ks_ref[hh, :, LANES:2 * LANES] = kpe
            vs_ref[hh, :, 0:LANES] = kvh
            vs_ref[hh, :, LANES:2 * LANES] = ones
        _attend(qs_ref, ks_ref, vs_ref, o_ref, pp * LANES, (col // CHUNK) <= (row // CHUNK))


def _mla_attn(qn, qpe, kv, kpe, batch, seq):
    qn, qpe, kv, kpe = (a.reshape(batch, seq, a.shape[-1]) for a in (qn, qpe, kv, kpe))
    pair = pltpu.VMEM((PAIRS_PER_STEP, 2, seq, 2 * LANES), BF16)
    width = PAIRS_PER_STEP * LANES
    return pl.pallas_call(
        _mla_attn_kernel,
        grid=(batch, MLA_HEADS // (2 * PAIRS_PER_STEP)),
        in_specs=[pl.BlockSpec((1, seq, width), lambda b, j: (b, 0, j)),
                  pl.BlockSpec((1, seq, LANES), lambda b, j: (b, 0, j)),
                  pl.BlockSpec((1, seq, 2 * width), lambda b, j: (b, 0, j)),
                  pl.BlockSpec((1, seq, LANES), lambda b, j: (b, 0, 0))],
        out_specs=pl.BlockSpec((1, seq, width), lambda b, j: (b, 0, j)),
        out_shape=jax.ShapeDtypeStruct((batch, seq, HEAD_W), BF16),
        scratch_shapes=[pair, pair, pair],
        compiler_params=_params(("parallel", "parallel")),
        name="mla_attn",
    )(qn, qpe, kv, kpe)


def _fox_attn_kernel(q_ref, k_ref, v_ref, ck_ref, o_ref, qs_all, ks_all, vs_all):
    j = pl.program_id(1)
    seq = q_ref.shape[1]
    lane = lax.broadcasted_iota(jnp.int32, (seq, LANES), 1)
    row, col = _tile_iota()
    for pp in range(PAIRS_PER_STEP):
        qs_ref, ks_ref, vs_ref = qs_all.at[pp], ks_all.at[pp], vs_all.at[pp]
        chunk = slice(pp * LANES, (pp + 1) * LANES)
        q = q_ref[0, :, chunk]
        ks_ref[0, :, 0:LANES] = k_ref[0, :, chunk]
        ks_ref[0, :, LANES:2 * LANES] = ck_ref[0]
        vs_ref[0, :, 0:LANES] = v_ref[0, :, chunk]
        vs_ref[0, :, LANES:2 * LANES] = jnp.ones((seq, LANES), BF16)
        for hh in range(2):
            head = 2 * (PAIRS_PER_STEP * j + pp) + hh
            head_mask = (lane < 64) if hh == 0 else (lane >= 64)
            piece = jnp.where(lane < 3 * FOX_HEADS, lane % FOX_HEADS, -1) == head
            qs_ref[hh, :, 0:LANES] = jnp.where(head_mask, q, jnp.zeros_like(q))
            qs_ref[hh, :, LANES:2 * LANES] = jnp.where(piece, 1.0, 0.0).astype(BF16)
        _attend(qs_ref, ks_ref, vs_ref, o_ref, pp * LANES, col <= row)


def _fox_attn(fq, fk, fv, ck, batch, seq):
    fq, fk, fv = (a.reshape(batch, seq, HEAD_W) for a in (fq, fk, fv))
    spec = pl.BlockSpec((1, seq, PAIRS_PER_STEP * LANES), lambda b, j: (b, 0, j))
    pair = pltpu.VMEM((PAIRS_PER_STEP, 2, seq, 2 * LANES), BF16)
    shared = pltpu.VMEM((PAIRS_PER_STEP, 1, seq, 2 * LANES), BF16)
    return pl.pallas_call(
        _fox_attn_kernel,
        grid=(batch, FOX_HEADS // (2 * PAIRS_PER_STEP)),
        in_specs=[spec, spec, spec, pl.BlockSpec((1, seq, LANES), lambda b, j: (b, 0, 0))],
        out_specs=spec,
        out_shape=jax.ShapeDtypeStruct((batch, seq, HEAD_W), BF16),
        scratch_shapes=[pair, shared, shared],
        compiler_params=_params(("parallel", "parallel")),
        name="fox_attn",
    )(fq, fk, fv, ck)


def _lane_max(v):
    return jnp.max(v, axis=1, keepdims=True)


def _first_lane(hit, lane_f):
    return jnp.min(jnp.where(hit, lane_f, float(LANES)), axis=1, keepdims=True)


def _out_proj_kernel(om_ref, of_ref, x_ref, gm_ref, gf_ref, wo_ref, gn_ref, wr_ref, br_ref, tri_ref,
                     x1_ref, h2_ref, rt_ref, rtt_ref, cnt_ref, carry_ref):
    @pl.when(pl.program_id(0) == 0)
    def _():
        carry_ref[...] = jnp.zeros_like(carry_ref)

    subs = _sub_slices(x_ref.shape[0])
    mixed = [_mix_heads(sl, om_ref, of_ref, gm_ref, gf_ref, wo_ref) for sl in subs]
    logits = [_residual_and_logits(sl, m, x_ref, gn_ref, wr_ref, br_ref, x1_ref, h2_ref) for sl, m in zip(subs, mixed)]
    carry = carry_ref[...]
    for sl, lg in zip(subs, logits):
        rt, carry = _route_rows(lg, tri_ref[...], carry)
        rt_ref[sl, :] = rt
        rtt_ref[:, sl] = rt.T[0:ROUTE_ROWS, :]
    carry_ref[...] = carry
    cnt_ref[...] = carry


def _mix_heads(sl, om_ref, of_ref, gm_ref, gf_ref, wo_ref):
    a = _rms(om_ref[sl, :].astype(F32), gm_ref[...]).astype(BF16)
    b = _rms(of_ref[sl, :].astype(F32), gf_ref[...]).astype(BF16)
    return _dot(a, wo_ref[0:HEAD_W, :]) + _dot(b, wo_ref[HEAD_W:2 * HEAD_W, :])


def _residual_and_logits(sl, mixed, x_ref, gn_ref, wr_ref, br_ref, x1_ref, h2_ref):
    x1 = x_ref[sl, :] + mixed
    x1_ref[sl, :] = x1
    h2 = _rms(x1, gn_ref[...])
    h_hi = h2.astype(BF16)
    h2_ref[sl, :] = _pack_rows(h_hi.astype(F32))
    h_lo = (h2 - h_hi.astype(F32)).astype(BF16)
    t = _dot(h_hi, wr_ref[...])
    return (t[:, 0:LANES] + t[:, LANES:2 * LANES]) + _dot(h_lo, wr_ref[:, 0:LANES]) + br_ref[...]


def _route_rows(lg, tri, carry):
    lane = lax.broadcasted_iota(jnp.int32, lg.shape, 1)
    lane_f = lane.astype(F32)
    neg_inf = float("-inf")
    is_group = lane < N_GROUPS
    gl = jnp.where(is_group, lg, neg_inf)
    mg = _lane_max(gl)
    gi = _first_lane(gl == mg, lane_f)
    g_val = 1.0 / jnp.sum(jnp.where(is_group, jnp.exp(lg - mg), 0.0), axis=1, keepdims=True)
    group_of_lane = ((lane - N_GROUPS) >> 3).astype(F32)
    is_expert = (lane >= N_GROUPS) & (lane < N_GROUPS + N_EXPERTS) & (group_of_lane == gi)
    el = jnp.where(is_expert, lg, neg_inf)
    m1 = _lane_max(el)
    i1 = _first_lane(el == m1, lane_f)
    el2 = jnp.where(lane_f == i1, neg_inf, el)
    m2 = _lane_max(el2)
    i2 = _first_lane(el2 == m2, lane_f)
    r = jnp.exp(m2 - m1)
    g0 = g_val / (1.0 + r)
    g1 = g0 * r

    hit1 = lane_f == i1
    hit2 = lane_f == i2
    onehot = jnp.where(hit1 | hit2, 1.0, 0.0)
    before = _dot(tri, onehot.astype(BF16)) + carry
    r0 = jnp.sum(jnp.where(hit1, before, 0.0), axis=1, keepdims=True)
    r1 = jnp.sum(jnp.where(hit2, before, 0.0), axis=1, keepdims=True)

    vals = (i1 - N_GROUPS, i2 - N_GROUPS, g0, g1, r0, r1)
    rt = jnp.zeros_like(lg)
    for k, v in enumerate(vals):
        rt = jnp.where(lane == k, v, rt)
    return rt, carry + jnp.sum(onehot, axis=0, keepdims=True)


def _out_proj(om, of, x2d, gm, gf, wo, gn, wr, br, tri):
    n = x2d.shape[0]
    rows = tri.shape[0] * SUB_TILES
    row = lambda c: pl.BlockSpec((rows, c), lambda i: (i, 0))
    full = lambda a: pl.BlockSpec(a.shape, lambda i: (0,) * a.ndim)
    return pl.pallas_call(
        _out_proj_kernel,
        grid=(n // rows,),
        in_specs=[row(HEAD_W), row(HEAD_W), row(D_MODEL), full(gm), full(gf), full(wo), full(gn), full(wr),
                  full(br), full(tri)],
        out_specs=(row(D_MODEL), row(D_MODEL // 2), row(LANES), pl.BlockSpec((ROUTE_ROWS, rows), lambda i: (0, i)),
                   pl.BlockSpec((1, LANES), lambda i: (0, 0))),
        out_shape=(jax.ShapeDtypeStruct((n, D_MODEL), F32), jax.ShapeDtypeStruct((n, D_MODEL // 2), jnp.uint32),
                   jax.ShapeDtypeStruct((n, LANES), F32), jax.ShapeDtypeStruct((ROUTE_ROWS, n), F32),
                   jax.ShapeDtypeStruct((1, LANES), F32)),
        scratch_shapes=[pltpu.VMEM((1, LANES), F32)],
        compiler_params=_params(("arbitrary",)),
        name="out_proj",
    )(om, of, x2d, gm, gf, wo, gn, wr, br, tri)


def _gather_rows(table, idx):
    m, w = idx.shape[0], table.shape[1]
    workers = SC_CORES * SC_SUBCORES
    per_worker = m // workers
    n_win = per_worker // GATHER_WINDOW
    assert per_worker * workers == m and n_win * GATHER_WINDOW == per_worker and n_win % 2 == 0
    mesh = plsc.VectorSubcoreMesh(core_axis_name="core", subcore_axis_name="subcore")

    @functools.partial(
        pl.kernel, mesh=mesh, out_type=jax.ShapeDtypeStruct((m, w), table.dtype),
        scratch_types=[pltpu.VMEM((per_worker,), jnp.int32), pltpu.VMEM((2, GATHER_WINDOW, w), table.dtype),
                       pltpu.SemaphoreType.DMA((2,)), pltpu.SemaphoreType.DMA((2,))],
        name="gather_rows")
    def gather(table_hbm, idx_hbm, out_hbm, idx_v, rows_v, gather_sems, write_sems):
        base = (lax.axis_index("subcore") * SC_CORES + lax.axis_index("core")) * per_worker
        pltpu.sync_copy(idx_hbm.at[pl.ds(base, per_worker)], idx_v)

        def gather_copy(win, buf):
            rows = idx_v.at[pl.ds(win * GATHER_WINDOW, GATHER_WINDOW)]
            return pltpu.make_async_copy(table_hbm.at[rows], rows_v.at[buf], gather_sems.at[buf])

        def write_copy(win, buf):
            dst = out_hbm.at[pl.ds(base + win * GATHER_WINDOW, GATHER_WINDOW)]
            return pltpu.make_async_copy(rows_v.at[buf], dst, write_sems.at[buf])

        gather_copy(0, 0).start()

        @pl.loop(0, n_win, step=2)
        def _(j):
            for buf in range(2):
                win = j + buf
                gather_copy(win, buf).wait()
                write_copy(win, buf).start()

                @pl.when(win + 1 < n_win)
                def _():
                    @pl.when(win >= 1)
                    def _():
                        write_copy(win - 1, 1 - buf).wait()

                    gather_copy(win + 1, 1 - buf).start()

        write_copy(n_win - 2, 0).wait()
        write_copy(n_win - 1, 1).wait()

    return gather(table, idx)


def _scatter_rows(rows, dests, n_out):
    n, w = rows.shape
    workers = SC_CORES * SC_SUBCORES
    per_worker = n // workers
    n_win = per_worker // GATHER_WINDOW
    n_lists = len(dests)
    assert per_worker * workers == n and n_win * GATHER_WINDOW == per_worker and n_win % 2 == 0
    mesh = plsc.VectorSubcoreMesh(core_axis_name="core", subcore_axis_name="subcore")

    @functools.partial(
        pl.kernel, mesh=mesh, out_type=jax.ShapeDtypeStruct((n_out, w), rows.dtype),
        scratch_types=[pltpu.VMEM((per_worker,), jnp.int32)] * n_lists + [
            pltpu.VMEM((2, GATHER_WINDOW, w), rows.dtype), pltpu.SemaphoreType.DMA((2,)),
            pltpu.SemaphoreType.DMA((2,))],
        name="scatter_rows")
    def scatter(rows_hbm, *refs):
        dest_hbm, out_hbm = refs[:n_lists], refs[n_lists]
        dest_v = refs[n_lists + 1:2 * n_lists + 1]
        rows_v, read_sems, write_sems = refs[2 * n_lists + 1:]
        base = (lax.axis_index("subcore") * SC_CORES + lax.axis_index("core")) * per_worker
        for d_hbm, d_v in zip(dest_hbm, dest_v):
            pltpu.sync_copy(d_hbm.at[pl.ds(base, per_worker)], d_v)

        def read_copy(win, buf):
            src = rows_hbm.at[pl.ds(base + win * GATHER_WINDOW, GATHER_WINDOW)]
            return pltpu.make_async_copy(src, rows_v.at[buf], read_sems.at[buf])

        def write_copies(win, buf):
            return [pltpu.make_async_copy(rows_v.at[buf],
                                          out_hbm.at[d_v.at[pl.ds(win * GATHER_WINDOW, GATHER_WINDOW)]],
                                          write_sems.at[buf]) for d_v in dest_v]

        read_copy(0, 0).start()

        @pl.loop(0, n_win, step=2)
        def _(j):
            for buf in range(2):
                win = j + buf
                read_copy(win, buf).wait()
                for c in write_copies(win, buf):
                    c.start()

                @pl.when(win + 1 < n_win)
                def _():
                    @pl.when(win >= 1)
                    def _():
                        for c in write_copies(win - 1, 1 - buf):
                            c.wait()

                    read_copy(win + 1, 1 - buf).start()

        for c in write_copies(n_win - 2, 0) + write_copies(n_win - 1, 1):
            c.wait()

    return scatter(rows, *dests)


def _experts_kernel(be_ref, bv_ref, xs_ref, wg_ref, wu_ref, wd_ref, o_ref, wg_s, wu_s, wd_s):
    i = pl.program_id(0)
    valid = bv_ref[i] != 0

    @pl.when(jnp.logical_and(valid, jnp.logical_or(i == 0, be_ref[i] != be_ref[jnp.maximum(i - 1, 0)])))
    def _():
        wg_s[...] = wg_ref[0, 0].astype(BF16)
        wu_s[...] = wu_ref[0, 0].astype(BF16)
        wd_s[...] = wd_ref[0, 0].astype(BF16)

    @pl.when(valid)
    def _():
        row = lax.broadcasted_iota(jnp.int32, xs_ref.shape, 0)
        packed = jnp.where(row < bv_ref[i], xs_ref[...], jnp.uint32(0))
        subs = _sub_slices(EXPERT_TILE)
        gate_up = []
        for sl in subs:
            x = _unpack_rows(packed[sl, :]).astype(BF16)
            gate_up.append((_dot(x, wg_s[...]), _dot(x, wu_s[...])))
        for sl, (g, u) in zip(subs, gate_up):
            act = ((g * jax.nn.sigmoid(g)) * u).astype(BF16)
            o_ref[sl, :] = _pack_rows(_dot(act, wd_s[...]).astype(BF16).astype(F32))

    @pl.when(jnp.logical_not(valid))
    def _():
        o_ref[...] = jnp.zeros_like(o_ref)


def _experts(layer, blk_expert, blk_rows, xs, w_gate, w_up, w_down):
    n_slots = xs.shape[0]
    n_blocks = n_slots // EXPERT_TILE
    grid_spec = pltpu.PrefetchScalarGridSpec(
        num_scalar_prefetch=2,
        grid=(n_blocks,),
        in_specs=[pl.BlockSpec((EXPERT_TILE, D_MODEL // 2), lambda i, be, bv: (i, 0)),
                  pl.BlockSpec((1, 1, D_MODEL, D_EXPERT), lambda i, be, bv: (layer, be[i], 0, 0)),
                  pl.BlockSpec((1, 1, D_MODEL, D_EXPERT), lambda i, be, bv: (layer, be[i], 0, 0)),
                  pl.BlockSpec((1, 1, D_EXPERT, D_MODEL), lambda i, be, bv: (layer, be[i], 0, 0))],
        out_specs=pl.BlockSpec((EXPERT_TILE, D_MODEL // 2), lambda i, be, bv: (i, 0)),
        scratch_shapes=[pltpu.VMEM((D_MODEL, D_EXPERT), BF16), pltpu.VMEM((D_MODEL, D_EXPERT), BF16),
                        pltpu.VMEM((D_EXPERT, D_MODEL), BF16)],
    )
    return pl.pallas_call(
        _experts_kernel,
        grid_spec=grid_spec,
        out_shape=jax.ShapeDtypeStruct((n_slots, D_MODEL // 2), jnp.uint32),
        compiler_params=_params(("arbitrary",)),
        name="experts",
    )(blk_expert, blk_rows, xs, w_gate, w_up, w_down)


def _combine_kernel(x_ref, y0_ref, y1_ref, rt_ref, g_ref, o_ref, *, final):
    g0 = rt_ref[:, TOP_K:TOP_K + 1]
    g1 = rt_ref[:, TOP_K + 1:TOP_K + 2]
    x = x_ref[...] + (_unpack_rows(y0_ref[...]) * g0 + _unpack_rows(y1_ref[...]) * g1)
    o_ref[...] = _rms(x, g_ref[...]) if final else x


def _combine(x1, y01, route, g, final):
    n = x1.shape[0]
    rows = min(COMBINE_TILE, n)
    steps = n // rows
    row = pl.BlockSpec((rows, D_MODEL), lambda i: (i, 0))
    return pl.pallas_call(
        functools.partial(_combine_kernel, final=final),
        grid=(steps,),
        in_specs=[row, pl.BlockSpec((rows, D_MODEL // 2), lambda i: (i, 0)),
                  pl.BlockSpec((rows, D_MODEL // 2), lambda i: (i + steps, 0)),
                  pl.BlockSpec((rows, LANES), lambda i: (i, 0)),
                  pl.BlockSpec((1, D_MODEL), lambda i: (0, 0))],
        out_specs=row,
        out_shape=jax.ShapeDtypeStruct((n, D_MODEL), F32),
        compiler_params=_params(("parallel",)),
        name="combine",
    )(x1, y01, y01, route, g)


def _prep_in_weights(w_in):
    src = np.full((IN_COLS,), -1, np.int32)
    src[0:1536] = np.arange(416, 1952)
    src[1536:1920] = np.arange(0, 384)
    src[1920:1920 + MLA_ROPE] = np.arange(384, 416)
    src[1920 + MLA_ROPE:1920 + MLA_ROPE + FOX_HEADS] = np.arange(1952, 1960)
    scale = np.where(np.arange(IN_COLS) < HEAD_W, FOX_DIM ** -0.5, 1.0).astype(np.float32)
    rows = lax.broadcasted_iota(jnp.int32, (w_in.shape[1], IN_COLS), 0)
    select = jnp.where(rows == jnp.asarray(src)[None, :], jnp.asarray(scale)[None, :], 0.0).astype(BF16)
    return jnp.dot(w_in.astype(BF16), select, preferred_element_type=F32).astype(BF16)


def _prep_uq(w_uq):
    w = w_uq.reshape(MLA_Q_RANK, MLA_HEADS, MLA_NOPE + MLA_ROPE)
    nope, pe = w[:, :, :MLA_NOPE], w[:, :, MLA_NOPE:]
    pairs = nope.reshape(MLA_Q_RANK, MLA_HEADS // 2, 2, MLA_NOPE)[:, :, ::-1, :].reshape(MLA_Q_RANK, -1)
    half = MLA_ROPE // 2
    pe_swapped = jnp.concatenate([pe[..., half:], pe[..., :half]], axis=-1)
    return jnp.concatenate([pairs, pe.reshape(MLA_Q_RANK, -1), pe_swapped.reshape(MLA_Q_RANK, -1)],
                           axis=1).astype(BF16)


def _prep_ukv(w_ukv):
    w = w_ukv.reshape(MLA_KV_RANK, MLA_HEADS // 2, 2, 2, MLA_NOPE)
    even = w[:, :, 0, ::-1, :]
    odd = w[:, :, 1, :, :]
    return jnp.stack([even, odd], axis=2).reshape(MLA_KV_RANK, -1).astype(BF16)


def _prep_router(w_rg, b_rg, w_re, b_re):
    d = w_rg.shape[0]
    w = jnp.concatenate([w_rg, w_re, jnp.zeros((d, LANES - N_GROUPS - N_EXPERTS), F32)], axis=1)
    w_hi = w.astype(BF16)
    w_lo = (w - w_hi.astype(F32)).astype(BF16)
    b = jnp.concatenate([b_rg, b_re, jnp.zeros((LANES - N_GROUPS - N_EXPERTS,), F32)])[None, :]
    return jnp.concatenate([w_hi, w_lo], axis=1), b


def _rope_slabs(positions):
    half = MLA_ROPE // 2
    inv_freq = ROPE_THETA ** (-jnp.arange(half, dtype=F32) / half)
    ang = positions.astype(F32).reshape(-1)[:, None] * inv_freq
    lane = np.arange(LANES)
    spread = (np.arange(half)[:, None] == (lane % half)[None, :]).astype(np.float32)
    sign = np.where(lane % MLA_ROPE < half, -1.0, 1.0).astype(np.float32)
    expand = functools.partial(jnp.dot, precision=lax.Precision.HIGHEST)
    return expand(jnp.cos(ang), jnp.asarray(spread)), expand(jnp.sin(ang), jnp.asarray(spread * sign))


def _slot_layout(route_t, counts, n_blocks):
    eid = route_t[0:TOP_K].astype(jnp.int32)
    rank = route_t[2 * TOP_K:3 * TOP_K].astype(jnp.int32)
    counts = counts[0, N_GROUPS:N_GROUPS + N_EXPERTS].astype(jnp.int32)
    padded = (counts + EXPERT_TILE - 1) // EXPERT_TILE * EXPERT_TILE
    pad_end = jnp.cumsum(padded)
    pad_start = pad_end - padded
    experts = jnp.arange(N_EXPERTS, dtype=jnp.int32)
    dest = jnp.sum(jnp.where(eid[None] == experts[:, None, None], pad_start[:, None, None], 0), axis=0) + rank
    blk_start = jnp.arange(n_blocks, dtype=jnp.int32) * EXPERT_TILE
    blk_expert = jnp.minimum(jnp.sum((blk_start[:, None] >= pad_end[None, :]).astype(jnp.int32), axis=1),
                             N_EXPERTS - 1)
    in_expert = jnp.sum(jnp.where(blk_expert[:, None] == experts, counts + pad_start, 0), axis=-1) - blk_start
    blk_rows = jnp.where(blk_start < pad_end[-1], jnp.clip(in_expert, 0, EXPERT_TILE), 0).astype(jnp.int32)
    return dest, blk_expert, blk_rows


def kernel(x, positions, attn_norm, w_in, b_forget, q_norm, w_uq, kv_norm, w_ukv, mla_out_norm, fox_out_norm,
           w_out, ffn_norm, w_router_group, b_router_group, w_router_expert, b_router_expert, w_gate, w_up,
           w_down, final_norm):
    batch, seq, d = x.shape
    n = batch * seq
    depth = w_in.shape[0]
    n_blocks = -(-(n * TOP_K) // EXPERT_TILE) + N_EXPERTS
    cc, ss = _rope_slabs(positions)
    tri = (jnp.arange(CUM_TILE)[:, None] <= jnp.arange(CUM_TILE)[None, :]).astype(BF16)
    sub = min(ROW_TILE, n) // SUB_TILES
    tri_rows = (jnp.arange(sub)[None, :] < jnp.arange(sub)[:, None]).astype(BF16)
    xf = x.reshape(n, d)
    for l in range(depth):
        fq, fk, fv, qn, qpe, kv, kpe, flt = _in_proj(
            xf, attn_norm[l][None, :], _prep_in_weights(w_in[l]), q_norm[l][None, :], _prep_uq(w_uq[l]),
            kv_norm[l][None, :], _prep_ukv(w_ukv[l]), cc, ss)
        ck = _fox_decay(flt, b_forget[l][:, None], tri, batch, seq)
        o_mla = _mla_attn(qn, qpe, kv, kpe, batch, seq).reshape(n, HEAD_W)
        o_fox = _fox_attn(fq, fk, fv, ck, batch, seq).reshape(n, HEAD_W)
        wr, br = _prep_router(w_router_group[l], b_router_group[l], w_router_expert[l], b_router_expert[l])
        x1, h2, route, route_t, counts = _out_proj(
            o_mla, o_fox, xf, mla_out_norm[l][None, :], fox_out_norm[l][None, :], w_out[l].astype(BF16),
            ffn_norm[l][None, :], wr, br, tri_rows)
        dest, blk_expert, blk_rows = _slot_layout(route_t, counts, n_blocks)
        xs = _scatter_rows(h2, [dest[k] for k in range(TOP_K)], n_blocks * EXPERT_TILE)
        ys = _experts(l, blk_expert, blk_rows, xs, w_gate, w_up, w_down)
        y01 = _gather_rows(ys, dest.reshape(-1))
        final = l == depth - 1
        xf = _combine(x1, y01, route, final_norm[None, :] if final else ffn_norm[l][None, :], final)
    return xf.reshape(batch, seq, d)
```

```python
import functools

import jax
import numpy as np
import jax.numpy as jnp
from jax import lax
from jax.experimental import pallas as pl
from jax.experimental.pallas import tpu as pltpu
from jax.experimental.pallas import tpu_sc as plsc

D_MODEL = 1024
CHUNK = 64
MLA_HEADS = 8
MLA_NOPE = 64
MLA_ROPE = 32
MLA_V = 64
MLA_Q_RANK = 256
MLA_KV_RANK = 128
ROPE_THETA = 10000.0
FOX_HEADS = 8
FOX_DIM = 64
HEAD_W = 512
N_GROUPS = 4
EXPERTS_PER_GROUP = 8
N_EXPERTS = 32
TOP_K = 2
D_EXPERT = 512
NORM_EPS = 1e-6

LANES = 128
IN_COLS = 2048
ROW_TILE = 512
SUB_TILES = 2
COMBINE_TILE = 1024
COMBINE_PARTS = 2
Q_TILE = 256
PAIRS_PER_STEP = 2
EXPERT_TILE = 512
CUM_TILE = 256
NEG_BIG = -1e30
LOG2E = 1.4426950408889634
VMEM_LIMIT = 48 * 1024 * 1024
SC_CORES = 2
SC_SUBCORES = 16
GATHER_WINDOW = 64
ROUTE_ROWS = 8

F32 = jnp.float32
BF16 = jnp.bfloat16


def _rms(x, g):
    return (x * lax.rsqrt(jnp.mean(x * x, axis=-1, keepdims=True) + NORM_EPS)) * g


def _dot(a, b):
    return jnp.dot(a, b, preferred_element_type=F32)


def _dot_nt(a, b):
    return lax.dot_general(a, b, (((1,), (1,)), ((), ())), preferred_element_type=F32)


def _params(sem):
    return pltpu.CompilerParams(dimension_semantics=sem, vmem_limit_bytes=VMEM_LIMIT)


def _sub_slices(rows):
    sub = rows // SUB_TILES
    return [slice(t * sub, (t + 1) * sub) for t in range(SUB_TILES)]


def _pack_rows(v):
    w = v.shape[1] // 2
    lo = lax.bitcast_convert_type(v[:, :w], jnp.uint32) >> 16
    hi = lax.bitcast_convert_type(v[:, w:], jnp.uint32) & jnp.uint32(0xFFFF0000)
    return hi | lo


def _unpack_rows(p):
    lo = lax.bitcast_convert_type(p << 16, F32)
    hi = lax.bitcast_convert_type(p & jnp.uint32(0xFFFF0000), F32)
    return jnp.concatenate([lo, hi], axis=1)


def _swap_rope_halves(x):
    half = MLA_ROPE // 2
    tiles = []
    for c in range(x.shape[1] // LANES):
        t = x[:, c * LANES:(c + 1) * LANES]
        lane = lax.broadcasted_iota(jnp.int32, t.shape, 1)
        tiles.append(jnp.where(lane % MLA_ROPE < half, pltpu.roll(t, LANES - half, 1), pltpu.roll(t, half, 1)))
    return tiles[0] if len(tiles) == 1 else jnp.concatenate(tiles, axis=1)


def _in_proj_kernel(x_ref, g_ref, w_ref, gq_ref, wuq_ref, gkv_ref, wukv_ref, cc_ref, ss_ref,
                    fq_ref, fk_ref, fv_ref, qn_ref, qpe_ref, kv_ref, kpe_ref, flt_ref):
    subs = _sub_slices(x_ref.shape[0])
    ps = [_dot(_rms(x_ref[sl, :], g_ref[...]).astype(BF16), w_ref[...]) for sl in subs]
    scale = (MLA_NOPE + MLA_ROPE) ** -0.5 * LOG2E
    for sl, p in zip(subs, ps):
        fq_ref[sl, :] = (p[:, 0:512] * LOG2E).astype(BF16)
        fk_ref[sl, :] = p[:, 512:1024].astype(BF16)
        fv_ref[sl, :] = p[:, 1024:1536].astype(BF16)

        cc = cc_ref[sl, :]
        ss = ss_ref[sl, :]
        q = _dot(_rms(p[:, 1536:1792], gq_ref[...]).astype(BF16), wuq_ref[...])
        qn_ref[sl, :] = (q[:, 0:512] * scale).astype(BF16)
        cc2 = jnp.concatenate([cc, cc], axis=1)
        ss2 = jnp.concatenate([ss, ss], axis=1)
        qpe_ref[sl, :] = ((q[:, 512:768] * cc2 + q[:, 768:1024] * ss2) * scale).astype(BF16)

        kv_ref[sl, :] = _dot(_rms(p[:, 1792:1920], gkv_ref[...]).astype(BF16), wukv_ref[...]).astype(BF16)

        slab_a = p[:, 1920:2048]
        lane = lax.broadcasted_iota(jnp.int32, slab_a.shape, 1)
        roped = jnp.where(lane < MLA_ROPE, slab_a * cc + _swap_rope_halves(slab_a) * ss, 0.0)
        tiled = roped + pltpu.roll(roped, 32, 1) + pltpu.roll(roped, 64, 1) + pltpu.roll(roped, 96, 1)
        kpe_ref[sl, :] = tiled.astype(BF16)
        flt_ref[:, sl] = slab_a.T[MLA_ROPE:MLA_ROPE + FOX_HEADS, :]


def _in_proj(x2d, g, w, gq, wuq, gkv, wukv, cc, ss):
    n = x2d.shape[0]
    rows = min(ROW_TILE, n)
    row = lambda c: pl.BlockSpec((rows, c), lambda i: (i, 0))
    full = lambda a: pl.BlockSpec(a.shape, lambda i: (0,) * a.ndim)
    out_shape = (
        jax.ShapeDtypeStruct((n, HEAD_W), BF16), jax.ShapeDtypeStruct((n, HEAD_W), BF16),
        jax.ShapeDtypeStruct((n, HEAD_W), BF16), jax.ShapeDtypeStruct((n, HEAD_W), BF16),
        jax.ShapeDtypeStruct((n, 256), BF16), jax.ShapeDtypeStruct((n, 1024), BF16),
        jax.ShapeDtypeStruct((n, LANES), BF16), jax.ShapeDtypeStruct((FOX_HEADS, n), F32),
    )
    return pl.pallas_call(
        _in_proj_kernel,
        grid=(n // rows,),
        in_specs=[row(D_MODEL), full(g), full(w), full(gq), full(wuq), full(gkv), full(wukv),
                  row(LANES), row(LANES)],
        out_specs=(row(HEAD_W), row(HEAD_W), row(HEAD_W), row(HEAD_W), row(256), row(1024), row(LANES),
                   pl.BlockSpec((FOX_HEADS, rows), lambda i: (0, i))),
        out_shape=out_shape,
        compiler_params=_params(("parallel",)),
        name="in_proj",
    )(x2d, g, w, gq, wuq, gkv, wukv, cc, ss)


def _fox_decay_kernel(fl_ref, b_ref, tri_ref, ck_ref):
    z = fl_ref[...] + b_ref[...]
    lf = jnp.minimum(z, 0.0) - jnp.log1p(jnp.exp(-jnp.abs(z)))
    seq = lf.shape[1]
    tri = tri_ref[...]
    carry = jnp.zeros((FOX_HEADS, 1), F32)
    zeros = jnp.zeros((FOX_HEADS, CUM_TILE), F32)
    for j in range(seq // CUM_TILE):
        v = lf[:, j * CUM_TILE:(j + 1) * CUM_TILE]
        hi = v.astype(BF16).astype(F32)
        r1 = v - hi
        mid = r1.astype(BF16).astype(F32)
        lo = r1 - mid
        parts = _dot(jnp.concatenate([hi, mid, lo, zeros], axis=0).astype(BF16), tri)
        cs = (parts[0:8] + parts[8:16]) + parts[16:24] + carry
        carry = cs[:, CUM_TILE - 1:CUM_TILE]
        d = cs * (-LOG2E)
        d_hi = d.astype(BF16).astype(F32)
        d_r = d - d_hi
        d_mid = d_r.astype(BF16).astype(F32)
        d_lo = d_r - d_mid
        rows = jnp.concatenate([d_hi, d_mid, d_lo, jnp.zeros((LANES - 3 * FOX_HEADS, CUM_TILE), F32)], axis=0)
        ck_ref[0, j * CUM_TILE:(j + 1) * CUM_TILE, :] = rows.T.astype(BF16)


def _fox_decay(flt, b_col, tri, batch, seq):
    return pl.pallas_call(
        _fox_decay_kernel,
        grid=(batch,),
        in_specs=[pl.BlockSpec((FOX_HEADS, seq), lambda b: (0, b)),
                  pl.BlockSpec((FOX_HEADS, 1), lambda b: (0, 0)),
                  pl.BlockSpec((CUM_TILE, CUM_TILE), lambda b: (0, 0))],
        out_specs=pl.BlockSpec((1, seq, LANES), lambda b: (b, 0, 0)),
        out_shape=jax.ShapeDtypeStruct((batch, seq, LANES), BF16),
        compiler_params=_params(("parallel",)),
        name="fox_decay",
    )(flt, b_col, tri)


def _probs(s_off, s_diag):
    m = jnp.max(s_diag, axis=-1, keepdims=True)
    if s_off is not None:
        m = jnp.maximum(m, jnp.max(s_off, axis=-1, keepdims=True))
    p_off = None if s_off is None else jnp.exp2(s_off - m).astype(BF16)
    return p_off, jnp.exp2(s_diag - m).astype(BF16)


def _attend(q_ref, k_ref, v_ref, o_ref, out_lane0, allowed):
    seq = q_ref.shape[1]
    out_lane = lax.broadcasted_iota(jnp.int32, (Q_TILE, LANES), 1)
    units = [(i, hh) for i in range(seq // Q_TILE) for hh in range(2)]

    def scores(i, hh):
        qs, qe = i * Q_TILE, (i + 1) * Q_TILE
        q = q_ref[hh, qs:qe, :]
        kh = hh % k_ref.shape[0]
        s_diag = jnp.where(allowed, _dot_nt(q, k_ref[kh, qs:qe, :]), NEG_BIG)
        s_off = _dot_nt(q, k_ref[kh, 0:qs, :]) if i > 0 else None
        return s_off, s_diag

    def values(i, hh, p_off, p_diag):
        qs, qe = i * Q_TILE, (i + 1) * Q_TILE
        vh = hh % v_ref.shape[0]
        acc = _dot(p_diag, v_ref[vh, qs:qe, :])
        if p_off is not None:
            acc = acc + _dot(p_off, v_ref[vh, 0:qs, :])
        return acc[:, 0:LANES] / acc[:, LANES:2 * LANES]

    n_units = len(units)
    s = {0: scores(*units[0])}
    if n_units > 1:
        s[1] = scores(*units[1])
    p = {0: _probs(*s.pop(0))}
    outs = []
    for n, (i, hh) in enumerate(units):
        if n + 2 < n_units:
            s[n + 2] = scores(*units[n + 2])
        if n + 1 < n_units:
            p[n + 1] = _probs(*s.pop(n + 1))
        outs.append(values(i, hh, *p.pop(n)))
        if hh == 1:
            o_ref[0, i * Q_TILE:(i + 1) * Q_TILE, out_lane0:out_lane0 + LANES] = jnp.where(
                out_lane < 64, outs[0], outs[1]).astype(BF16)
            outs = []


def _tile_iota():
    row = lax.broadcasted_iota(jnp.int32, (Q_TILE, Q_TILE), 0)
    col = lax.broadcasted_iota(jnp.int32, (Q_TILE, Q_TILE), 1)
    return row, col


def _mla_attn_kernel(qn_ref, qpe_ref, kv_ref, kpe_ref, o_ref, qs_all, ks_all, vs_all):
    seq = qn_ref.shape[1]
    lane = lax.broadcasted_iota(jnp.int32, (seq, LANES), 1)
    qpe = qpe_ref[0]
    kpe = kpe_ref[0]
    ones = jnp.ones((seq, LANES), BF16)
    row, col = _tile_iota()
    for pp in range(PAIRS_PER_STEP):
        qs_ref, ks_ref, vs_ref = qs_all.at[pp], ks_all.at[pp], vs_all.at[pp]
        qn = qn_ref[0, :, pp * LANES:(pp + 1) * LANES]
        for hh in range(2):
            nope_mask = (lane >= 64) if hh == 0 else (lane < 64)
            pe_mask = (lane // MLA_ROPE) == 2 * pp + hh
            qs_ref[hh, :, 0:LANES] = jnp.where(nope_mask, qn, jnp.zeros_like(qn))
            qs_ref[hh, :, LANES:2 * LANES] = jnp.where(pe_mask, qpe, jnp.zeros_like(qpe))
            kvh = kv_ref[0, :, (2 * pp + hh) * LANES:(2 * pp + hh + 1) * LANES]
            ks_ref[hh, :, 0:LANES] = kvh
            ks_ref[hh, :, LANES:2 * LANES] = kpe
            vs_ref[hh, :, 0:LANES] = kvh
            vs_ref[hh, :, LANES:2 * LANES] = ones
        _attend(qs_ref, ks_ref, vs_ref, o_ref, pp * LANES, (col // CHUNK) <= (row // CHUNK))


def _mla_attn(qn, qpe, kv, kpe, batch, seq):
    qn, qpe, kv, kpe = (a.reshape(batch, seq, a.shape[-1]) for a in (qn, qpe, kv, kpe))
    pair = pltpu.VMEM((PAIRS_PER_STEP, 2, seq, 2 * LANES), BF16)
    width = PAIRS_PER_STEP * LANES
    return pl.pallas_call(
        _mla_attn_kernel,
        grid=(batch, MLA_HEADS // (2 * PAIRS_PER_STEP)),
        in_specs=[pl.BlockSpec((1, seq, width), lambda b, j: (b, 0, j)),
                  pl.BlockSpec((1, seq, LANES), lambda b, j: (b, 0, j)),
                  pl.BlockSpec((1, seq, 2 * width), lambda b, j: (b, 0, j)),
                  pl.BlockSpec((1, seq, LANES), lambda b, j: (b, 0, 0))],
        out_specs=pl.BlockSpec((1, seq, width), lambda b, j: (b, 0, j)),
        out_shape=jax.ShapeDtypeStruct((batch, seq, HEAD_W), BF16),
        scratch_shapes=[pair, pair, pair],
        compiler_params=_params(("parallel", "parallel")),
        name="mla_attn",
    )(qn, qpe, kv, kpe)


def _fox_attn_kernel(q_ref, k_ref, v_ref, ck_ref, o_ref, qs_all, ks_all, vs_all):
    j = pl.program_id(1)
    seq = q_ref.shape[1]
    lane = lax.broadcasted_iota(jnp.int32, (seq, LANES), 1)
    row, col = _tile_iota()
    for pp in range(PAIRS_PER_STEP):
        qs_ref, ks_ref, vs_ref = qs_all.at[pp], ks_all.at[pp], vs_all.at[pp]
        chunk = slice(pp * LANES, (pp + 1) * LANES)
        q = q_ref[0, :, chunk]
        ks_ref[0, :, 0:LANES] = k_ref[0, :, chunk]
        ks_ref[0, :, LANES:2 * LANES] = ck_ref[0]
        vs_ref[0, :, 0:LANES] = v_ref[0, :, chunk]
        vs_ref[0, :, LANES:2 * LANES] = jnp.ones((seq, LANES), BF16)
        for hh in range(2):
            head = 2 * (PAIRS_PER_STEP * j + pp) + hh
            head_mask = (lane < 64) if hh == 0 else (lane >= 64)
            piece = jnp.where(lane < 3 * FOX_HEADS, lane % FOX_HEADS, -1) == head
            qs_ref[hh, :, 0:LANES] = jnp.where(head_mask, q, jnp.zeros_like(q))
            qs_ref[hh, :, LANES:2 * LANES] = jnp.where(piece, 1.0, 0.0).astype(BF16)
        _attend(qs_ref, ks_ref, vs_ref, o_ref, pp * LANES, col <= row)


def _fox_attn(fq, fk, fv, ck, batch, seq):
    fq, fk, fv = (a.reshape(batch, seq, HEAD_W) for a in (fq, fk, fv))
    spec = pl.BlockSpec((1, seq, PAIRS_PER_STEP * LANES), lambda b, j: (b, 0, j))
    pair = pltpu.VMEM((PAIRS_PER_STEP, 2, seq, 2 * LANES), BF16)
    shared = pltpu.VMEM((PAIRS_PER_STEP, 1, seq, 2 * LANES), BF16)
    return pl.pallas_call(
        _fox_attn_kernel,
        grid=(batch, FOX_HEADS // (2 * PAIRS_PER_STEP)),
        in_specs=[spec, spec, spec, pl.BlockSpec((1, seq, LANES), lambda b, j: (b, 0, 0))],
        out_specs=spec,
        out_shape=jax.ShapeDtypeStruct((batch, seq, HEAD_W), BF16),
        scratch_shapes=[pair, shared, shared],
        compiler_params=_params(("parallel", "parallel")),
        name="fox_attn",
    )(fq, fk, fv, ck)


def _lane_max(v):
    return jnp.max(v, axis=1, keepdims=True)


def _first_lane(hit, lane_f):
    return jnp.min(jnp.where(hit, lane_f, float(LANES)), axis=1, keepdims=True)


def _out_proj_kernel(om_ref, of_ref, x_ref, gm_ref, gf_ref, wo_ref, gn_ref, wr_ref, br_ref, tri_ref,
                     x1_ref, h2_ref, rt_ref, rtt_ref, cnt_ref, carry_ref):
    @pl.when(pl.program_id(0) == 0)
    def _():
        carry_ref[...] = jnp.zeros_like(carry_ref)

    subs = _sub_slices(x_ref.shape[0])
    mixed = [_mix_heads(sl, om_ref, of_ref, gm_ref, gf_ref, wo_ref) for sl in subs]
    logits = [_residual_and_logits(sl, m, x_ref, gn_ref, wr_ref, br_ref, x1_ref, h2_ref) for sl, m in zip(subs, mixed)]
    carry = carry_ref[...]
    for sl, lg in zip(subs, logits):
        rt, carry = _route_rows(lg, tri_ref[...], carry)
        rt_ref[sl, :] = rt
        rtt_ref[:, sl] = rt.T[0:ROUTE_ROWS, :]
    carry_ref[...] = carry
    cnt_ref[...] = carry


def _mix_heads(sl, om_ref, of_ref, gm_ref, gf_ref, wo_ref):
    a = _rms(om_ref[sl, :].astype(F32), gm_ref[...]).astype(BF16)
    b = _rms(of_ref[sl, :].astype(F32), gf_ref[...]).astype(BF16)
    return _dot(a, wo_ref[0:HEAD_W, :]) + _dot(b, wo_ref[HEAD_W:2 * HEAD_W, :])


def _residual_and_logits(sl, mixed, x_ref, gn_ref, wr_ref, br_ref, x1_ref, h2_ref):
    x1 = x_ref[sl, :] + mixed
    x1_ref[sl, :] = x1
    h2 = _rms(x1, gn_ref[...])
    h_hi = h2.astype(BF16)
    h2_ref[sl, :] = _pack_rows(h_hi.astype(F32))
    h_lo = (h2 - h_hi.astype(F32)).astype(BF16)
    t = _dot(h_hi, wr_ref[...])
    return (t[:, 0:LANES] + t[:, LANES:2 * LANES]) + _dot(h_lo, wr_ref[:, 0:LANES]) + br_ref[...]


def _route_rows(lg, tri, carry):
    lane = lax.broadcasted_iota(jnp.int32, lg.shape, 1)
    lane_f = lane.astype(F32)
    neg_inf = float("-inf")
    is_group = lane < N_GROUPS
    gl = jnp.where(is_group, lg, neg_inf)
    mg = _lane_max(gl)
    gi = _first_lane(gl == mg, lane_f)
    g_val = 1.0 / jnp.sum(jnp.where(is_group, jnp.exp(lg - mg), 0.0), axis=1, keepdims=True)
    group_of_lane = ((lane - N_GROUPS) >> 3).astype(F32)
    is_expert = (lane >= N_GROUPS) & (lane < N_GROUPS + N_EXPERTS) & (group_of_lane == gi)
    el = jnp.where(is_expert, lg, neg_inf)
    m1 = _lane_max(el)
    i1 = _first_lane(el == m1, lane_f)
    el2 = jnp.where(lane_f == i1, neg_inf, el)
    m2 = _lane_max(el2)
    i2 = _first_lane(el2 == m2, lane_f)
    r = jnp.exp(m2 - m1)
    g0 = g_val / (1.0 + r)
    g1 = g0 * r

    hit1 = lane_f == i1
    hit2 = lane_f == i2
    onehot = jnp.where(hit1 | hit2, 1.0, 0.0)
    before = _dot(tri, onehot.astype(BF16)) + carry
    r0 = jnp.sum(jnp.where(hit1, before, 0.0), axis=1, keepdims=True)
    r1 = jnp.sum(jnp.where(hit2, before, 0.0), axis=1, keepdims=True)

    vals = (i1 - N_GROUPS, i2 - N_GROUPS, g0, g1, r0, r1)
    rt = jnp.zeros_like(lg)
    for k, v in enumerate(vals):
        rt = jnp.where(lane == k, v, rt)
    return rt, carry + jnp.sum(onehot, axis=0, keepdims=True)


def _out_proj(om, of, x2d, gm, gf, wo, gn, wr, br, tri):
    n = x2d.shape[0]
    rows = tri.shape[0] * SUB_TILES
    row = lambda c: pl.BlockSpec((rows, c), lambda i: (i, 0))
    full = lambda a: pl.BlockSpec(a.shape, lambda i: (0,) * a.ndim)
    return pl.pallas_call(
        _out_proj_kernel,
        grid=(n // rows,),
        in_specs=[row(HEAD_W), row(HEAD_W), row(D_MODEL), full(gm), full(gf), full(wo), full(gn), full(wr),
                  full(br), full(tri)],
        out_specs=(row(D_MODEL), row(D_MODEL // 2), row(LANES), pl.BlockSpec((ROUTE_ROWS, rows), lambda i: (0, i)),
                   pl.BlockSpec((1, LANES), lambda i: (0, 0))),
        out_shape=(jax.ShapeDtypeStruct((n, D_MODEL), F32), jax.ShapeDtypeStruct((n, D_MODEL // 2), jnp.uint32),
                   jax.ShapeDtypeStruct((n, LANES), F32), jax.ShapeDtypeStruct((ROUTE_ROWS, n), F32),
                   jax.ShapeDtypeStruct((1, LANES), F32)),
        scratch_shapes=[pltpu.VMEM((1, LANES), F32)],
        compiler_params=_params(("arbitrary",)),
        name="out_proj",
    )(om, of, x2d, gm, gf, wo, gn, wr, br, tri)


def _gather_rows(table, idx):
    m, w = idx.shape[0], table.shape[1]
    workers = SC_CORES * SC_SUBCORES
    per_worker = m // workers
    n_win = per_worker // GATHER_WINDOW
    assert per_worker * workers == m and n_win * GATHER_WINDOW == per_worker and n_win % 2 == 0
    mesh = plsc.VectorSubcoreMesh(core_axis_name="core", subcore_axis_name="subcore")

    @functools.partial(
        pl.kernel, mesh=mesh, out_type=jax.ShapeDtypeStruct((m, w), table.dtype),
        scratch_types=[pltpu.VMEM((per_worker,), jnp.int32), pltpu.VMEM((2, GATHER_WINDOW, w), table.dtype),
                       pltpu.SemaphoreType.DMA((2,)), pltpu.SemaphoreType.DMA((2,))],
        name="gather_rows")
    def gather(table_hbm, idx_hbm, out_hbm, idx_v, rows_v, gather_sems, write_sems):
        base = (lax.axis_index("subcore") * SC_CORES + lax.axis_index("core")) * per_worker
        pltpu.sync_copy(idx_hbm.at[pl.ds(base, per_worker)], idx_v)

        def gather_copy(win, buf):
            rows = idx_v.at[pl.ds(win * GATHER_WINDOW, GATHER_WINDOW)]
            return pltpu.make_async_copy(table_hbm.at[rows], rows_v.at[buf], gather_sems.at[buf])

        def write_copy(win, buf):
            dst = out_hbm.at[pl.ds(base + win * GATHER_WINDOW, GATHER_WINDOW)]
            return pltpu.make_async_copy(rows_v.at[buf], dst, write_sems.at[buf])

        gather_copy(0, 0).start()

        @pl.loop(0, n_win, step=2)
        def _(j):
            for buf in range(2):
                win = j + buf
                gather_copy(win, buf).wait()
                write_copy(win, buf).start()

                @pl.when(win + 1 < n_win)
                def _():
                    @pl.when(win >= 1)
                    def _():
                        write_copy(win - 1, 1 - buf).wait()

                    gather_copy(win + 1, 1 - buf).start()

        write_copy(n_win - 2, 0).wait()
        write_copy(n_win - 1, 1).wait()

    return gather(table, idx)


def _scatter_rows(rows, dests, n_out):
    n, w = rows.shape
    workers = SC_CORES * SC_SUBCORES
    per_worker = n // workers
    n_win = per_worker // GATHER_WINDOW
    n_lists = len(dests)
    assert per_worker * workers == n and n_win * GATHER_WINDOW == per_worker and n_win % 2 == 0
    mesh = plsc.VectorSubcoreMesh(core_axis_name="core", subcore_axis_name="subcore")

    @functools.partial(
        pl.kernel, mesh=mesh, out_type=jax.ShapeDtypeStruct((n_out, w), rows.dtype),
        scratch_types=[pltpu.VMEM((per_worker,), jnp.int32)] * n_lists + [
            pltpu.VMEM((2, GATHER_WINDOW, w), rows.dtype), pltpu.SemaphoreType.DMA((2,)),
            pltpu.SemaphoreType.DMA((2,))],
        name="scatter_rows")
    def scatter(rows_hbm, *refs):
        dest_hbm, out_hbm = refs[:n_lists], refs[n_lists]
        dest_v = refs[n_lists + 1:2 * n_lists + 1]
        rows_v, read_sems, write_sems = refs[2 * n_lists + 1:]
        base = (lax.axis_index("subcore") * SC_CORES + lax.axis_index("core")) * per_worker
        for d_hbm, d_v in zip(dest_hbm, dest_v):
            pltpu.sync_copy(d_hbm.at[pl.ds(base, per_worker)], d_v)

        def read_copy(win, buf):
            src = rows_hbm.at[pl.ds(base + win * GATHER_WINDOW, GATHER_WINDOW)]
            return pltpu.make_async_copy(src, rows_v.at[buf], read_sems.at[buf])

        def write_copies(win, buf):
            return [pltpu.make_async_copy(rows_v.at[buf],
                                          out_hbm.at[d_v.at[pl.ds(win * GATHER_WINDOW, GATHER_WINDOW)]],
                                          write_sems.at[buf]) for d_v in dest_v]

        read_copy(0, 0).start()

        @pl.loop(0, n_win, step=2)
        def _(j):
            for buf in range(2):
                win = j + buf
                read_copy(win, buf).wait()
                for c in write_copies(win, buf):
                    c.start()

                @pl.when(win + 1 < n_win)
                def _():
                    @pl.when(win >= 1)
                    def _():
                        for c in write_copies(win - 1, 1 - buf):
                            c.wait()

                    read_copy(win + 1, 1 - buf).start()

        for c in write_copies(n_win - 2, 0) + write_copies(n_win - 1, 1):
            c.wait()

    return scatter(rows, *dests)


def _experts_kernel(be_ref, bv_ref, xs_ref, wg_ref, wu_ref, wd_ref, o_ref, wg_s, wu_s, wd_s):
    i = pl.program_id(0)
    valid = bv_ref[i] != 0

    @pl.when(jnp.logical_and(valid, jnp.logical_or(i == 0, be_ref[i] != be_ref[jnp.maximum(i - 1, 0)])))
    def _():
        wg_s[...] = wg_ref[0, 0].astype(BF16)
        wu_s[...] = wu_ref[0, 0].astype(BF16)
        wd_s[...] = wd_ref[0, 0].astype(BF16)

    @pl.when(valid)
    def _():
        row = lax.broadcasted_iota(jnp.int32, xs_ref.shape, 0)
        packed = jnp.where(row < bv_ref[i], xs_ref[...], jnp.uint32(0))
        subs = _sub_slices(EXPERT_TILE)
        gate_up = []
        for sl in subs:
            x = _unpack_rows(packed[sl, :]).astype(BF16)
            gate_up.append((_dot(x, wg_s[...]), _dot(x, wu_s[...])))
        for sl, (g, u) in zip(subs, gate_up):
            act = ((g * jax.nn.sigmoid(g)) * u).astype(BF16)
            o_ref[sl, :] = _pack_rows(_dot(act, wd_s[...]).astype(BF16).astype(F32))

    @pl.when(jnp.logical_not(valid))
    def _():
        o_ref[...] = jnp.zeros_like(o_ref)


def _experts(layer, blk_expert, blk_rows, xs, w_gate, w_up, w_down):
    n_slots = xs.shape[0]
    n_blocks = n_slots // EXPERT_TILE
    grid_spec = pltpu.PrefetchScalarGridSpec(
        num_scalar_prefetch=2,
        grid=(n_blocks,),
        in_specs=[pl.BlockSpec((EXPERT_TILE, D_MODEL // 2), lambda i, be, bv: (i, 0)),
                  pl.BlockSpec((1, 1, D_MODEL, D_EXPERT), lambda i, be, bv: (layer, be[i], 0, 0)),
                  pl.BlockSpec((1, 1, D_MODEL, D_EXPERT), lambda i, be, bv: (layer, be[i], 0, 0)),
                  pl.BlockSpec((1, 1, D_EXPERT, D_MODEL), lambda i, be, bv: (layer, be[i], 0, 0))],
        out_specs=pl.BlockSpec((EXPERT_TILE, D_MODEL // 2), lambda i, be, bv: (i, 0)),
        scratch_shapes=[pltpu.VMEM((D_MODEL, D_EXPERT), BF16), pltpu.VMEM((D_MODEL, D_EXPERT), BF16),
                        pltpu.VMEM((D_EXPERT, D_MODEL), BF16)],
    )
    return pl.pallas_call(
        _experts_kernel,
        grid_spec=grid_spec,
        out_shape=jax.ShapeDtypeStruct((n_slots, D_MODEL // 2), jnp.uint32),
        compiler_params=_params(("arbitrary",)),
        name="experts",
    )(blk_expert, blk_rows, xs, w_gate, w_up, w_down)


def _combine_kernel(x_ref, y0_ref, y1_ref, rt_ref, g_ref, *rest, final):
    o_ref = rest[-1]
    g0 = rt_ref[:, TOP_K:TOP_K + 1]
    g1 = rt_ref[:, TOP_K + 1:TOP_K + 2]
    x = x_ref[...] + (_unpack_rows(y0_ref[...]) * g0 + _unpack_rows(y1_ref[...]) * g1)
    o_ref[...] = _rms(x, g_ref[...]) if final else x


def _combine(x1, y01, route, g, final, part, prev):
    n = x1.shape[0]
    rows = min(COMBINE_TILE, n // COMBINE_PARTS)
    steps = n // COMBINE_PARTS // rows
    first = part * steps
    row = pl.BlockSpec((rows, D_MODEL), lambda i: (i + first, 0))
    in_specs = [row, pl.BlockSpec((rows, D_MODEL // 2), lambda i: (i, 0)),
                pl.BlockSpec((rows, D_MODEL // 2), lambda i: (i + steps, 0)),
                pl.BlockSpec((rows, LANES), lambda i: (i + first, 0)),
                pl.BlockSpec((1, D_MODEL), lambda i: (0, 0))]
    args = [x1, y01, y01, route, g]
    aliases = {}
    if prev is not None:
        in_specs.append(pl.BlockSpec(memory_space=pl.ANY))
        args.append(prev)
        aliases = {len(args) - 1: 0}
    return pl.pallas_call(
        functools.partial(_combine_kernel, final=final),
        grid=(steps,),
        in_specs=in_specs,
        out_specs=row,
        out_shape=jax.ShapeDtypeStruct((n, D_MODEL), F32),
        input_output_aliases=aliases,
        compiler_params=_params(("parallel",)),
        name="combine",
    )(*args)


def _prep_in_weights(w_in):
    src = np.full((IN_COLS,), -1, np.int32)
    src[0:1536] = np.arange(416, 1952)
    src[1536:1920] = np.arange(0, 384)
    src[1920:1920 + MLA_ROPE] = np.arange(384, 416)
    src[1920 + MLA_ROPE:1920 + MLA_ROPE + FOX_HEADS] = np.arange(1952, 1960)
    scale = np.where(np.arange(IN_COLS) < HEAD_W, FOX_DIM ** -0.5, 1.0).astype(np.float32)
    rows = lax.broadcasted_iota(jnp.int32, (w_in.shape[1], IN_COLS), 0)
    select = jnp.where(rows == jnp.asarray(src)[None, :], jnp.asarray(scale)[None, :], 0.0).astype(BF16)
    return jnp.dot(w_in.astype(BF16), select, preferred_element_type=F32).astype(BF16)


def _prep_uq(w_uq):
    w = w_uq.reshape(MLA_Q_RANK, MLA_HEADS, MLA_NOPE + MLA_ROPE)
    nope, pe = w[:, :, :MLA_NOPE], w[:, :, MLA_NOPE:]
    pairs = nope.reshape(MLA_Q_RANK, MLA_HEADS // 2, 2, MLA_NOPE)[:, :, ::-1, :].reshape(MLA_Q_RANK, -1)
    half = MLA_ROPE // 2
    pe_swapped = jnp.concatenate([pe[..., half:], pe[..., :half]], axis=-1)
    return jnp.concatenate([pairs, pe.reshape(MLA_Q_RANK, -1), pe_swapped.reshape(MLA_Q_RANK, -1)],
                           axis=1).astype(BF16)


def _prep_ukv(w_ukv):
    w = w_ukv.reshape(MLA_KV_RANK, MLA_HEADS // 2, 2, 2, MLA_NOPE)
    even = w[:, :, 0, ::-1, :]
    odd = w[:, :, 1, :, :]
    return jnp.stack([even, odd], axis=2).reshape(MLA_KV_RANK, -1).astype(BF16)


def _prep_router(w_rg, b_rg, w_re, b_re):
    d = w_rg.shape[0]
    w = jnp.concatenate([w_rg, w_re, jnp.zeros((d, LANES - N_GROUPS - N_EXPERTS), F32)], axis=1)
    w_hi = w.astype(BF16)
    w_lo = (w - w_hi.astype(F32)).astype(BF16)
    b = jnp.concatenate([b_rg, b_re, jnp.zeros((LANES - N_GROUPS - N_EXPERTS,), F32)])[None, :]
    return jnp.concatenate([w_hi, w_lo], axis=1), b


def _rope_slabs(positions):
    half = MLA_ROPE // 2
    inv_freq = ROPE_THETA ** (-jnp.arange(half, dtype=F32) / half)
    ang = positions.astype(F32).reshape(-1)[:, None] * jnp.tile(inv_freq, LANES // half)
    sign = np.where(np.arange(LANES) % MLA_ROPE < half, -1.0, 1.0).astype(np.float32)
    return jnp.cos(ang), jnp.sin(ang) * sign


def _slot_layout(route_t, counts, n_blocks):
    eid = route_t[0:TOP_K].astype(jnp.int32)
    rank = route_t[2 * TOP_K:3 * TOP_K].astype(jnp.int32)
    counts = counts[0, N_GROUPS:N_GROUPS + N_EXPERTS].astype(jnp.int32)
    padded = (counts + EXPERT_TILE - 1) // EXPERT_TILE * EXPERT_TILE
    pad_end = jnp.cumsum(padded)
    pad_start = pad_end - padded
    experts = jnp.arange(N_EXPERTS, dtype=jnp.int32)
    dest = jnp.sum(jnp.where(eid[None] == experts[:, None, None], pad_start[:, None, None], 0), axis=0) + rank
    blk_start = jnp.arange(n_blocks, dtype=jnp.int32) * EXPERT_TILE
    blk_expert = jnp.minimum(jnp.sum((blk_start[:, None] >= pad_end[None, :]).astype(jnp.int32), axis=1),
                             N_EXPERTS - 1)
    in_expert = jnp.sum(jnp.where(blk_expert[:, None] == experts, counts + pad_start, 0), axis=-1) - blk_start
    blk_rows = jnp.where(blk_start < pad_end[-1], jnp.clip(in_expert, 0, EXPERT_TILE), 0).astype(jnp.int32)
    return dest, blk_expert, blk_rows


def kernel(x, positions, attn_norm, w_in, b_forget, q_norm, w_uq, kv_norm, w_ukv, mla_out_norm, fox_out_norm,
           w_out, ffn_norm, w_router_group, b_router_group, w_router_expert, b_router_expert, w_gate, w_up,
           w_down, final_norm):
    batch, seq, d = x.shape
    n = batch * seq
    depth = w_in.shape[0]
    n_blocks = -(-(n * TOP_K) // EXPERT_TILE) + N_EXPERTS
    cc, ss = _rope_slabs(positions)
    tri = (jnp.arange(CUM_TILE)[:, None] <= jnp.arange(CUM_TILE)[None, :]).astype(BF16)
    sub = min(ROW_TILE, n) // SUB_TILES
    tri_rows = (jnp.arange(sub)[None, :] < jnp.arange(sub)[:, None]).astype(BF16)
    xf = x.reshape(n, d)
    for l in range(depth):
        fq, fk, fv, qn, qpe, kv, kpe, flt = _in_proj(
            xf, attn_norm[l][None, :], _prep_in_weights(w_in[l]), q_norm[l][None, :], _prep_uq(w_uq[l]),
            kv_norm[l][None, :], _prep_ukv(w_ukv[l]), cc, ss)
        ck = _fox_decay(flt, b_forget[l][:, None], tri, batch, seq)
        o_mla = _mla_attn(qn, qpe, kv, kpe, batch, seq).reshape(n, HEAD_W)
        o_fox = _fox_attn(fq, fk, fv, ck, batch, seq).reshape(n, HEAD_W)
        wr, br = _prep_router(w_router_group[l], b_router_group[l], w_router_expert[l], b_router_expert[l])
        x1, h2, route, route_t, counts = _out_proj(
            o_mla, o_fox, xf, mla_out_norm[l][None, :], fox_out_norm[l][None, :], w_out[l].astype(BF16),
            ffn_norm[l][None, :], wr, br, tri_rows)
        dest, blk_expert, blk_rows = _slot_layout(route_t, counts, n_blocks)
        xs = _scatter_rows(h2, [dest[k] for k in range(TOP_K)], n_blocks * EXPERT_TILE)
        ys = _experts(l, blk_expert, blk_rows, xs, w_gate, w_up, w_down)
        final = l == depth - 1
        gain = final_norm[None, :] if final else ffn_norm[l][None, :]
        part_len = n // COMBINE_PARTS
        xf = None
        for part in range(COMBINE_PARTS):
            idx = dest[:, part * part_len:(part + 1) * part_len].reshape(-1)
            xf = _combine(x1, _gather_rows(ys, idx), route, gain, final, part, xf)
    return xf.reshape(batch, seq, d)
```
